```python
import math
import jax, jax.numpy as jnp
from jax import lax
import numpy as np

D_MODEL = 1024
BATCH = 8
SEQ = 2048
DEPTH = 1
DEC_BATCH = 32
DEC_SEQ = 8
PAST_LEN = 16384
PAGE_SIZE = 128

HEAD_DIM = 64
N_ATTN_HEADS = 12
D_ATTN = N_ATTN_HEADS * HEAD_DIM
D_CONV = D_MODEL - D_ATTN
CONV_WIDTH = 3
BRANCHES = ((128, 1), (512, 4), (2048, 16))
W_MAX = 2048
Q_BLOCK = 128
NUM_BUCKETS = 32
MAX_DISTANCE = 2048
N_EXPERTS = 32
TOP_K = 4
D_FF = D_MODEL
SWIGLU_LIMIT = 7.0
SWIGLU_ALPHA = 1.702
EXPERT_BLOCK = 128
DEEPNORM_ALPHA = (2 * DEPTH) ** 0.25
DEEPNORM_BETA = (8 * DEPTH) ** -0.25
LN_EPS = 1e-5
RMS_EPS = 1e-6
NEG_INF = -1e30
D_IN = 3 * D_ATTN + 3 * D_CONV
SPLITS = [D_ATTN, 2 * D_ATTN, 3 * D_ATTN, 3 * D_ATTN + D_CONV, 3 * D_ATTN + 2 * D_CONV]

kernel_name = 'hymba_dilated_conv_moe_step'


def layer_norm(x, g, b):
    xf = x.astype(jnp.float32)
    mu = jnp.mean(xf, axis=-1, keepdims=True)
    var = jnp.mean(jnp.square(xf - mu), axis=-1, keepdims=True)
    y = (xf - mu) * lax.rsqrt(var + LN_EPS)
    return (y * g.astype(jnp.float32) + b.astype(jnp.float32)).astype(x.dtype)


def rms_norm(x, g):
    xf = x.astype(jnp.float32)
    y = xf * lax.rsqrt(jnp.mean(jnp.square(xf), axis=-1, keepdims=True) + RMS_EPS)
    return (y * g.astype(jnp.float32)).astype(x.dtype)


def t5_bucket(dist):
    max_exact = NUM_BUCKETS // 2
    df = jnp.maximum(dist, max_exact).astype(jnp.float32)
    large = max_exact + (jnp.log(df / max_exact) / math.log(MAX_DISTANCE / max_exact)
                         * (NUM_BUCKETS - max_exact)).astype(jnp.int32)
    return jnp.where(dist < max_exact, dist, jnp.minimum(large, NUM_BUCKETS - 1))


def softmax_stats(logits, valid):
    logits = jnp.where(valid, logits, NEG_INF)
    m = jnp.max(logits, axis=-1, keepdims=True)
    p = jnp.exp(logits - m)
    s = jnp.sum(p, axis=-1, keepdims=True)
    return p, s[..., 0], (m + jnp.log(s))[..., 0]


def merge_branches(outs, lses):
    w = jax.nn.softmax(jnp.stack(lses, axis=0), axis=0)
    return jnp.sum(w[..., None] * jnp.stack(outs, axis=0), axis=0)


def dilated_attention_prompt(q, k, v, rel_bias):
    bsz, seq = q.shape[:2]
    pad = ((0, 0), (W_MAX, 0), (0, 0), (0, 0))
    kp = jnp.pad(k, pad)
    vp = jnp.pad(v, pad)
    scale = HEAD_DIM ** -0.5

    def one_block(t0):
        qb = lax.dynamic_slice_in_dim(q, t0, Q_BLOCK, axis=1)
        outs, lses = [], []
        for w, d in BRANCHES:
            nq, nk, nj = Q_BLOCK // d, (w + Q_BLOCK) // d, w // d
            start = t0 + W_MAX - w
            ks = lax.dynamic_slice_in_dim(kp, start, w + Q_BLOCK, axis=1).reshape(bsz, nk, d, N_ATTN_HEADS, HEAD_DIM)
            vs = lax.dynamic_slice_in_dim(vp, start, w + Q_BLOCK, axis=1).reshape(bsz, nk, d, N_ATTN_HEADS, HEAD_DIM)
            qs = qb.reshape(bsz, nq, d, N_ATTN_HEADS, HEAD_DIM)
            step = nj + jnp.arange(nq)[:, None] - jnp.arange(nk)[None, :]
            key_pos = t0 - w + jnp.arange(nk)[None, :] * d + jnp.arange(d)[:, None]
            valid = ((step >= 0) & (step <= nj))[None, :, :] & (key_pos >= 0)[:, None, :]
            bias = rel_bias[t5_bucket(jnp.clip(step, 0, nj) * d)]
            logits = jnp.einsum('bqrhe,bkrhe->bhrqk', qs, ks).astype(jnp.float32) * scale
            logits = logits + jnp.transpose(bias, (2, 0, 1)).astype(jnp.float32)[None, :, None]
            p, s, lse = softmax_stats(logits, valid[None, None])
            o = jnp.einsum('bhrqk,bkrhe->bqrhe', p, vs.astype(jnp.float32))
            s = jnp.transpose(s, (0, 3, 2, 1))
            outs.append((o / s[..., None]).reshape(bsz, Q_BLOCK, N_ATTN_HEADS, HEAD_DIM))
            lses.append(jnp.transpose(lse, (0, 3, 2, 1)).reshape(bsz, Q_BLOCK, N_ATTN_HEADS))
        return merge_branches(outs, lses)

    starts = jnp.arange(seq // Q_BLOCK, dtype=jnp.int32) * Q_BLOCK
    o = lax.map(one_block, starts)
    return jnp.transpose(o, (1, 0, 2, 3, 4)).reshape(bsz, seq, N_ATTN_HEADS, HEAD_DIM).astype(q.dtype)


def dilated_attention_sample(q, k_new, v_new, cache_k, cache_v, rel_bias):
    n_past, n_new = cache_k.shape[1], q.shape[1]
    kc = jnp.concatenate([cache_k, k_new.astype(cache_k.dtype)], axis=1)
    vc = jnp.concatenate([cache_v, v_new.astype(cache_v.dtype)], axis=1)
    scale = HEAD_DIM ** -0.5
    outs, lses = [], []
    for w, d in BRANCHES:
        nj = w // d
        j = jnp.arange(nj + 1)
        idx = n_past + jnp.arange(n_new)[:, None] - j[None, :] * d
        valid = idx >= 0
        kg = kc[:, jnp.maximum(idx, 0)]
        vg = vc[:, jnp.maximum(idx, 0)]
        bias = rel_bias[t5_bucket(j * d)]
        logits = jnp.einsum('bthe,btjhe->bhtj', q, kg).astype(jnp.float32) * scale
        logits = logits + jnp.transpose(bias).astype(jnp.float32)[None, :, None, :]
        p, s, lse = softmax_stats(logits, valid[None, None])
        o = jnp.einsum('bhtj,btjhe->bthe', p, vg.astype(jnp.float32)) / jnp.transpose(s, (0, 2, 1))[..., None]
        outs.append(o)
        lses.append(jnp.transpose(lse, (0, 2, 1)))
    return merge_branches(outs, lses).astype(q.dtype)


def short_conv(z, z_past, conv_w):
    zp = jnp.concatenate([z_past.astype(z.dtype), z], axis=1)
    n = z.shape[1]
    y = conv_w[0] * zp[:, 0:n]
    for i in range(1, CONV_WIDTH):
        y = y + conv_w[i] * zp[:, i:i + n]
    return y, zp[:, -(CONV_WIDTH - 1):]


def moe(u, router_w, router_b, w_up, b_up, w_down, b_down):
    bsz, n, dm = u.shape
    n_tok = bsz * n
    xf = u.reshape(n_tok, dm)
    logits = (xf @ router_w + router_b).astype(jnp.float32)
    top_val, top_idx = lax.top_k(logits, TOP_K)
    gates = jax.nn.softmax(top_val, axis=-1)
    nk = n_tok * TOP_K
    flat_e = top_idx.reshape(-1)
    flat_tok = jnp.repeat(jnp.arange(n_tok, dtype=jnp.int32), TOP_K)
    flat_g = gates.reshape(-1)
    order = jnp.argsort(flat_e)
    sorted_e, sorted_tok, sorted_g = flat_e[order], flat_tok[order], flat_g[order]
    counts = jnp.zeros((N_EXPERTS,), jnp.int32).at[flat_e].add(1)
    padded = ((counts + EXPERT_BLOCK - 1) // EXPERT_BLOCK) * EXPERT_BLOCK
    starts = jnp.cumsum(counts) - counts
    pad_ends = jnp.cumsum(padded)
    pad_starts = pad_ends - padded
    dest = pad_starts[sorted_e] + jnp.arange(nk, dtype=jnp.int32) - starts[sorted_e]
    n_rows = (-(-nk // EXPERT_BLOCK) + N_EXPERTS) * EXPERT_BLOCK
    buf_tok = jnp.full((n_rows,), n_tok, jnp.int32).at[dest].set(sorted_tok)
    buf_g = jnp.zeros((n_rows,), jnp.float32).at[dest].set(sorted_g)
    n_blocks = n_rows // EXPERT_BLOCK
    blk_e = jnp.minimum(jnp.searchsorted(pad_ends, jnp.arange(n_blocks, dtype=jnp.int32) * EXPERT_BLOCK, side='right'),
                        N_EXPERTS - 1)
    x_pad = jnp.concatenate([xf, jnp.zeros((1, dm), xf.dtype)], axis=0)
    xb = x_pad[buf_tok].reshape(n_blocks, EXPERT_BLOCK, dm)

    def expert_block(args):
        xblk, e = args
        hcat = xblk @ w_up[e] + b_up[e]
        x_glu, x_lin = hcat[:, :D_FF], hcat[:, D_FF:]
        x_glu = jnp.minimum(x_glu, SWIGLU_LIMIT)
        x_lin = jnp.clip(x_lin, -SWIGLU_LIMIT, SWIGLU_LIMIT)
        a = x_glu * jax.nn.sigmoid(SWIGLU_ALPHA * x_glu) * (x_lin + 1)
        return a @ w_down[e] + b_down[e]

    yb = lax.map(expert_block, (xb, blk_e)).reshape(n_rows, dm)
    y = jnp.zeros((n_tok + 1, dm), jnp.float32).at[buf_tok].add(yb.astype(jnp.float32) * buf_g[:, None])[:n_tok]
    return y.astype(u.dtype).reshape(bsz, n, dm)


def hybrid_layer(x, c, conv_past, cache_k, cache_v, w_ada, b_ada, w_in, conv_w, norm_attn_g, norm_conv_g, w_o,
                 ln1_g, ln1_b, rel_bias, router_w, router_b, w_up, b_up, w_down, b_down, ln2_g, ln2_b):
    bsz, n = x.shape[:2]
    ada = jax.nn.silu(c) @ w_ada + b_ada
    shift1, scale1, gate1, shift2, scale2, gate2 = jnp.split(ada[:, None, :], 6, axis=-1)
    u = x * (1 + scale1) + shift1
    proj = u @ w_in
    q, k, v, gb, gc, h = jnp.split(proj, SPLITS, axis=-1)
    q = q.reshape(bsz, n, N_ATTN_HEADS, HEAD_DIM)
    k = k.reshape(bsz, n, N_ATTN_HEADS, HEAD_DIM)
    v = v.reshape(bsz, n, N_ATTN_HEADS, HEAD_DIM)
    if cache_k is None:
        o = dilated_attention_prompt(q, k, v, rel_bias)
    else:
        o = dilated_attention_sample(q, k, v, cache_k, cache_v, rel_bias)
    attn = rms_norm(o.reshape(bsz, n, D_ATTN), norm_attn_g)
    yc, conv_state = short_conv(gc * h, conv_past, conv_w)
    conv = rms_norm(gb * yc, norm_conv_g)
    mix = jnp.concatenate([attn, conv], axis=-1) @ w_o
    x = layer_norm(DEEPNORM_ALPHA * x + (1 + gate1) * mix, ln1_g, ln1_b)
    u2 = x * (1 + scale2) + shift2
    ffn = moe(u2, router_w, router_b, w_up, b_up, w_down, b_down)
    x = layer_norm(DEEPNORM_ALPHA * x + (1 + gate2) * ffn, ln2_g, ln2_b)
    return x, k, v, conv_state


def setup_inputs(seed: int = 0) -> dict:
    key = jax.random.key(seed)
    ks = jax.random.split(key, 32)
    f32 = jnp.float32

    def nrm(k, shape, s):
        return s * jax.random.normal(k, shape, f32)

    n_keep = min(W_MAX, PAST_LEN)
    dm = D_MODEL
    return {
        'x_prompt': nrm(ks[0], (BATCH, SEQ, dm), 1.0),
        'x_sample': nrm(ks[1], (DEC_BATCH, DEC_SEQ, dm), 1.0),
        'cache_k': nrm(ks[2], (DEPTH, DEC_BATCH, n_keep, N_ATTN_HEADS, HEAD_DIM), 1.0),
        'cache_v': nrm(ks[3], (DEPTH, DEC_BATCH, n_keep, N_ATTN_HEADS, HEAD_DIM), 1.0),
        'state_conv': nrm(ks[4], (DEPTH, DEC_BATCH, CONV_WIDTH - 1, D_CONV), 1.0),
        'c_prompt': nrm(ks[5], (BATCH, dm), 1.0),
        'c_sample': nrm(ks[6], (DEC_BATCH, dm), 1.0),
        'w_ada': nrm(ks[7], (DEPTH, dm, 6 * dm), 0.2 * dm ** -0.5),
        'b_ada': nrm(ks[8], (DEPTH, 6 * dm), 0.01),
        'w_in': nrm(ks[9], (DEPTH, dm, D_IN), dm ** -0.5),
        'conv_w': nrm(ks[10], (DEPTH, CONV_WIDTH, D_CONV), CONV_WIDTH ** -0.5),
        'norm_attn_g': 1.0 + nrm(ks[11], (DEPTH, D_ATTN), 0.1),
        'norm_conv_g': 1.0 + nrm(ks[12], (DEPTH, D_CONV), 0.1),
        'w_o': nrm(ks[13], (DEPTH, dm, dm), DEEPNORM_BETA * dm ** -0.5),
        'ln1_g': 1.0 + nrm(ks[14], (DEPTH, dm), 0.1),
        'ln1_b': nrm(ks[15], (DEPTH, dm), 0.01),
        'rel_bias': nrm(ks[16], (NUM_BUCKETS, N_ATTN_HEADS), 0.5),
        'router_w': nrm(ks[17], (DEPTH, dm, N_EXPERTS), dm ** -0.5),
        'router_b': nrm(ks[18], (DEPTH, N_EXPERTS), 0.01),
        'w_up': nrm(ks[19], (DEPTH, N_EXPERTS, dm, 2 * D_FF), dm ** -0.5),
        'b_up': nrm(ks[20], (DEPTH, N_EXPERTS, 2 * D_FF), 0.01),
        'w_down': nrm(ks[21], (DEPTH, N_EXPERTS, D_FF, dm), DEEPNORM_BETA * D_FF ** -0.5),
        'b_down': nrm(ks[22], (DEPTH, N_EXPERTS, dm), 0.01),
        'ln2_g': 1.0 + nrm(ks[23], (DEPTH, dm), 0.1),
        'ln2_b': nrm(ks[24], (DEPTH, dm), 0.01),
    }


def reference(x_prompt, x_sample, cache_k, cache_v, state_conv, c_prompt, c_sample, w_ada, b_ada, w_in, conv_w,
              norm_attn_g, norm_conv_g, w_o, ln1_g, ln1_b, rel_bias, router_w, router_b, w_up, b_up, w_down, b_down,
              ln2_g, ln2_b):
    xp, xs = x_prompt, x_sample
    n_keep = min(W_MAX, x_prompt.shape[1])
    kp_l, vp_l, cp_l, ks_l, vs_l, cs_l = [], [], [], [], [], []
    for l in range(DEPTH):
        weights = (w_ada[l], b_ada[l], w_in[l], conv_w[l], norm_attn_g[l], norm_conv_g[l], w_o[l], ln1_g[l], ln1_b[l],
                   rel_bias, router_w[l], router_b[l], w_up[l], b_up[l], w_down[l], b_down[l], ln2_g[l], ln2_b[l])
        conv0 = jnp.zeros((xp.shape[0], CONV_WIDTH - 1, D_CONV), xp.dtype)
        xp, k, v, cst = hybrid_layer(xp, c_prompt, conv0, None, None, *weights)
        kp_l.append(k[:, -n_keep:])
        vp_l.append(v[:, -n_keep:])
        cp_l.append(cst)
        xs, k, v, cst = hybrid_layer(xs, c_sample, state_conv[l], cache_k[l], cache_v[l], *weights)
        ks_l.append(k)
        vs_l.append(v)
        cs_l.append(cst)
    return (xp, xs, jnp.stack(kp_l), jnp.stack(vp_l), jnp.stack(cp_l), jnp.stack(ks_l), jnp.stack(vs_l), jnp.stack(cs_l))
```

```python
import functools
import math

import numpy as np
import jax
import jax.numpy as jnp
from jax import lax
from jax.experimental import pallas as pl
from jax.experimental.pallas import tpu as pltpu

F32 = jnp.float32
BF16 = jnp.bfloat16
I32 = jnp.int32

HEAD_DIM = 64
N_HEADS = 12
D_ATTN = N_HEADS * HEAD_DIM
D_CONV = 256
CONV_WIDTH = 3
BRANCHES = ((128, 1), (512, 4), (2048, 16))
NUM_BUCKETS = 32
MAX_DISTANCE = 2048
N_EXPERTS = 32
TOP_K = 4
SWIGLU_LIMIT = 7.0
SWIGLU_ALPHA = 1.702
LN_EPS = 1e-5
RMS_EPS = 1e-6
NEG_INF = -1e30

V7X_LANES = 128
V7X_SUBLANES = 8
V7X_VMEM_BYTES = 64 * 1024 * 1024

Q_BLK = 128
PAIR = 2 * HEAD_DIM
N_PAIRS = N_HEADS // 2
TOK_TILE = 256
ROW_TILE = 512
EXPERT_BLK = 256


def _cparams(sem, vmem_mb):
    return pltpu.CompilerParams(dimension_semantics=sem, vmem_limit_bytes=vmem_mb * 1024 * 1024)


def _t5_bucket_np(dist):
    dist = np.asarray(dist, np.int64)
    max_exact = NUM_BUCKETS // 2
    df = np.maximum(dist, max_exact).astype(np.float32)
    large = max_exact + (np.log(df / np.float32(max_exact)) / np.float32(math.log(MAX_DISTANCE / max_exact))
                         * np.float32(NUM_BUCKETS - max_exact)).astype(np.int32)
    return np.where(dist < max_exact, dist, np.minimum(large, NUM_BUCKETS - 1)).astype(np.int32)


def _prompt_bucket_index():
    a = np.arange(Q_BLK)[:, None]
    c = np.arange(2 * Q_BLK)[None, :]
    step = Q_BLK + a - c
    valid = (step >= 0) & (step <= Q_BLK)
    out = []
    for _, d in BRANCHES:
        out.append(np.where(valid, _t5_bucket_np(np.clip(step, 0, Q_BLK) * d), -1))
    return np.stack(out).astype(np.int32)


def _sample_bucket_index(n_new, n_past):
    cache = np.full((len(BRANCHES), n_new, Q_BLK), -1, np.int32)
    new = np.full((len(BRANCHES), n_new, n_new), -1, np.int32)
    for bi, (w, d) in enumerate(BRANCHES):
        for t in range(n_new):
            pos_q = n_past + t
            r = t % d
            base = n_past - Q_BLK * d + r
            for m in range(Q_BLK):
                dist = pos_q - (base + d * m)
                if 0 <= dist <= w and dist % d == 0 and base + d * m >= 0:
                    cache[bi, t, m] = _t5_bucket_np(dist)
            for t2 in range(n_new):
                dist = t - t2
                if 0 <= dist <= w and dist % d == 0:
                    new[bi, t, t2] = _t5_bucket_np(dist)
    cache = np.broadcast_to(cache[..., None], cache.shape + (V7X_LANES,))
    new = np.broadcast_to(new[..., None], new.shape + (V7X_LANES,))
    return np.ascontiguousarray(cache), np.ascontiguousarray(new)


def _bias_prompt_body(rb_ref, idx_ref, out_ref):
    idx = idx_ref[0]
    for h in range(N_HEADS):
        acc = jnp.where(idx < 0, NEG_INF, 0.0).astype(F32)
        for b in range(NUM_BUCKETS):
            acc = acc + jnp.where(idx == b, rb_ref[b * N_HEADS + h], 0.0)
        out_ref[0, h] = acc


def _bias_sample_body(rb_ref, idx_ref, out_ref):
    idx = idx_ref[...]
    acc = jnp.where(idx < 0, NEG_INF, 0.0).astype(F32)
    for b in range(NUM_BUCKETS):
        acc = acc + jnp.where(idx == b, rb_ref[b:b + 1, :], 0.0)
    out_ref[...] = acc


def _bias_tables(rel_bias, n_new, n_past):
    idx_p = jnp.asarray(_prompt_bucket_index())
    bias_p = pl.pallas_call(
        _bias_prompt_body,
        grid=(len(BRANCHES),),
        in_specs=[pl.BlockSpec(memory_space=pltpu.SMEM),
                  pl.BlockSpec((1, Q_BLK, 2 * Q_BLK), lambda i: (i, 0, 0))],
        out_specs=pl.BlockSpec((1, N_HEADS, Q_BLK, 2 * Q_BLK), lambda i: (i, 0, 0, 0)),
        out_shape=jax.ShapeDtypeStruct((len(BRANCHES), N_HEADS, Q_BLK, 2 * Q_BLK), F32),
        name="bias_prompt",
    )(rel_bias.reshape(-1), idx_p)
    bias_p = bias_p.reshape(len(BRANCHES), N_PAIRS, 2 * Q_BLK, 2 * Q_BLK)

    ic, inw = _sample_bucket_index(n_new, n_past)
    rb_pad = jnp.pad(rel_bias, ((0, 0), (0, V7X_LANES - N_HEADS)))

    def expand(idx_np):
        flat = jnp.asarray(idx_np.reshape(-1, V7X_LANES))
        return pl.pallas_call(
            _bias_sample_body,
            out_shape=jax.ShapeDtypeStruct(flat.shape, F32),
            name="bias_sample",
        )(rb_pad, flat).reshape(idx_np.shape)

    return bias_p, expand(ic), expand(inw)


def _ada_body(c_ref, w_ref, b_ref, o_ref):
    c = c_ref[...]
    s = c * (1.0 / (1.0 + jnp.exp(-c)))
    o_ref[...] = jnp.dot(s.astype(BF16), w_ref[...].astype(BF16), preferred_element_type=F32) + b_ref[...]


def _ada(c_all, w_ada, b_ada):
    n, dm = c_all.shape
    n_out = w_ada.shape[1]
    tn = dm
    return pl.pallas_call(
        _ada_body,
        grid=(n_out // tn,),
        in_specs=[pl.BlockSpec((n, dm), lambda j: (0, 0)),
                  pl.BlockSpec((dm, tn), lambda j: (0, j)),
                  pl.BlockSpec((1, tn), lambda j: (0, j))],
        out_specs=pl.BlockSpec((n, tn), lambda j: (0, j)),
        out_shape=jax.ShapeDtypeStruct((n, n_out), F32),
        compiler_params=_cparams(("parallel",), 32),
        name="ada",
    )(c_all, w_ada, b_ada.reshape(1, n_out))


_SHIFT1, _SCALE1, _GATE1, _SHIFT2, _SCALE2, _GATE2 = range(6)


def _inproj_body(x_ref, sh_ref, sc_ref, w_ref, cw_ref, ng_ref, past_ref,
                 q_ref, k_ref, v_ref, conv_ref, cst_ref, zz_ref, *, nb, tm, carry):
    dm = x_ref.shape[-1]
    rows = nb * tm
    u = x_ref[...] * (1.0 + sc_ref[...]) + sh_ref[...]
    u = u.reshape(rows, dm).astype(BF16)

    def proj(lo, width):
        return jnp.dot(u, w_ref[:, lo:lo + width], preferred_element_type=F32)

    q_ref[...] = proj(0, D_ATTN).reshape(nb, tm, D_ATTN)
    k_ref[...] = proj(D_ATTN, D_ATTN).reshape(nb, tm, D_ATTN)
    v_ref[...] = proj(2 * D_ATTN, D_ATTN).reshape(nb, tm, D_ATTN)
    gb = proj(3 * D_ATTN, D_CONV)
    gc = proj(3 * D_ATTN + D_CONV, D_CONV)
    hh = proj(3 * D_ATTN + 2 * D_CONV, D_CONV)
    z = (gc * hh).reshape(nb, tm, D_CONV)

    if carry:
        s = pl.program_id(1)

        @pl.when(s == 0)
        def _():
            zz_ref[:, 0:V7X_SUBLANES, :] = jnp.zeros((nb, V7X_SUBLANES, D_CONV), F32)

        @pl.when(s > 0)
        def _():
            zz_ref[:, 0:V7X_SUBLANES, :] = zz_ref[:, tm:tm + V7X_SUBLANES, :]
    else:
        zz_ref[:, V7X_SUBLANES - 2:V7X_SUBLANES, :] = past_ref[...]
    zz_ref[:, V7X_SUBLANES:, :] = z

    cw = cw_ref[...]
    yc = (cw[0:1, :] * zz_ref[:, V7X_SUBLANES - 2:V7X_SUBLANES - 2 + tm, :]
          + cw[1:2, :] * zz_ref[:, V7X_SUBLANES - 1:V7X_SUBLANES - 1 + tm, :]
          + cw[2:3, :] * z)
    g = gb.reshape(nb, tm, D_CONV) * yc
    ms = jnp.mean(g * g, axis=-1, keepdims=True)
    conv_ref[...] = g * lax.rsqrt(ms + RMS_EPS) * ng_ref[...]
    cst_ref[...] = zz_ref[:, tm + V7X_SUBLANES - 2:tm + V7X_SUBLANES, :]


def _inproj(x, ada3, w_in_bf, conv_w, norm_conv_g, past, *, nb, tm):
    bsz, seq, dm = x.shape
    carry = past is None
    if carry:
        past = jnp.zeros((bsz, CONV_WIDTH - 1, D_CONV), F32)
    grid = (bsz // nb, seq // tm)
    d_in = w_in_bf.shape[1]
    body = functools.partial(_inproj_body, nb=nb, tm=tm, carry=carry)
    tile = lambda width: pl.BlockSpec((nb, tm, width), lambda b, s: (b, s, 0))
    return pl.pallas_call(
        body,
        grid=grid,
        in_specs=[tile(dm),
                  pl.BlockSpec((nb, 1, dm), lambda b, s: (b, 0, _SHIFT1)),
                  pl.BlockSpec((nb, 1, dm), lambda b, s: (b, 0, _SCALE1)),
                  pl.BlockSpec((dm, d_in), lambda b, s: (0, 0)),
                  pl.BlockSpec((CONV_WIDTH, D_CONV), lambda b, s: (0, 0)),
                  pl.BlockSpec((1, D_CONV), lambda b, s: (0, 0)),
                  pl.BlockSpec((nb, CONV_WIDTH - 1, D_CONV), lambda b, s: (b, 0, 0))],
        out_specs=[tile(D_ATTN), tile(D_ATTN), tile(D_ATTN), tile(D_CONV),
                   pl.BlockSpec((nb, CONV_WIDTH - 1, D_CONV), lambda b, s: (b, 0, 0))],
        out_shape=[jax.ShapeDtypeStruct((bsz, seq, D_ATTN), F32)] * 3
        + [jax.ShapeDtypeStruct((bsz, seq, D_CONV), F32),
           jax.ShapeDtypeStruct((bsz, CONV_WIDTH - 1, D_CONV), F32)],
        scratch_shapes=[pltpu.VMEM((nb, tm + V7X_SUBLANES, D_CONV), F32)],
        compiler_params=_cparams(("parallel", "arbitrary"), 48),
        name="inproj_prompt" if carry else "inproj_sample",
    )(x, ada3, ada3, w_in_bf, conv_w, norm_conv_g.reshape(1, D_CONV), past)


def _attn_prompt_body(q_ref, k_ref, v_ref, bias_ref, o_ref, ob_ref, lb_ref, *, seq):
    scale = HEAD_DIM ** -0.5
    lane = lax.broadcasted_iota(I32, (Q_BLK, PAIR), 1)
    even = lane < HEAD_DIM
    ones = jnp.ones((Q_BLK, PAIR), BF16)

    def rows(ref, start, d):
        if d == 1:
            return ref[pl.ds(start, Q_BLK), :]
        return ref[pl.ds(start, Q_BLK, stride=d), :]

    def store(ref, br, start, d, val):
        if d == 1:
            ref[br, pl.ds(start, Q_BLK), :] = val
        else:
            ref[br, pl.ds(start, Q_BLK, stride=d), :] = val

    for br, (_, d) in enumerate(BRANCHES):
        nblk = seq // (d * Q_BLK)

        def block(it, carry, br=br, d=d, nblk=nblk):
            bias = bias_ref[br, 0]
            r = it // nblk
            i = it - r * nblk
            cur0 = r + d * Q_BLK * i
            qf = rows(q_ref, cur0, d) * scale
            qs = jnp.concatenate([jnp.where(even, qf, 0.0), jnp.where(even, 0.0, qf)], axis=0).astype(BF16)
            kc = rows(k_ref, cur0, d).astype(BF16)
            vc = rows(v_ref, cur0, d).astype(BF16)
            if nblk == 1:
                s = lax.dot_general(qs, kc, (((1,), (1,)), ((), ())), preferred_element_type=F32)
                s = s + bias[:, Q_BLK:]
                vaug = jnp.concatenate([vc, ones], axis=1)
            else:
                prev0 = r + d * Q_BLK * jnp.maximum(i - 1, 0)
                kp = rows(k_ref, prev0, d).astype(BF16)
                vp = rows(v_ref, prev0, d).astype(BF16)
                kk = jnp.concatenate([kp, kc], axis=0)
                s = lax.dot_general(qs, kk, (((1,), (1,)), ((), ())), preferred_element_type=F32)
                col = lax.broadcasted_iota(I32, (1, 2 * Q_BLK), 1)
                first = jnp.where((col < Q_BLK) & (i == 0), NEG_INF, 0.0)
                s = s + bias + first
                vaug = jnp.concatenate([jnp.concatenate([vp, vc], axis=0),
                                        jnp.concatenate([ones, ones], axis=0)], axis=1)
            m = jnp.max(s, axis=-1, keepdims=True)
            p = jnp.exp(s - m).astype(BF16)
            out = jnp.dot(p, vaug, preferred_element_type=F32)
            num, den = out[:, :PAIR], out[:, PAIR:]
            o = num / den
            lse = m + jnp.log(den)
            store(ob_ref, br, cur0, d, jnp.where(even, o[:Q_BLK], o[Q_BLK:]))
            store(lb_ref, br, cur0, d, jnp.where(even, lse[:Q_BLK], lse[Q_BLK:]))
            return carry

        lax.fori_loop(0, d * nblk, block, 0)

    mt = 2 * Q_BLK

    def merge(i, carry):
        sl = pl.ds(pl.multiple_of(i * mt, mt), mt)
        l0, l1, l2 = lb_ref[0, sl, :], lb_ref[1, sl, :], lb_ref[2, sl, :]
        mx = jnp.maximum(jnp.maximum(l0, l1), l2)
        w0, w1, w2 = jnp.exp(l0 - mx), jnp.exp(l1 - mx), jnp.exp(l2 - mx)
        acc = w0 * ob_ref[0, sl, :] + w1 * ob_ref[1, sl, :] + w2 * ob_ref[2, sl, :]
        o_ref[sl, :] = acc / (w0 + w1 + w2)
        return carry

    lax.fori_loop(0, seq // mt, merge, 0)


def _attn_prompt(q, k, v, bias_p):
    bsz, seq, _ = q.shape
    assert seq % (BRANCHES[-1][1] * Q_BLK) == 0
    body = functools.partial(_attn_prompt_body, seq=seq)
    col = pl.BlockSpec((None, seq, PAIR), lambda b, j: (b, 0, j))
    return pl.pallas_call(
        body,
        grid=(bsz, N_PAIRS),
        in_specs=[col, col, col,
                  pl.BlockSpec((len(BRANCHES), 1, 2 * Q_BLK, 2 * Q_BLK), lambda b, j: (0, j, 0, 0))],
        out_specs=col,
        out_shape=jax.ShapeDtypeStruct((bsz, seq, D_ATTN), F32),
        scratch_shapes=[pltpu.VMEM((len(BRANCHES), seq, PAIR), F32),
                        pltpu.VMEM((len(BRANCHES), seq, PAIR), F32)],
        compiler_params=_cparams(("parallel", "parallel"), 48),
        name="attn_prompt",
    )(q, k, v, bias_p)


def _attn_sample_body(q_ref, kn_ref, vn_ref, ck1, ck4, ck16, cv1, cv4, cv16, bc_ref, bn_ref, o_ref, *, n_new):
    scale = HEAD_DIM ** -0.5
    li = lax.broadcasted_iota(I32, (D_ATTN, V7X_LANES), 0)
    hi = lax.broadcasted_iota(I32, (D_ATTN, V7X_LANES), 1)
    seg = jnp.where((li >= hi * HEAD_DIM) & (li < (hi + 1) * HEAD_DIM), 1.0, 0.0).astype(BF16)
    lt = lax.broadcasted_iota(I32, (V7X_LANES, D_ATTN), 1)
    ht = lax.broadcasted_iota(I32, (V7X_LANES, D_ATTN), 0)
    seg_t = jnp.where((lt >= ht * HEAD_DIM) & (lt < (ht + 1) * HEAD_DIM), 1.0, 0.0).astype(BF16)
    kcache = (ck1, ck4, ck16)
    vcache = (cv1, cv4, cv16)
    kn = kn_ref[...]
    vn = vn_ref[...]
    for t in range(n_new):
        qt = q_ref[t:t + 1, :] * scale
        s_new = jnp.dot((kn * qt).astype(BF16), seg, preferred_element_type=F32)
        ps, lses, dens = [], [], []
        for br, (_, d) in enumerate(BRANCHES):
            lo = (t % d) * D_ATTN
            ks = kcache[br][:, lo:lo + D_ATTN]
            sc = jnp.dot((ks * qt).astype(BF16), seg, preferred_element_type=F32) + bc_ref[br, t]
            sn = s_new + bn_ref[br, t]
            m = jnp.maximum(jnp.max(sc, axis=0, keepdims=True), jnp.max(sn, axis=0, keepdims=True))
            pc = jnp.exp(sc - m)
            pn = jnp.exp(sn - m)
            den = jnp.sum(pc, axis=0, keepdims=True) + jnp.sum(pn, axis=0, keepdims=True)
            ps.append((pc, pn))
            dens.append(den)
            lses.append(m + jnp.log(den))
        mx = jnp.maximum(jnp.maximum(lses[0], lses[1]), lses[2])
        ws = [jnp.exp(l - mx) for l in lses]
        wsum = ws[0] + ws[1] + ws[2]
        acc = jnp.zeros((1, D_ATTN), F32)
        for br, (_, d) in enumerate(BRANCHES):
            lo = (t % d) * D_ATTN
            coef = ws[br] / (wsum * dens[br])
            pc, pn = ps[br]
            pce = jnp.dot((pc * coef).astype(BF16), seg_t, preferred_element_type=F32)
            pne = jnp.dot((pn * coef).astype(BF16), seg_t, preferred_element_type=F32)
            acc = acc + jnp.sum(pce * vcache[br][:, lo:lo + D_ATTN], axis=0, keepdims=True)
            acc = acc + jnp.sum(pne * vn, axis=0, keepdims=True)
        o_ref[t:t + 1, :] = acc


def _attn_sample(q, k_new, v_new, cache_k, cache_v, bias_c, bias_n):
    bsz, n_new, _ = q.shape
    n_past = cache_k.shape[1]
    assert n_past % (BRANCHES[-1][1] * Q_BLK) == 0 and n_new <= V7X_SUBLANES
    ck = cache_k.reshape(bsz, n_past, D_ATTN)
    cv = cache_v.reshape(bsz, n_past, D_ATTN)
    views, specs = [], []
    for c in (ck, cv):
        for _, d in BRANCHES:
            nrow = n_past // d
            views.append(c.reshape(bsz, nrow, d * D_ATTN))
            width = min(d, n_new) * D_ATTN
            specs.append(pl.BlockSpec((None, Q_BLK, width), lambda b, nrow=nrow: (b, nrow // Q_BLK - 1, 0)))
    new = pl.BlockSpec((None, n_new, D_ATTN), lambda b: (b, 0, 0))
    body = functools.partial(_attn_sample_body, n_new=n_new)
    return pl.pallas_call(
        body,
        grid=(bsz,),
        in_specs=[new, new, new] + specs
        + [pl.BlockSpec(bias_c.shape, lambda b: (0, 0, 0, 0)), pl.BlockSpec(bias_n.shape, lambda b: (0, 0, 0, 0))],
        out_specs=new,
        out_shape=jax.ShapeDtypeStruct((bsz, n_new, D_ATTN), F32),
        compiler_params=_cparams(("parallel",), 48),
        name="attn_sample",
    )(q, k_new, v_new, *views, bias_c, bias_n)


def _layer_norm(y, g, b):
    mu = jnp.mean(y, axis=-1, keepdims=True)
    c = y - mu
    var = jnp.mean(c * c, axis=-1, keepdims=True)
    return c * lax.rsqrt(var + LN_EPS) * g + b


def _outproj_body(*refs, nb, tm, alpha, aliased, n_main):
    if aliased:
        refs = refs[2:]
    (a_ref, c_ref, x_ref, g1_ref, sh2_ref, sc2_ref, wa_ref, wc_ref, ng_ref, lg_ref, lb_ref, x1_ref, u2_ref) = refs
    rows = nb * tm

    @pl.when(pl.program_id(0) < n_main)
    def _():
        a = a_ref[...]
        ms = jnp.mean(a * a, axis=-1, keepdims=True)
        an = (a * lax.rsqrt(ms + RMS_EPS) * ng_ref[...]).reshape(rows, D_ATTN).astype(BF16)
        cn = c_ref[...].reshape(rows, D_CONV).astype(BF16)
        mix = (jnp.dot(an, wa_ref[...], preferred_element_type=F32)
               + jnp.dot(cn, wc_ref[...], preferred_element_type=F32))
        dm = mix.shape[-1]
        y = alpha * x_ref[...] + (1.0 + g1_ref[...]) * mix.reshape(nb, tm, dm)
        x1 = _layer_norm(y, lg_ref[...], lb_ref[...])
        x1_ref[...] = x1.reshape(rows, dm)
        u2_ref[...] = (x1 * (1.0 + sc2_ref[...]) + sh2_ref[...]).reshape(rows, dm)

    @pl.when(pl.program_id(0) >= n_main)
    def _():
        x1_ref[...] = jnp.zeros_like(x1_ref)
        u2_ref[...] = jnp.zeros_like(u2_ref)


def _outproj(attn, conv, x, ada3, w_o_bf, norm_attn_g, ln1_g, ln1_b, *, nb, tm, alpha, n_total, row0, prev=None):
    bsz, seq, dm = x.shape
    rows = nb * tm
    assert row0 % rows == 0
    st = seq // tm
    blk0 = row0 // rows
    n_main = (bsz // nb) * st
    aliased = prev is not None
    n_steps = n_main if aliased else pl.cdiv(n_total, rows)

    def bs_of(i):
        j = jnp.minimum(i, n_main - 1)
        return j // st, j % st

    tile = lambda width: pl.BlockSpec((nb, tm, width), lambda i: bs_of(i) + (0,))
    ada = lambda chunk: pl.BlockSpec((nb, 1, dm), lambda i: (bs_of(i)[0], 0, chunk))
    const = lambda shape: pl.BlockSpec(shape, lambda i: (0,) * len(shape))
    out = pl.BlockSpec((rows, dm), lambda i: (blk0 + i, 0))
    body = functools.partial(_outproj_body, nb=nb, tm=tm, alpha=alpha, aliased=aliased, n_main=n_main)
    in_specs = [tile(D_ATTN), tile(D_CONV), tile(dm), ada(_GATE1), ada(_SHIFT2), ada(_SCALE2),
                const((D_ATTN, dm)), const((D_CONV, dm)), const((1, D_ATTN)), const((1, dm)), const((1, dm))]
    args = [attn, conv, x, ada3, ada3, ada3, w_o_bf[:D_ATTN], w_o_bf[D_ATTN:],
            norm_attn_g.reshape(1, D_ATTN), ln1_g.reshape(1, dm), ln1_b.reshape(1, dm)]
    kwargs = {}
    if aliased:
        in_specs = [pl.BlockSpec(memory_space=pl.ANY)] * 2 + in_specs
        args = list(prev) + args
        kwargs["input_output_aliases"] = {0: 0, 1: 1}
    return pl.pallas_call(
        body,
        grid=(n_steps,),
        in_specs=in_specs,
        out_specs=[out, out],
        out_shape=[jax.ShapeDtypeStruct((n_total, dm), F32)] * 2,
        compiler_params=_cparams(("parallel",), 48),
        name="outproj_sample" if aliased else "outproj_prompt",
        **kwargs,
    )(*args)


def _router_body(u_ref, w_ref, b_ref, ri_ref, rg_ref, cnt_ref, run_ref, *, tm):
    i = pl.program_id(0)

    @pl.when(i == 0)
    def _():
        run_ref[...] = jnp.zeros_like(run_ref)

    logits = jnp.dot(u_ref[...], w_ref[...], preferred_element_type=F32,
                     precision=lax.Precision.HIGHEST) + b_ref[...]
    lane_i = lax.broadcasted_iota(I32, (tm, V7X_LANES), 1)
    lane = lane_i.astype(F32)
    vals = logits
    tops, idxs = [], []
    for _ in range(TOP_K):
        mk = jnp.max(vals, axis=-1, keepdims=True)
        ik = jnp.min(jnp.where(vals == mk, lane, float(V7X_LANES)), axis=-1, keepdims=True)
        tops.append(mk)
        idxs.append(ik)
        vals = jnp.where(lane == ik, -jnp.inf, vals)
    es = [jnp.exp(t - tops[0]) for t in tops]
    den = es[0] + es[1] + es[2] + es[3]
    sel = jnp.zeros((tm, V7X_LANES), F32)
    for ik in idxs:
        sel = sel + jnp.where(lane == ik, 1.0, 0.0)
    ri = lax.broadcasted_iota(I32, (tm, tm), 0)
    ci = lax.broadcasted_iota(I32, (tm, tm), 1)
    lower = jnp.where(ci < ri, 1.0, 0.0).astype(BF16)
    before = jnp.dot(lower, sel.astype(BF16), preferred_element_type=F32) + run_ref[...]
    out_i = jnp.zeros((tm, V7X_LANES), I32)
    out_g = jnp.zeros((tm, V7X_LANES), F32)
    for kk in range(TOP_K):
        rank = jnp.sum(jnp.where(lane == idxs[kk], before, 0.0), axis=-1, keepdims=True)
        out_i = jnp.where(lane_i == kk, idxs[kk].astype(I32), out_i)
        out_i = jnp.where(lane_i == TOP_K + kk, rank.astype(I32), out_i)
        out_g = jnp.where(lane_i == kk, es[kk] / den, out_g)
    ri_ref[...] = out_i
    rg_ref[...] = out_g
    run_ref[...] = run_ref[...] + jnp.sum(sel, axis=0, keepdims=True)
    cnt_ref[...] = run_ref[...]


def _router(u2, router_w, router_b, *, tm):
    n, dm = u2.shape
    w_pad = jnp.pad(router_w, ((0, 0), (0, V7X_LANES - N_EXPERTS)))
    b_pad = jnp.pad(router_b, (0, V7X_LANES - N_EXPERTS), constant_values=NEG_INF).reshape(1, V7X_LANES)
    tok = pl.BlockSpec((tm, V7X_LANES), lambda i: (i, 0))
    return pl.pallas_call(
        functools.partial(_router_body, tm=tm),
        grid=(n // tm,),
        in_specs=[pl.BlockSpec((tm, dm), lambda i: (i, 0)),
                  pl.BlockSpec((dm, V7X_LANES), lambda i: (0, 0)),
                  pl.BlockSpec((1, V7X_LANES), lambda i: (0, 0))],
        out_specs=[tok, tok, pl.BlockSpec((1, V7X_LANES), lambda i: (0, 0))],
        out_shape=[jax.ShapeDtypeStruct((n, V7X_LANES), I32), jax.ShapeDtypeStruct((n, V7X_LANES), F32),
                   jax.ShapeDtypeStruct((1, V7X_LANES), F32)],
        scratch_shapes=[pltpu.VMEM((1, V7X_LANES), F32)],
        compiler_params=_cparams(("arbitrary",), 32),
        name="router",
    )(u2, w_pad, b_pad)


def _row_copy_in(src_ref, r, dst_hbm, dst, sem):
    return pltpu.make_async_copy(src_ref.at[pl.ds(r, 1), :], dst_hbm.at[pl.ds(dst, 1), :], sem)


def _dispatch_body(ps_ref, rt_ref, u_ref, xs_in, xs_out, sem, *, tm):
    del xs_in

    def issue(r, carry):
        for kk in range(TOP_K):
            e = rt_ref[0, 0, r * 2 * TOP_K + kk]
            rank = rt_ref[0, 0, r * 2 * TOP_K + TOP_K + kk]
            _row_copy_in(u_ref, r, xs_out, ps_ref[e] + rank, sem).start()
        return carry

    lax.fori_loop(0, tm, issue, 0)

    def drain(r, carry):
        _row_copy_in(u_ref, 0, xs_out, 0, sem).wait()
        return carry

    lax.fori_loop(0, tm * TOP_K, drain, 0)


def _dispatch(u2, route_sm, pad_starts, n_rows, *, tm):
    n, dm = u2.shape
    grid_spec = pltpu.PrefetchScalarGridSpec(
        num_scalar_prefetch=1,
        grid=(n // tm,),
        in_specs=[pl.BlockSpec((1, 1, tm * 2 * TOP_K), lambda i, ps: (i, 0, 0), memory_space=pltpu.SMEM),
                  pl.BlockSpec((tm, dm), lambda i, ps: (i, 0)),
                  pl.BlockSpec(memory_space=pl.ANY)],
        out_specs=pl.BlockSpec(memory_space=pl.ANY),
        scratch_shapes=[pltpu.SemaphoreType.DMA(())],
    )
    return pl.pallas_call(
        functools.partial(_dispatch_body, tm=tm),
        grid_spec=grid_spec,
        out_shape=jax.ShapeDtypeStruct((n_rows, dm), F32),
        input_output_aliases={3: 0},
        compiler_params=_cparams(("arbitrary",), 32),
        name="dispatch",
    )(pad_starts, route_sm, u2, jnp.zeros((n_rows, dm), F32))


def _gmm_body(be_ref, bv_ref, xs_ref, wu_ref, bu_ref, wd_ref, bd_ref, ys_ref, wu_bf, wd_bf):
    j = pl.program_id(0)
    e = be_ref[j]
    prev = be_ref[jnp.maximum(j - 1, 0)]
    d_ff = wd_ref.shape[1]
    chunk = 64

    @pl.when((j == 0) | (e != prev))
    def _():
        def cast(c, carry):
            sl = pl.ds(pl.multiple_of(c * chunk, chunk), chunk)
            wu_bf[sl, :] = wu_ref[0, sl, :].astype(BF16)
            wd_bf[sl, :] = wd_ref[0, sl, :].astype(BF16)
            return carry

        lax.fori_loop(0, wu_ref.shape[1] // chunk, cast, 0)

    @pl.when(bv_ref[j] != 0)
    def _():
        x = xs_ref[...].astype(BF16)
        glu = jnp.dot(x, wu_bf[:, :d_ff], preferred_element_type=F32) + bu_ref[0, :, :d_ff]
        lin = jnp.dot(x, wu_bf[:, d_ff:], preferred_element_type=F32) + bu_ref[0, :, d_ff:]
        glu = jnp.minimum(glu, SWIGLU_LIMIT)
        lin = jnp.clip(lin, -SWIGLU_LIMIT, SWIGLU_LIMIT)
        act = glu * (1.0 / (1.0 + jnp.exp(-SWIGLU_ALPHA * glu))) * (lin + 1.0)
        ys_ref[...] = jnp.dot(act.astype(BF16), wd_bf[...], preferred_element_type=F32) + bd_ref[0]

    @pl.when(bv_ref[j] == 0)
    def _():
        ys_ref[...] = jnp.zeros_like(ys_ref)


def _gmm(xs, blk_e, blk_valid, w_up, b_up, w_down, b_down, *, bm):
    n_rows, dm = xs.shape
    n_e, _, d_up = w_up.shape
    d_ff = w_down.shape[1]
    assert d_ff == dm
    grid_spec = pltpu.PrefetchScalarGridSpec(
        num_scalar_prefetch=2,
        grid=(n_rows // bm,),
        in_specs=[pl.BlockSpec((bm, dm), lambda j, be, bv: (j, 0)),
                  pl.BlockSpec((1, dm, d_up), lambda j, be, bv: (be[j], 0, 0)),
                  pl.BlockSpec((1, 1, d_up), lambda j, be, bv: (be[j], 0, 0)),
                  pl.BlockSpec((1, d_ff, dm), lambda j, be, bv: (be[j], 0, 0)),
                  pl.BlockSpec((1, 1, dm), lambda j, be, bv: (be[j], 0, 0))],
        out_specs=pl.BlockSpec((bm, dm), lambda j, be, bv: (j, 0)),
        scratch_shapes=[pltpu.VMEM((dm, d_up), BF16), pltpu.VMEM((d_ff, dm), BF16)],
    )
    return pl.pallas_call(
        _gmm_body,
        grid_spec=grid_spec,
        out_shape=jax.ShapeDtypeStruct((n_rows, dm), F32),
        compiler_params=_cparams(("arbitrary",), 56),
        name="expert_mlp",
    )(blk_e, blk_valid, xs, w_up, b_up.reshape(n_e, 1, d_up), w_down, b_down.reshape(n_e, 1, dm))


def _row_copy_out(src_hbm, src, dst_ref, kk, r, sem):
    return pltpu.make_async_copy(src_hbm.at[pl.ds(src, 1), :], dst_ref.at[kk, pl.ds(r, 1), :], sem)


def _combine_body(ps_ref, rt_ref, ys_hbm, rg_ref, x1_ref, g2_ref, lg_ref, lb_ref, y_ref, rows_ref, sem,
                  *, nb, tm, alpha):
    n_tok = nb * tm

    def issue(r, carry):
        for kk in range(TOP_K):
            e = rt_ref[0, 0, r * 2 * TOP_K + kk]
            rank = rt_ref[0, 0, r * 2 * TOP_K + TOP_K + kk]
            _row_copy_out(ys_hbm, ps_ref[e] + rank, rows_ref, kk, r, sem).start()
        return carry

    lax.fori_loop(0, n_tok, issue, 0)

    def drain(r, carry):
        _row_copy_out(ys_hbm, 0, rows_ref, 0, 0, sem).wait()
        return carry

    lax.fori_loop(0, n_tok * TOP_K, drain, 0)

    gates = rg_ref[...]
    ffn = gates[:, 0:1] * rows_ref[0]
    for kk in range(1, TOP_K):
        ffn = ffn + gates[:, kk:kk + 1] * rows_ref[kk]
    dm = ffn.shape[-1]
    y = alpha * x1_ref[...].reshape(nb, tm, dm) + (1.0 + g2_ref[...]) * ffn.reshape(nb, tm, dm)
    y_ref[...] = _layer_norm(y, lg_ref[...], lb_ref[...])


def _combine(ys, route_sm, route_g, pad_starts, x1_all, ada3, ln2_g, ln2_b, *, bsz, seq, nb, tm, alpha, row0):
    dm = x1_all.shape[1]
    n_tok = nb * tm
    assert row0 % n_tok == 0 and route_sm.shape[2] == n_tok * 2 * TOP_K
    blk0 = row0 // n_tok
    st = seq // tm
    grid_spec = pltpu.PrefetchScalarGridSpec(
        num_scalar_prefetch=1,
        grid=(bsz // nb, st),
        in_specs=[pl.BlockSpec((1, 1, n_tok * 2 * TOP_K), lambda b, s, ps: (blk0 + b * st + s, 0, 0),
                               memory_space=pltpu.SMEM),
                  pl.BlockSpec(memory_space=pl.ANY),
                  pl.BlockSpec((n_tok, V7X_LANES), lambda b, s, ps: (blk0 + b * st + s, 0)),
                  pl.BlockSpec((n_tok, dm), lambda b, s, ps: (blk0 + b * st + s, 0)),
                  pl.BlockSpec((nb, 1, dm), lambda b, s, ps: (b, 0, _GATE2)),
                  pl.BlockSpec((1, dm), lambda b, s, ps: (0, 0)),
                  pl.BlockSpec((1, dm), lambda b, s, ps: (0, 0))],
        out_specs=pl.BlockSpec((nb, tm, dm), lambda b, s, ps: (b, s, 0)),
        scratch_shapes=[pltpu.VMEM((TOP_K, n_tok, dm), F32), pltpu.SemaphoreType.DMA(())],
    )
    return pl.pallas_call(
        functools.partial(_combine_body, nb=nb, tm=tm, alpha=alpha),
        grid_spec=grid_spec,
        out_shape=jax.ShapeDtypeStruct((bsz, seq, dm), F32),
        compiler_params=_cparams(("arbitrary", "arbitrary"), 48),
        name="combine_prompt" if row0 == 0 else "combine_sample",
    )(pad_starts, route_sm, ys, route_g, x1_all, ada3, ln2_g.reshape(1, dm), ln2_b.reshape(1, dm))


def _layer(xp, xs, cache_k, cache_v, conv_past, cp, cs, w_ada, b_ada, w_in, conv_w, norm_attn_g, norm_conv_g,
           w_o, ln1_g, ln1_b, bias_tabs, router_w, router_b, w_up, b_up, w_down, b_down, ln2_g, ln2_b, alpha):
    bp, sp, dm = xp.shape
    bs, ts, _ = xs.shape
    n_p, n_s = bp * sp, bs * ts
    n_tok = n_p + n_s
    assert n_p % TOK_TILE == 0 and n_s == TOK_TILE and sp % ROW_TILE == 0
    bias_p, bias_c, bias_n = bias_tabs

    ada = _ada(jnp.concatenate([cp, cs], axis=0), w_ada, b_ada)
    ada_p = ada[:bp].reshape(bp, 1, -1)
    ada_s = ada[bp:].reshape(bs, 1, -1)
    w_in_bf = w_in.astype(BF16)
    w_o_bf = w_o.astype(BF16)

    qp, kp, vp, convp, cstp = _inproj(xp, ada_p, w_in_bf, conv_w, norm_conv_g, None, nb=1, tm=ROW_TILE)
    attn_p = _attn_prompt(qp, kp, vp, bias_p)
    x1_all, u2_all = _outproj(attn_p, convp, xp, ada_p, w_o_bf, norm_attn_g, ln1_g, ln1_b,
                              nb=1, tm=ROW_TILE, alpha=alpha, n_total=n_tok, row0=0)
    qs, ks, vs, convs, csts = _inproj(xs, ada_s, w_in_bf, conv_w, norm_conv_g, conv_past, nb=bs, tm=ts)
    attn_s = _attn_sample(qs, ks, vs, cache_k, cache_v, bias_c, bias_n)
    x1_all, u2_all = _outproj(attn_s, convs, xs, ada_s, w_o_bf, norm_attn_g, ln1_g, ln1_b,
                              nb=bs, tm=ts, alpha=alpha, n_total=n_tok, row0=n_p, prev=(x1_all, u2_all))

    route_i, route_g, counts = _router(u2_all, router_w, router_b, tm=TOK_TILE)
    counts = counts[0, :N_EXPERTS].astype(I32)
    padded = ((counts + EXPERT_BLK - 1) // EXPERT_BLK) * EXPERT_BLK
    pad_ends = jnp.cumsum(padded)
    pad_starts = (pad_ends - padded).astype(I32)
    n_blocks = n_tok * TOP_K // EXPERT_BLK + N_EXPERTS
    n_rows = n_blocks * EXPERT_BLK
    blk_row = jnp.arange(n_blocks, dtype=I32) * EXPERT_BLK
    blk_valid = (blk_row < pad_ends[-1]).astype(I32)
    blk_e = jnp.sum((blk_row[:, None] >= pad_ends[None, :]).astype(I32), axis=1)
    last_e = jnp.sum((pad_ends[-1] - 1 >= pad_ends).astype(I32))
    blk_e = jnp.where(blk_valid != 0, blk_e, last_e).astype(I32)
    route_sm = route_i[:, :2 * TOP_K].reshape(n_tok // TOK_TILE, 1, TOK_TILE * 2 * TOP_K)

    x_sorted = _dispatch(u2_all, route_sm, pad_starts, n_rows, tm=TOK_TILE)
    y_sorted = _gmm(x_sorted, blk_e, blk_valid, w_up, b_up, w_down, b_down, bm=EXPERT_BLK)
    yp = _combine(y_sorted, route_sm, route_g, pad_starts, x1_all, ada_p, ln2_g, ln2_b,
                  bsz=bp, seq=sp, nb=1, tm=TOK_TILE, alpha=alpha, row0=0)
    ys_out = _combine(y_sorted, route_sm, route_g, pad_starts, x1_all, ada_s, ln2_g, ln2_b,
                      bsz=bs, seq=ts, nb=bs, tm=ts, alpha=alpha, row0=n_p)
    return yp, ys_out, kp, vp, cstp, ks, vs, csts


def kernel(x_prompt, x_sample, cache_k, cache_v, state_conv, c_prompt, c_sample, w_ada, b_ada, w_in, conv_w,
           norm_attn_g, norm_conv_g, w_o, ln1_g, ln1_b, rel_bias, router_w, router_b, w_up, b_up, w_down, b_down,
           ln2_g, ln2_b):
    depth = w_ada.shape[0]
    alpha = (2 * depth) ** 0.25
    xp, xs = x_prompt, x_sample
    bp, sp, _ = xp.shape
    bs, ts, _ = xs.shape
    n_keep = min(BRANCHES[-1][0], sp)
    bias_tabs = _bias_tables(rel_bias, ts, cache_k.shape[2])
    outs = [[] for _ in range(6)]
    for l in range(depth):
        xp, xs, kp, vp, cstp, ks, vs, csts = _layer(
            xp, xs, cache_k[l], cache_v[l], state_conv[l], c_prompt, c_sample, w_ada[l], b_ada[l], w_in[l],
            conv_w[l], norm_attn_g[l], norm_conv_g[l], w_o[l], ln1_g[l], ln1_b[l], bias_tabs, router_w[l],
            router_b[l], w_up[l], b_up[l], w_down[l], b_down[l], ln2_g[l], ln2_b[l], alpha)
        kp = kp.reshape(bp, sp, N_HEADS, HEAD_DIM)[:, -n_keep:]
        vp = vp.reshape(bp, sp, N_HEADS, HEAD_DIM)[:, -n_keep:]
        for lst, val in zip(outs, (kp, vp, cstp, ks.reshape(bs, ts, N_HEADS, HEAD_DIM),
                                   vs.reshape(bs, ts, N_HEADS, HEAD_DIM), csts)):
            lst.append(val)
    return (xp, xs) + tuple(jnp.stack(o) for o in outs)
```

```python
import functools
import math

import numpy as np
import jax
import jax.numpy as jnp
from jax import lax
from jax.experimental import pallas as pl
from jax.experimental.pallas import tpu as pltpu

F32 = jnp.float32
BF16 = jnp.bfloat16
I32 = jnp.int32

HEAD_DIM = 64
N_HEADS = 12
D_ATTN = N_HEADS * HEAD_DIM
D_CONV = 256
CONV_WIDTH = 3
BRANCHES = ((128, 1), (512, 4), (2048, 16))
NUM_BUCKETS = 32
MAX_DISTANCE = 2048
N_EXPERTS = 32
TOP_K = 4
SWIGLU_LIMIT = 7.0
SWIGLU_ALPHA = 1.702
LN_EPS = 1e-5
RMS_EPS = 1e-6
NEG_INF = -1e30

V7X_LANES = 128
V7X_SUBLANES = 8
V7X_VMEM_BYTES = 64 * 1024 * 1024

Q_BLK = 128
PAIR = 2 * HEAD_DIM
N_PAIRS = N_HEADS // 2
TOK_TILE = 256
ROW_TILE = 512
EXPERT_BLK = 256


def _cparams(sem, vmem_mb):
    return pltpu.CompilerParams(dimension_semantics=sem, vmem_limit_bytes=vmem_mb * 1024 * 1024)


def _t5_bucket_np(dist):
    dist = np.asarray(dist, np.int64)
    max_exact = NUM_BUCKETS // 2
    df = np.maximum(dist, max_exact).astype(np.float32)
    large = max_exact + (np.log(df / np.float32(max_exact)) / np.float32(math.log(MAX_DISTANCE / max_exact))
                         * np.float32(NUM_BUCKETS - max_exact)).astype(np.int32)
    return np.where(dist < max_exact, dist, np.minimum(large, NUM_BUCKETS - 1)).astype(np.int32)


def _prompt_bucket_index():
    a = np.arange(Q_BLK)[:, None]
    c = np.arange(2 * Q_BLK)[None, :]
    step = Q_BLK + a - c
    valid = (step >= 0) & (step <= Q_BLK)
    out = []
    for _, d in BRANCHES:
        out.append(np.where(valid, _t5_bucket_np(np.clip(step, 0, Q_BLK) * d), -1))
    return np.stack(out).astype(np.int32)


def _sample_bucket_index(n_new, n_past):
    t = np.arange(n_new)[:, None]
    out_c, out_n = [], []
    for w, d in BRANCHES:
        dist_c = n_past + t - np.arange(n_past)[None, :]
        dist_n = t - np.arange(n_new)[None, :]
        for dist, out in ((dist_c, out_c), (dist_n, out_n)):
            valid = (dist >= 0) & (dist <= w) & (dist % d == 0)
            out.append(np.where(valid, _t5_bucket_np(np.clip(dist, 0, w)), -1))
    return np.stack(out_c).astype(np.int32), np.stack(out_n).astype(np.int32)


def _bias_body(rb_ref, idx_ref, out_ref):
    idx = idx_ref[0]
    for h in range(N_HEADS):
        acc = jnp.where(idx < 0, NEG_INF, 0.0).astype(F32)
        for b in range(NUM_BUCKETS):
            acc = acc + jnp.where(idx == b, rb_ref[b * N_HEADS + h], 0.0)
        out_ref[0, h] = acc


def _bias_expand(rel_bias, idx_np, name):
    nbr, r, c = idx_np.shape
    return pl.pallas_call(
        _bias_body,
        grid=(nbr,),
        in_specs=[pl.BlockSpec(memory_space=pltpu.SMEM),
                  pl.BlockSpec((1, r, c), lambda i: (i, 0, 0))],
        out_specs=pl.BlockSpec((1, N_HEADS, r, c), lambda i: (i, 0, 0, 0)),
        out_shape=jax.ShapeDtypeStruct((nbr, N_HEADS, r, c), F32),
        name=name,
    )(rel_bias.reshape(-1), jnp.asarray(idx_np))


def _bias_tables(rel_bias, n_new, n_past):
    bias_p = _bias_expand(rel_bias, _prompt_bucket_index(), "bias_prompt")
    bias_p = bias_p.reshape(len(BRANCHES), N_PAIRS, 2 * Q_BLK, 2 * Q_BLK)
    ic, inw = _sample_bucket_index(n_new, n_past)
    return bias_p, _bias_expand(rel_bias, ic, "bias_cache"), _bias_expand(rel_bias, inw, "bias_new")


def _ada_body(c_ref, w_ref, b_ref, o_ref):
    c = c_ref[...]
    s = c * (1.0 / (1.0 + jnp.exp(-c)))
    o_ref[...] = jnp.dot(s.astype(BF16), w_ref[...].astype(BF16), preferred_element_type=F32) + b_ref[...]


def _ada(c_all, w_ada, b_ada):
    n, dm = c_all.shape
    n_out = w_ada.shape[1]
    tn = dm
    return pl.pallas_call(
        _ada_body,
        grid=(n_out // tn,),
        in_specs=[pl.BlockSpec((n, dm), lambda j: (0, 0)),
                  pl.BlockSpec((dm, tn), lambda j: (0, j)),
                  pl.BlockSpec((1, tn), lambda j: (0, j))],
        out_specs=pl.BlockSpec((n, tn), lambda j: (0, j)),
        out_shape=jax.ShapeDtypeStruct((n, n_out), F32),
        compiler_params=_cparams(("parallel",), 32),
        name="ada",
    )(c_all, w_ada, b_ada.reshape(1, n_out))


_SHIFT1, _SCALE1, _GATE1, _SHIFT2, _SCALE2, _GATE2 = range(6)


def _inproj_body(*refs, nb, tm, carry):
    if carry:
        (x_ref, sh_ref, sc_ref, w_ref, cw_ref, ng_ref, past_ref, wt_ref,
         q_ref, k_ref, v_ref, conv_ref, cst_ref, kt_ref, vt_ref, zz_ref) = refs
    else:
        (x_ref, sh_ref, sc_ref, w_ref, cw_ref, ng_ref, past_ref,
         q_ref, k_ref, v_ref, conv_ref, cst_ref, zz_ref) = refs
    dm = x_ref.shape[-1]
    rows = nb * tm
    u = x_ref[...] * (1.0 + sc_ref[...]) + sh_ref[...]
    u = u.reshape(rows, dm).astype(BF16)

    def proj(lo, width):
        return jnp.dot(u, w_ref[:, lo:lo + width], preferred_element_type=F32)

    q_ref[...] = proj(0, D_ATTN).reshape(nb, tm, D_ATTN)
    k_ref[...] = proj(D_ATTN, D_ATTN).reshape(nb, tm, D_ATTN)
    v_ref[...] = proj(2 * D_ATTN, D_ATTN).reshape(nb, tm, D_ATTN)
    if carry:
        nt = (((1,), (1,)), ((), ()))
        kt_ref[0] = lax.dot_general(wt_ref[D_ATTN:2 * D_ATTN, :], u, nt, preferred_element_type=F32)
        vt_ref[0] = lax.dot_general(wt_ref[2 * D_ATTN:3 * D_ATTN, :], u, nt, preferred_element_type=F32)
    gb = proj(3 * D_ATTN, D_CONV)
    gc = proj(3 * D_ATTN + D_CONV, D_CONV)
    hh = proj(3 * D_ATTN + 2 * D_CONV, D_CONV)
    z = (gc * hh).reshape(nb, tm, D_CONV)

    if carry:
        s = pl.program_id(1)

        @pl.when(s == 0)
        def _():
            zz_ref[:, 0:V7X_SUBLANES, :] = jnp.zeros((nb, V7X_SUBLANES, D_CONV), F32)

        @pl.when(s > 0)
        def _():
            zz_ref[:, 0:V7X_SUBLANES, :] = zz_ref[:, tm:tm + V7X_SUBLANES, :]
    else:
        zz_ref[:, V7X_SUBLANES - 2:V7X_SUBLANES, :] = past_ref[...]
    zz_ref[:, V7X_SUBLANES:, :] = z

    cw = cw_ref[...]
    yc = (cw[0:1, :] * zz_ref[:, V7X_SUBLANES - 2:V7X_SUBLANES - 2 + tm, :]
          + cw[1:2, :] * zz_ref[:, V7X_SUBLANES - 1:V7X_SUBLANES - 1 + tm, :]
          + cw[2:3, :] * z)
    g = gb.reshape(nb, tm, D_CONV) * yc
    ms = jnp.mean(g * g, axis=-1, keepdims=True)
    conv_ref[...] = g * lax.rsqrt(ms + RMS_EPS) * ng_ref[...]
    cst_ref[...] = zz_ref[:, tm + V7X_SUBLANES - 2:tm + V7X_SUBLANES, :]


def _inproj(x, ada3, w_in_bf, conv_w, norm_conv_g, past, *, nb, tm):
    bsz, seq, dm = x.shape
    carry = past is None
    if carry:
        past = jnp.zeros((bsz, CONV_WIDTH - 1, D_CONV), F32)
    grid = (bsz // nb, seq // tm)
    d_in = w_in_bf.shape[1]
    body = functools.partial(_inproj_body, nb=nb, tm=tm, carry=carry)
    tile = lambda width: pl.BlockSpec((nb, tm, width), lambda b, s: (b, s, 0))
    in_specs = [tile(dm),
                pl.BlockSpec((nb, 1, dm), lambda b, s: (b, 0, _SHIFT1)),
                pl.BlockSpec((nb, 1, dm), lambda b, s: (b, 0, _SCALE1)),
                pl.BlockSpec((dm, d_in), lambda b, s: (0, 0)),
                pl.BlockSpec((CONV_WIDTH, D_CONV), lambda b, s: (0, 0)),
                pl.BlockSpec((1, D_CONV), lambda b, s: (0, 0)),
                pl.BlockSpec((nb, CONV_WIDTH - 1, D_CONV), lambda b, s: (b, 0, 0))]
    args = [x, ada3, ada3, w_in_bf, conv_w, norm_conv_g.reshape(1, D_CONV), past]
    out_specs = [tile(D_ATTN), tile(D_ATTN), tile(D_ATTN), tile(D_CONV),
                 pl.BlockSpec((nb, CONV_WIDTH - 1, D_CONV), lambda b, s: (b, 0, 0))]
    out_shape = ([jax.ShapeDtypeStruct((bsz, seq, D_ATTN), F32)] * 3
                 + [jax.ShapeDtypeStruct((bsz, seq, D_CONV), F32),
                    jax.ShapeDtypeStruct((bsz, CONV_WIDTH - 1, D_CONV), F32)])
    if carry:
        assert nb == 1
        in_specs.append(pl.BlockSpec((d_in, dm), lambda b, s: (0, 0)))
        args.append(w_in_bf.T)
        out_specs += [pl.BlockSpec((1, D_ATTN, tm), lambda b, s: (b, 0, s))] * 2
        out_shape += [jax.ShapeDtypeStruct((bsz, D_ATTN, seq), F32)] * 2
    return pl.pallas_call(
        body,
        grid=grid,
        in_specs=in_specs,
        out_specs=out_specs,
        out_shape=out_shape,
        scratch_shapes=[pltpu.VMEM((nb, tm + V7X_SUBLANES, D_CONV), F32)],
        compiler_params=_cparams(("parallel", "arbitrary"), 56),
        name="inproj_prompt" if carry else "inproj_sample",
    )(*args)


def _attn_prompt_body(q_ref, k_ref, v_ref, bias_ref, o_ref, ob_ref, lb_ref, *, seq):
    scale = HEAD_DIM ** -0.5
    lane = lax.broadcasted_iota(I32, (Q_BLK, PAIR), 1)
    even = lane < HEAD_DIM
    ones = jnp.ones((Q_BLK, PAIR), BF16)

    def rows(ref, start, d):
        if d == 1:
            return ref[pl.ds(start, Q_BLK), :]
        return ref[pl.ds(start, Q_BLK, stride=d), :]

    def store(ref, br, start, d, val):
        if d == 1:
            ref[br, pl.ds(start, Q_BLK), :] = val
        else:
            ref[br, pl.ds(start, Q_BLK, stride=d), :] = val

    for br, (_, d) in enumerate(BRANCHES):
        nblk = seq // (d * Q_BLK)

        def block(it, carry, br=br, d=d, nblk=nblk):
            bias = bias_ref[br, 0]
            r = it // nblk
            i = it - r * nblk
            cur0 = r + d * Q_BLK * i
            qf = rows(q_ref, cur0, d) * scale
            qs = jnp.concatenate([jnp.where(even, qf, 0.0), jnp.where(even, 0.0, qf)], axis=0).astype(BF16)
            kc = rows(k_ref, cur0, d).astype(BF16)
            vc = rows(v_ref, cur0, d).astype(BF16)
            if nblk == 1:
                s = lax.dot_general(qs, kc, (((1,), (1,)), ((), ())), preferred_element_type=F32)
                s = s + bias[:, Q_BLK:]
                vaug = jnp.concatenate([vc, ones], axis=1)
            else:
                prev0 = r + d * Q_BLK * jnp.maximum(i - 1, 0)
                kp = rows(k_ref, prev0, d).astype(BF16)
                vp = rows(v_ref, prev0, d).astype(BF16)
                kk = jnp.concatenate([kp, kc], axis=0)
                s = lax.dot_general(qs, kk, (((1,), (1,)), ((), ())), preferred_element_type=F32)
                col = lax.broadcasted_iota(I32, (1, 2 * Q_BLK), 1)
                first = jnp.where((col < Q_BLK) & (i == 0), NEG_INF, 0.0)
                s = s + bias + first
                vaug = jnp.concatenate([jnp.concatenate([vp, vc], axis=0),
                                        jnp.concatenate([ones, ones], axis=0)], axis=1)
            m = jnp.max(s, axis=-1, keepdims=True)
            p = jnp.exp(s - m).astype(BF16)
            out = jnp.dot(p, vaug, preferred_element_type=F32)
            num, den = out[:, :PAIR], out[:, PAIR:]
            o = num / den
            lse = m + jnp.log(den)
            store(ob_ref, br, cur0, d, jnp.where(even, o[:Q_BLK], o[Q_BLK:]))
            store(lb_ref, br, cur0, d, jnp.where(even, lse[:Q_BLK], lse[Q_BLK:]))
            return carry

        lax.fori_loop(0, d * nblk, block, 0)

    mt = 2 * Q_BLK

    def merge(i, carry):
        sl = pl.ds(pl.multiple_of(i * mt, mt), mt)
        l0, l1, l2 = lb_ref[0, sl, :], lb_ref[1, sl, :], lb_ref[2, sl, :]
        mx = jnp.maximum(jnp.maximum(l0, l1), l2)
        w0, w1, w2 = jnp.exp(l0 - mx), jnp.exp(l1 - mx), jnp.exp(l2 - mx)
        acc = w0 * ob_ref[0, sl, :] + w1 * ob_ref[1, sl, :] + w2 * ob_ref[2, sl, :]
        o_ref[sl, :] = acc / (w0 + w1 + w2)
        return carry

    lax.fori_loop(0, seq // mt, merge, 0)


def _attn_prompt(q, k, v, bias_p):
    bsz, seq, _ = q.shape
    assert seq % (BRANCHES[-1][1] * Q_BLK) == 0
    body = functools.partial(_attn_prompt_body, seq=seq)
    col = pl.BlockSpec((None, seq, PAIR), lambda b, j: (b, 0, j))
    return pl.pallas_call(
        body,
        grid=(bsz, N_PAIRS),
        in_specs=[col, col, col,
                  pl.BlockSpec((len(BRANCHES), 1, 2 * Q_BLK, 2 * Q_BLK), lambda b, j: (0, j, 0, 0))],
        out_specs=col,
        out_shape=jax.ShapeDtypeStruct((bsz, seq, D_ATTN), F32),
        scratch_shapes=[pltpu.VMEM((len(BRANCHES), seq, PAIR), F32),
                        pltpu.VMEM((len(BRANCHES), seq, PAIR), F32)],
        compiler_params=_cparams(("parallel", "parallel"), 48),
        name="attn_prompt",
    )(q, k, v, bias_p)


def _attn_sample_body(q_ref, kn_ref, vn_ref, ckt_ref, cvt_ref, bc_ref, bn_ref, o_ref):
    scale = HEAD_DIM ** -0.5
    nt = (((1,), (1,)), ((), ()))
    nbr = len(BRANCHES)
    for h in range(N_HEADS):
        rows = slice(h * HEAD_DIM, (h + 1) * HEAD_DIM)
        qh = (q_ref[h] * scale).astype(BF16)
        s_c = jnp.dot(qh, ckt_ref[rows, :].astype(BF16), preferred_element_type=F32)
        s_n = lax.dot_general(qh, kn_ref[h].astype(BF16), nt, preferred_element_type=F32)
        ps, lses, dens = [], [], []
        for br in range(nbr):
            lc = s_c + bc_ref[br, h]
            ln = s_n + bn_ref[br, h]
            m = jnp.maximum(jnp.max(lc, axis=-1, keepdims=True), jnp.max(ln, axis=-1, keepdims=True))
            pc = jnp.exp(lc - m)
            pn = jnp.exp(ln - m)
            den = jnp.sum(pc, axis=-1, keepdims=True) + jnp.sum(pn, axis=-1, keepdims=True)
            ps.append((pc, pn))
            dens.append(den)
            lses.append(m + jnp.log(den))
        mx = jnp.maximum(jnp.maximum(lses[0], lses[1]), lses[2])
        ws = [jnp.exp(l - mx) for l in lses]
        wsum = ws[0] + ws[1] + ws[2]
        coefs = [ws[br] / (wsum * dens[br]) for br in range(nbr)]
        p_c = coefs[0] * ps[0][0] + coefs[1] * ps[1][0] + coefs[2] * ps[2][0]
        p_n = coefs[0] * ps[0][1] + coefs[1] * ps[1][1] + coefs[2] * ps[2][1]
        o = lax.dot_general(p_c.astype(BF16), cvt_ref[rows, :].astype(BF16), nt, preferred_element_type=F32)
        o_ref[h] = o + jnp.dot(p_n.astype(BF16), vn_ref[h].astype(BF16), preferred_element_type=F32)


def _attn_sample(q, k_new, v_new, cache_kt, cache_vt, bias_c, bias_n):
    bsz, _, n_new, _ = q.shape
    n_past = cache_kt.shape[2]
    new = pl.BlockSpec((None, N_HEADS, n_new, HEAD_DIM), lambda b: (b, 0, 0, 0))
    cache = pl.BlockSpec((None, D_ATTN, n_past), lambda b: (b, 0, 0))
    return pl.pallas_call(
        _attn_sample_body,
        grid=(bsz,),
        in_specs=[new, new, new, cache, cache,
                  pl.BlockSpec(bias_c.shape, lambda b: (0, 0, 0, 0)), pl.BlockSpec(bias_n.shape, lambda b: (0, 0, 0, 0))],
        out_specs=new,
        out_shape=jax.ShapeDtypeStruct((bsz, N_HEADS, n_new, HEAD_DIM), F32),
        compiler_params=_cparams(("parallel",), 48),
        name="attn_sample",
    )(q, k_new, v_new, cache_kt, cache_vt, bias_c, bias_n)


def _layer_norm(y, g, b):
    mu = jnp.mean(y, axis=-1, keepdims=True)
    c = y - mu
    var = jnp.mean(c * c, axis=-1, keepdims=True)
    return c * lax.rsqrt(var + LN_EPS) * g + b


def _to_row_tiles(ref, val):
    rows, dm = val.shape
    assert dm == V7X_SUBLANES * V7X_LANES
    for c in range(V7X_SUBLANES):
        ref[pl.ds(c, rows, stride=V7X_SUBLANES), :] = val[:, c * V7X_LANES:(c + 1) * V7X_LANES]


def _from_row_tiles(ref, rows, lead=()):
    chunks = [ref[lead + (pl.ds(c, rows, stride=V7X_SUBLANES), slice(None))] for c in range(V7X_SUBLANES)]
    return jnp.concatenate(chunks, axis=1)


def _outproj_body(*refs, nb, tm, alpha, aliased, n_main):
    if aliased:
        refs = refs[2:]
    (a_ref, c_ref, x_ref, g1_ref, sh2_ref, sc2_ref, wa_ref, wc_ref, ng_ref, lg_ref, lb_ref, x1_ref, u2_ref) = refs
    rows = nb * tm

    @pl.when(pl.program_id(0) < n_main)
    def _():
        a = a_ref[...]
        ms = jnp.mean(a * a, axis=-1, keepdims=True)
        an = (a * lax.rsqrt(ms + RMS_EPS) * ng_ref[...]).reshape(rows, D_ATTN).astype(BF16)
        cn = c_ref[...].reshape(rows, D_CONV).astype(BF16)
        mix = (jnp.dot(an, wa_ref[...], preferred_element_type=F32)
               + jnp.dot(cn, wc_ref[...], preferred_element_type=F32))
        dm = mix.shape[-1]
        y = alpha * x_ref[...] + (1.0 + g1_ref[...]) * mix.reshape(nb, tm, dm)
        x1 = _layer_norm(y, lg_ref[...], lb_ref[...])
        x1_ref[...] = x1.reshape(rows, dm)
        _to_row_tiles(u2_ref, (x1 * (1.0 + sc2_ref[...]) + sh2_ref[...]).reshape(rows, dm))

    @pl.when(pl.program_id(0) >= n_main)
    def _():
        x1_ref[...] = jnp.zeros_like(x1_ref)
        u2_ref[...] = jnp.zeros_like(u2_ref)


def _outproj(attn, conv, x, ada3, w_o_bf, norm_attn_g, ln1_g, ln1_b, *, nb, tm, alpha, n_total, row0, prev=None):
    bsz, seq, dm = x.shape
    rows = nb * tm
    assert row0 % rows == 0
    st = seq // tm
    blk0 = row0 // rows
    n_main = (bsz // nb) * st
    aliased = prev is not None
    n_steps = n_main if aliased else pl.cdiv(n_total, rows)

    def bs_of(i):
        j = jnp.minimum(i, n_main - 1)
        return j // st, j % st

    tile = lambda width: pl.BlockSpec((nb, tm, width), lambda i: bs_of(i) + (0,))
    ada = lambda chunk: pl.BlockSpec((nb, 1, dm), lambda i: (bs_of(i)[0], 0, chunk))
    const = lambda shape: pl.BlockSpec(shape, lambda i: (0,) * len(shape))
    out = pl.BlockSpec((rows, dm), lambda i: (blk0 + i, 0))
    body = functools.partial(_outproj_body, nb=nb, tm=tm, alpha=alpha, aliased=aliased, n_main=n_main)
    in_specs = [tile(D_ATTN), tile(D_CONV), tile(dm), ada(_GATE1), ada(_SHIFT2), ada(_SCALE2),
                const((D_ATTN, dm)), const((D_CONV, dm)), const((1, D_ATTN)), const((1, dm)), const((1, dm))]
    args = [attn, conv, x, ada3, ada3, ada3, w_o_bf[:D_ATTN], w_o_bf[D_ATTN:],
            norm_attn_g.reshape(1, D_ATTN), ln1_g.reshape(1, dm), ln1_b.reshape(1, dm)]
    kwargs = {}
    if aliased:
        in_specs = [pl.BlockSpec(memory_space=pl.ANY)] * 2 + in_specs
        args = list(prev) + args
        kwargs["input_output_aliases"] = {0: 0, 1: 1}
    return pl.pallas_call(
        body,
        grid=(n_steps,),
        in_specs=in_specs,
        out_specs=[out, pl.BlockSpec((rows * V7X_SUBLANES, V7X_LANES), lambda i: (blk0 + i, 0))],
        out_shape=[jax.ShapeDtypeStruct((n_total, dm), F32),
                   jax.ShapeDtypeStruct((n_total * V7X_SUBLANES, V7X_LANES), F32)],
        compiler_params=_cparams(("parallel",), 48),
        name="outproj_sample" if aliased else "outproj_prompt",
        **kwargs,
    )(*args)


def _router_body(u_ref, w_ref, b_ref, ri_ref, rr_ref, rg_ref, cnt_ref, run_ref, *, tm):
    i = pl.program_id(0)

    @pl.when(i == 0)
    def _():
        run_ref[...] = jnp.zeros_like(run_ref)

    logits = jnp.dot(_from_row_tiles(u_ref, tm), w_ref[...], preferred_element_type=F32,
                     precision=lax.Precision.HIGHEST) + b_ref[...]
    lane_i = lax.broadcasted_iota(I32, (tm, V7X_LANES), 1)
    lane = lane_i.astype(F32)
    vals = logits
    tops, idxs = [], []
    for _ in range(TOP_K):
        mk = jnp.max(vals, axis=-1, keepdims=True)
        ik = jnp.min(jnp.where(vals == mk, lane, float(V7X_LANES)), axis=-1, keepdims=True)
        tops.append(mk)
        idxs.append(ik)
        vals = jnp.where(lane == ik, -jnp.inf, vals)
    es = [jnp.exp(t - tops[0]) for t in tops]
    den = es[0] + es[1] + es[2] + es[3]
    sel = jnp.zeros((tm, V7X_LANES), F32)
    for ik in idxs:
        sel = sel + jnp.where(lane == ik, 1.0, 0.0)
    ri = lax.broadcasted_iota(I32, (tm, tm), 0)
    ci = lax.broadcasted_iota(I32, (tm, tm), 1)
    lower = jnp.where(ci < ri, 1.0, 0.0).astype(BF16)
    before = jnp.dot(lower, sel.astype(BF16), preferred_element_type=F32) + run_ref[...]
    out_i = jnp.full((tm, V7X_LANES), -1, I32)
    out_r = jnp.zeros((tm, V7X_LANES), I32)
    out_g = jnp.zeros((tm, V7X_LANES), F32)
    for kk in range(TOP_K):
        rank = jnp.sum(jnp.where(lane == idxs[kk], before, 0.0), axis=-1, keepdims=True)
        out_i = jnp.where(lane_i == kk, idxs[kk].astype(I32), out_i)
        out_r = jnp.where(lane_i == kk, rank.astype(I32), out_r)
        out_g = jnp.where(lane_i == kk, es[kk] / den, out_g)
    ri_ref[...] = out_i
    rr_ref[...] = out_r
    rg_ref[...] = out_g
    run_ref[...] = run_ref[...] + jnp.sum(sel, axis=0, keepdims=True)
    cnt_ref[...] = run_ref[...]


def _router(u2t, router_w, router_b, *, tm):
    n, dm = u2t.shape[0] // V7X_SUBLANES, router_w.shape[0]
    w_pad = jnp.pad(router_w, ((0, 0), (0, V7X_LANES - N_EXPERTS)))
    b_pad = jnp.pad(router_b, (0, V7X_LANES - N_EXPERTS), constant_values=NEG_INF).reshape(1, V7X_LANES)
    tok = pl.BlockSpec((tm, V7X_LANES), lambda i: (i, 0))
    return pl.pallas_call(
        functools.partial(_router_body, tm=tm),
        grid=(n // tm,),
        in_specs=[pl.BlockSpec((tm * V7X_SUBLANES, V7X_LANES), lambda i: (i, 0)),
                  pl.BlockSpec((dm, V7X_LANES), lambda i: (0, 0)),
                  pl.BlockSpec((1, V7X_LANES), lambda i: (0, 0))],
        out_specs=[tok, tok, tok, pl.BlockSpec((1, V7X_LANES), lambda i: (0, 0))],
        out_shape=[jax.ShapeDtypeStruct((n, V7X_LANES), I32), jax.ShapeDtypeStruct((n, V7X_LANES), I32),
                   jax.ShapeDtypeStruct((n, V7X_LANES), F32), jax.ShapeDtypeStruct((1, V7X_LANES), F32)],
        scratch_shapes=[pltpu.VMEM((1, V7X_LANES), F32)],
        compiler_params=_cparams(("arbitrary",), 32),
        name="router",
    )(u2t, w_pad, b_pad)


def _dest_body(ps_ref, ri_ref, rr_ref, d_ref):
    idx = ri_ref[...]
    acc = rr_ref[...]
    for e in range(N_EXPERTS):
        acc = acc + jnp.where(idx == e, ps_ref[e], 0)
    d_ref[...] = acc


def _dest_rows(route_i, route_r, pad_starts, *, tm):
    n = route_i.shape[0]
    tok = pl.BlockSpec((tm, V7X_LANES), lambda i: (i, 0))
    return pl.pallas_call(
        _dest_body,
        grid=(n // tm,),
        in_specs=[pl.BlockSpec(memory_space=pltpu.SMEM), tok, tok],
        out_specs=tok,
        out_shape=jax.ShapeDtypeStruct((n, V7X_LANES), I32),
        compiler_params=_cparams(("parallel",), 32),
        name="dest_rows",
    )(pad_starts, route_i, route_r)


ISSUE_TOKENS = 2


def _row_tile(ref, row):
    return ref.at[pl.ds(pl.multiple_of(row * V7X_SUBLANES, V7X_SUBLANES), V7X_SUBLANES), :]


def _issue_rows(dest_ref, n_tok, start_copy):
    def trip(it, carry):
        base = it * ISSUE_TOKENS
        rows = [dest_ref[0, 0, (base + u) * TOP_K + kk] for u in range(ISSUE_TOKENS) for kk in range(TOP_K)]
        for u in range(ISSUE_TOKENS):
            for kk in range(TOP_K):
                start_copy(base + u, kk, rows[u * TOP_K + kk])
        return carry

    lax.fori_loop(0, n_tok // ISSUE_TOKENS, trip, 0)


def _dispatch_body(dest_ref, u_hbm, xs_in, xs_out, sems, *, tm):
    del xs_in
    i = pl.program_id(0)
    slot = i % 2

    def start_copy(r, kk, row):
        pltpu.make_async_copy(_row_tile(u_hbm, i * tm + r), _row_tile(xs_out, row), sems.at[slot]).start()

    _issue_rows(dest_ref, tm, start_copy)

    def wait_step(s):
        span = xs_out.at[pl.ds(0, tm * TOP_K * V7X_SUBLANES), :]
        pltpu.make_async_copy(span, span, sems.at[s]).wait()

    @pl.when(i > 0)
    def _():
        wait_step(1 - slot)

    @pl.when(i == pl.num_programs(0) - 1)
    def _():
        wait_step(slot)


def _dispatch(u2t, dest_sm, n_rows, *, tm):
    n = u2t.shape[0] // V7X_SUBLANES
    return pl.pallas_call(
        functools.partial(_dispatch_body, tm=tm),
        grid=(n // tm,),
        in_specs=[pl.BlockSpec((1, 1, tm * TOP_K), lambda i: (i, 0, 0), memory_space=pltpu.SMEM),
                  pl.BlockSpec(memory_space=pl.ANY),
                  pl.BlockSpec(memory_space=pl.ANY)],
        out_specs=pl.BlockSpec(memory_space=pl.ANY),
        scratch_shapes=[pltpu.SemaphoreType.DMA((2,))],
        out_shape=jax.ShapeDtypeStruct((n_rows * V7X_SUBLANES, V7X_LANES), F32),
        input_output_aliases={2: 0},
        compiler_params=_cparams(("arbitrary",), 32),
        name="dispatch",
    )(dest_sm, u2t, jnp.zeros((n_rows * V7X_SUBLANES, V7X_LANES), F32))


def _gmm_body(be_ref, bv_ref, xs_ref, wu_ref, bu_ref, wd_ref, bd_ref, ys_ref, wu_bf, wd_bf, *, bm):
    j = pl.program_id(0)
    e = be_ref[j]
    prev = be_ref[jnp.maximum(j - 1, 0)]
    d_ff = wd_ref.shape[1]
    chunk = 64

    @pl.when((j == 0) | (e != prev))
    def _():
        def cast(c, carry):
            sl = pl.ds(pl.multiple_of(c * chunk, chunk), chunk)
            wu_bf[sl, :] = wu_ref[0, sl, :].astype(BF16)
            wd_bf[sl, :] = wd_ref[0, sl, :].astype(BF16)
            return carry

        lax.fori_loop(0, wu_ref.shape[1] // chunk, cast, 0)

    @pl.when(bv_ref[j] != 0)
    def _():
        x = _from_row_tiles(xs_ref, bm).astype(BF16)
        glu = jnp.dot(x, wu_bf[:, :d_ff], preferred_element_type=F32) + bu_ref[0, :, :d_ff]
        lin = jnp.dot(x, wu_bf[:, d_ff:], preferred_element_type=F32) + bu_ref[0, :, d_ff:]
        glu = jnp.minimum(glu, SWIGLU_LIMIT)
        lin = jnp.clip(lin, -SWIGLU_LIMIT, SWIGLU_LIMIT)
        act = glu * (1.0 / (1.0 + jnp.exp(-SWIGLU_ALPHA * glu))) * (lin + 1.0)
        _to_row_tiles(ys_ref, jnp.dot(act.astype(BF16), wd_bf[...], preferred_element_type=F32) + bd_ref[0])

    @pl.when(bv_ref[j] == 0)
    def _():
        ys_ref[...] = jnp.zeros_like(ys_ref)


def _gmm(xs, blk_e, blk_valid, w_up, b_up, w_down, b_down, *, bm):
    n_rows = xs.shape[0] // V7X_SUBLANES
    n_e, dm, d_up = w_up.shape
    d_ff = w_down.shape[1]
    assert d_ff == dm
    row_tiles = pl.BlockSpec((bm * V7X_SUBLANES, V7X_LANES), lambda j, be, bv: (j, 0))
    grid_spec = pltpu.PrefetchScalarGridSpec(
        num_scalar_prefetch=2,
        grid=(n_rows // bm,),
        in_specs=[row_tiles,
                  pl.BlockSpec((1, dm, d_up), lambda j, be, bv: (be[j], 0, 0)),
                  pl.BlockSpec((1, 1, d_up), lambda j, be, bv: (be[j], 0, 0)),
                  pl.BlockSpec((1, d_ff, dm), lambda j, be, bv: (be[j], 0, 0)),
                  pl.BlockSpec((1, 1, dm), lambda j, be, bv: (be[j], 0, 0))],
        out_specs=row_tiles,
        scratch_shapes=[pltpu.VMEM((dm, d_up), BF16), pltpu.VMEM((d_ff, dm), BF16)],
    )
    return pl.pallas_call(
        functools.partial(_gmm_body, bm=bm),
        grid_spec=grid_spec,
        out_shape=jax.ShapeDtypeStruct(xs.shape, F32),
        compiler_params=_cparams(("arbitrary",), 56),
        name="expert_mlp",
    )(blk_e, blk_valid, xs, w_up, b_up.reshape(n_e, 1, d_up), w_down, b_down.reshape(n_e, 1, dm))


def _combine_body(rt_ref, rtn_ref, ys_hbm, rg_ref, x1_ref, g2_ref, lg_ref, lb_ref, y_ref, rows_ref, sems,
                  *, nb, tm, alpha, n_steps):
    n_tok = nb * tm
    i = pl.program_id(0)
    slot = i % 2

    def issue(dest_ref, s):
        def start_copy(r, kk, row):
            pltpu.make_async_copy(_row_tile(ys_hbm, row), _row_tile(rows_ref.at[s, kk], r), sems.at[s]).start()

        _issue_rows(dest_ref, n_tok, start_copy)

    @pl.when(i == 0)
    def _():
        issue(rt_ref, 0)

    @pl.when(i + 1 < n_steps)
    def _():
        issue(rtn_ref, 1 - slot)

    for kk in range(TOP_K):
        pltpu.make_async_copy(ys_hbm.at[pl.ds(0, n_tok * V7X_SUBLANES), :], rows_ref.at[slot, kk],
                              sems.at[slot]).wait()

    gates = rg_ref[...]
    ffn = gates[:, 0:1] * _from_row_tiles(rows_ref, n_tok, (slot, 0))
    for kk in range(1, TOP_K):
        ffn = ffn + gates[:, kk:kk + 1] * _from_row_tiles(rows_ref, n_tok, (slot, kk))
    dm = ffn.shape[-1]
    y = alpha * x1_ref[...].reshape(nb, tm, dm) + (1.0 + g2_ref[...]) * ffn.reshape(nb, tm, dm)
    y_ref[...] = _layer_norm(y, lg_ref[...], lb_ref[...])


def _combine(ys, dest_sm, route_g, x1_all, ada3, ln2_g, ln2_b, *, bsz, seq, nb, tm, alpha, row0):
    dm = x1_all.shape[1]
    n_tok = nb * tm
    assert row0 % n_tok == 0 and dest_sm.shape[2] == n_tok * TOP_K and n_tok % ISSUE_TOKENS == 0
    blk0 = row0 // n_tok
    st = seq // tm
    n_steps = (bsz // nb) * st
    dest = lambda off: pl.BlockSpec((1, 1, n_tok * TOP_K),
                                    lambda i: (blk0 + jnp.minimum(i + off, n_steps - 1), 0, 0),
                                    memory_space=pltpu.SMEM)
    return pl.pallas_call(
        functools.partial(_combine_body, nb=nb, tm=tm, alpha=alpha, n_steps=n_steps),
        grid=(n_steps,),
        in_specs=[dest(0), dest(1),
                  pl.BlockSpec(memory_space=pl.ANY),
                  pl.BlockSpec((n_tok, V7X_LANES), lambda i: (blk0 + i, 0)),
                  pl.BlockSpec((n_tok, dm), lambda i: (blk0 + i, 0)),
                  pl.BlockSpec((nb, 1, dm), lambda i: (i // st, 0, _GATE2)),
                  pl.BlockSpec((1, dm), lambda i: (0, 0)),
                  pl.BlockSpec((1, dm), lambda i: (0, 0))],
        out_specs=pl.BlockSpec((nb, tm, dm), lambda i: (i // st, i % st, 0)),
        scratch_shapes=[pltpu.VMEM((2, TOP_K, n_tok * V7X_SUBLANES, V7X_LANES), F32),
                        pltpu.SemaphoreType.DMA((2,))],
        out_shape=jax.ShapeDtypeStruct((bsz, seq, dm), F32),
        compiler_params=_cparams(("arbitrary",), 48),
        name="combine_prompt" if row0 == 0 else "combine_sample",
    )(dest_sm, dest_sm, ys, route_g, x1_all, ada3, ln2_g.reshape(1, dm), ln2_b.reshape(1, dm))


def _layer(xp, xs, cache_k, cache_v, conv_past, cp, cs, w_ada, b_ada, w_in, conv_w, norm_attn_g, norm_conv_g,
           w_o, ln1_g, ln1_b, bias_tabs, router_w, router_b, w_up, b_up, w_down, b_down, ln2_g, ln2_b, alpha):
    bp, sp, dm = xp.shape
    bs, ts, _ = xs.shape
    n_p, n_s = bp * sp, bs * ts
    n_tok = n_p + n_s
    assert n_p % TOK_TILE == 0 and n_s == TOK_TILE and sp % ROW_TILE == 0
    bias_p, bias_c, bias_n = bias_tabs

    ada = _ada(jnp.concatenate([cp, cs], axis=0), w_ada, b_ada)
    ada_p = ada[:bp].reshape(bp, 1, -1)
    ada_s = ada[bp:].reshape(bs, 1, -1)
    w_in_bf = w_in.astype(BF16)
    w_o_bf = w_o.astype(BF16)

    qp, kp, vp, convp, cstp, kpt, vpt = _inproj(xp, ada_p, w_in_bf, conv_w, norm_conv_g, None, nb=1, tm=ROW_TILE)
    attn_p = _attn_prompt(qp, kp, vp, bias_p)
    x1_all, u2_all = _outproj(attn_p, convp, xp, ada_p, w_o_bf, norm_attn_g, ln1_g, ln1_b,
                              nb=1, tm=ROW_TILE, alpha=alpha, n_total=n_tok, row0=0)
    qs, ks, vs, convs, csts = _inproj(xs, ada_s, w_in_bf, conv_w, norm_conv_g, conv_past, nb=bs, tm=ts)
    heads = lambda a: jnp.transpose(a.reshape(bs, ts, N_HEADS, HEAD_DIM), (0, 2, 1, 3))
    n_past = cache_k.shape[1]
    feat_major = lambda c: jnp.transpose(c.reshape(bs, n_past, D_ATTN), (0, 2, 1))
    attn_s = _attn_sample(heads(qs), heads(ks), heads(vs), feat_major(cache_k), feat_major(cache_v), bias_c, bias_n)
    attn_s = jnp.transpose(attn_s, (0, 2, 1, 3)).reshape(bs, ts, D_ATTN)
    x1_all, u2_all = _outproj(attn_s, convs, xs, ada_s, w_o_bf, norm_attn_g, ln1_g, ln1_b,
                              nb=bs, tm=ts, alpha=alpha, n_total=n_tok, row0=n_p, prev=(x1_all, u2_all))

    route_i, route_r, route_g, counts = _router(u2_all, router_w, router_b, tm=TOK_TILE)
    counts = counts[0, :N_EXPERTS].astype(I32)
    padded = ((counts + EXPERT_BLK - 1) // EXPERT_BLK) * EXPERT_BLK
    pad_ends = jnp.cumsum(padded)
    pad_starts = (pad_ends - padded).astype(I32)
    n_blocks = n_tok * TOP_K // EXPERT_BLK + N_EXPERTS
    n_rows = n_blocks * EXPERT_BLK
    blk_row = jnp.arange(n_blocks, dtype=I32) * EXPERT_BLK
    blk_valid = (blk_row < pad_ends[-1]).astype(I32)
    blk_e = jnp.sum((blk_row[:, None] >= pad_ends[None, :]).astype(I32), axis=1)
    last_e = jnp.sum((pad_ends[-1] - 1 >= pad_ends).astype(I32))
    blk_e = jnp.where(blk_valid != 0, blk_e, last_e).astype(I32)
    dest = _dest_rows(route_i, route_r, pad_starts, tm=TOK_TILE)
    dest_sm = dest[:, :TOP_K].reshape(n_tok // TOK_TILE, 1, TOK_TILE * TOP_K)

    x_sorted = _dispatch(u2_all, dest_sm, n_rows, tm=TOK_TILE)
    y_sorted = _gmm(x_sorted, blk_e, blk_valid, w_up, b_up, w_down, b_down, bm=EXPERT_BLK)
    yp = _combine(y_sorted, dest_sm, route_g, x1_all, ada_p, ln2_g, ln2_b,
                  bsz=bp, seq=sp, nb=1, tm=TOK_TILE, alpha=alpha, row0=0)
    ys_out = _combine(y_sorted, dest_sm, route_g, x1_all, ada_s, ln2_g, ln2_b,
                      bsz=bs, seq=ts, nb=bs, tm=ts, alpha=alpha, row0=n_p)
    return yp, ys_out, kpt, vpt, cstp, ks, vs, csts


def kernel(x_prompt, x_sample, cache_k, cache_v, state_conv, c_prompt, c_sample, w_ada, b_ada, w_in, conv_w,
           norm_attn_g, norm_conv_g, w_o, ln1_g, ln1_b, rel_bias, router_w, router_b, w_up, b_up, w_down, b_down,
           ln2_g, ln2_b):
    depth = w_ada.shape[0]
    alpha = (2 * depth) ** 0.25
    xp, xs = x_prompt, x_sample
    bp, sp, _ = xp.shape
    bs, ts, _ = xs.shape
    n_keep = min(BRANCHES[-1][0], sp)
    bias_tabs = _bias_tables(rel_bias, ts, cache_k.shape[2])
    outs = [[] for _ in range(6)]
    for l in range(depth):
        xp, xs, kp, vp, cstp, ks, vs, csts = _layer(
            xp, xs, cache_k[l], cache_v[l], state_conv[l], c_prompt, c_sample, w_ada[l], b_ada[l], w_in[l],
            conv_w[l], norm_attn_g[l], norm_conv_g[l], w_o[l], ln1_g[l], ln1_b[l], bias_tabs, router_w[l],
            router_b[l], w_up[l], b_up[l], w_down[l], b_down[l], ln2_g[l], ln2_b[l], alpha)
        kp = jnp.transpose(kp, (0, 2, 1)).reshape(bp, sp, N_HEADS, HEAD_DIM)[:, -n_keep:]
        vp = jnp.transpose(vp, (0, 2, 1)).reshape(bp, sp, N_HEADS, HEAD_DIM)[:, -n_keep:]
        for lst, val in zip(outs, (kp, vp, cstp, ks.reshape(bs, ts, N_HEADS, HEAD_DIM),
                                   vs.reshape(bs, ts, N_HEADS, HEAD_DIM), csts)):
            lst.append(val)
    return (xp, xs) + tuple(jnp.stack(o) for o in outs)
```

```python
import functools
import math

import numpy as np
import jax
import jax.numpy as jnp
from jax import lax
from jax.experimental import pallas as pl
from jax.experimental.pallas import tpu as pltpu

F32 = jnp.float32
BF16 = jnp.bfloat16
I32 = jnp.int32

HEAD_DIM = 64
N_HEADS = 12
D_ATTN = N_HEADS * HEAD_DIM
D_CONV = 256
CONV_WIDTH = 3
BRANCHES = ((128, 1), (512, 4), (2048, 16))
NUM_BUCKETS = 32
MAX_DISTANCE = 2048
N_EXPERTS = 32
TOP_K = 4
SWIGLU_LIMIT = 7.0
SWIGLU_ALPHA = 1.702
LN_EPS = 1e-5
RMS_EPS = 1e-6
NEG_INF = -1e30

V7X_LANES = 128
V7X_SUBLANES = 8
V7X_VMEM_BYTES = 64 * 1024 * 1024

Q_BLK = 128
PAIR = 2 * HEAD_DIM
N_PAIRS = N_HEADS // 2
TOK_TILE = 256
ROW_TILE = 512
EXPERT_BLK = 256


def _cparams(sem, vmem_mb):
    return pltpu.CompilerParams(dimension_semantics=sem, vmem_limit_bytes=vmem_mb * 1024 * 1024)


def _t5_bucket_np(dist):
    dist = np.asarray(dist, np.int64)
    max_exact = NUM_BUCKETS // 2
    df = np.maximum(dist, max_exact).astype(np.float32)
    large = max_exact + (np.log(df / np.float32(max_exact)) / np.float32(math.log(MAX_DISTANCE / max_exact))
                         * np.float32(NUM_BUCKETS - max_exact)).astype(np.int32)
    return np.where(dist < max_exact, dist, np.minimum(large, NUM_BUCKETS - 1)).astype(np.int32)


def _prompt_bucket_index():
    a = np.arange(Q_BLK)[:, None]
    c = np.arange(2 * Q_BLK)[None, :]
    step = Q_BLK + a - c
    valid = (step >= 0) & (step <= Q_BLK)
    out = []
    for _, d in BRANCHES:
        out.append(np.where(valid, _t5_bucket_np(np.clip(step, 0, Q_BLK) * d), -1))
    return np.stack(out).astype(np.int32)


def _sample_bucket_index(n_new, n_past):
    t = np.arange(n_new)[:, None]
    out_c, out_n = [], []
    for w, d in BRANCHES:
        dist_c = n_past + t - np.arange(n_past)[None, :]
        dist_n = t - np.arange(n_new)[None, :]
        for dist, out in ((dist_c, out_c), (dist_n, out_n)):
            valid = (dist >= 0) & (dist <= w) & (dist % d == 0)
            out.append(np.where(valid, _t5_bucket_np(np.clip(dist, 0, w)), -1))
    return np.stack(out_c).astype(np.int32), np.stack(out_n).astype(np.int32)


def _bias_body(rb_ref, idx_ref, out_ref):
    idx = idx_ref[0]
    for h in range(N_HEADS):
        acc = jnp.where(idx < 0, NEG_INF, 0.0).astype(F32)
        for b in range(NUM_BUCKETS):
            acc = acc + jnp.where(idx == b, rb_ref[b * N_HEADS + h], 0.0)
        out_ref[0, h] = acc


def _bias_expand(rel_bias, idx_np, name):
    nbr, r, c = idx_np.shape
    return pl.pallas_call(
        _bias_body,
        grid=(nbr,),
        in_specs=[pl.BlockSpec(memory_space=pltpu.SMEM),
                  pl.BlockSpec((1, r, c), lambda i: (i, 0, 0))],
        out_specs=pl.BlockSpec((1, N_HEADS, r, c), lambda i: (i, 0, 0, 0)),
        out_shape=jax.ShapeDtypeStruct((nbr, N_HEADS, r, c), F32),
        name=name,
    )(rel_bias.reshape(-1), jnp.asarray(idx_np))


def _bias_tables(rel_bias, n_new, n_past):
    bias_p = _bias_expand(rel_bias, _prompt_bucket_index(), "bias_prompt")
    bias_p = bias_p.reshape(len(BRANCHES), N_PAIRS, 2 * Q_BLK, 2 * Q_BLK)
    ic, inw = _sample_bucket_index(n_new, n_past)
    return bias_p, _bias_expand(rel_bias, ic, "bias_cache"), _bias_expand(rel_bias, inw, "bias_new")


def _ada_body(c_ref, w_ref, b_ref, o_ref):
    c = c_ref[...]
    s = c * (1.0 / (1.0 + jnp.exp(-c)))
    o_ref[...] = jnp.dot(s.astype(BF16), w_ref[...].astype(BF16), preferred_element_type=F32) + b_ref[...]


def _ada(c_all, w_ada, b_ada):
    n, dm = c_all.shape
    n_out = w_ada.shape[1]
    tn = dm
    return pl.pallas_call(
        _ada_body,
        grid=(n_out // tn,),
        in_specs=[pl.BlockSpec((n, dm), lambda j: (0, 0)),
                  pl.BlockSpec((dm, tn), lambda j: (0, j)),
                  pl.BlockSpec((1, tn), lambda j: (0, j))],
        out_specs=pl.BlockSpec((n, tn), lambda j: (0, j)),
        out_shape=jax.ShapeDtypeStruct((n, n_out), F32),
        compiler_params=_cparams(("parallel",), 32),
        name="ada",
    )(c_all, w_ada, b_ada.reshape(1, n_out))


_SHIFT1, _SCALE1, _GATE1, _SHIFT2, _SCALE2, _GATE2 = range(6)


def _inproj_body(*refs, nb, tm, carry):
    if carry:
        (x_ref, sh_ref, sc_ref, w_ref, cw_ref, ng_ref, past_ref, wt_ref,
         q_ref, k_ref, v_ref, conv_ref, cst_ref, kt_ref, vt_ref, zz_ref) = refs
    else:
        (x_ref, sh_ref, sc_ref, w_ref, cw_ref, ng_ref, past_ref,
         q_ref, k_ref, v_ref, conv_ref, cst_ref, zz_ref) = refs
    dm = x_ref.shape[-1]
    rows = nb * tm
    u = x_ref[...] * (1.0 + sc_ref[...]) + sh_ref[...]
    u = u.reshape(rows, dm).astype(BF16)

    def proj(lo, width):
        return jnp.dot(u, w_ref[:, lo:lo + width], preferred_element_type=F32)

    q_ref[...] = proj(0, D_ATTN).reshape(nb, tm, D_ATTN)
    k_ref[...] = proj(D_ATTN, D_ATTN).reshape(nb, tm, D_ATTN)
    v_ref[...] = proj(2 * D_ATTN, D_ATTN).reshape(nb, tm, D_ATTN)
    if carry:
        nt = (((1,), (1,)), ((), ()))
        kt_ref[0] = lax.dot_general(wt_ref[D_ATTN:2 * D_ATTN, :], u, nt, preferred_element_type=F32)
        vt_ref[0] = lax.dot_general(wt_ref[2 * D_ATTN:3 * D_ATTN, :], u, nt, preferred_element_type=F32)
    gb = proj(3 * D_ATTN, D_CONV)
    gc = proj(3 * D_ATTN + D_CONV, D_CONV)
    hh = proj(3 * D_ATTN + 2 * D_CONV, D_CONV)
    z = (gc * hh).reshape(nb, tm, D_CONV)

    if carry:
        s = pl.program_id(1)

        @pl.when(s == 0)
        def _():
            zz_ref[:, 0:V7X_SUBLANES, :] = jnp.zeros((nb, V7X_SUBLANES, D_CONV), F32)

        @pl.when(s > 0)
        def _():
            zz_ref[:, 0:V7X_SUBLANES, :] = zz_ref[:, tm:tm + V7X_SUBLANES, :]
    else:
        zz_ref[:, V7X_SUBLANES - 2:V7X_SUBLANES, :] = past_ref[...]
    zz_ref[:, V7X_SUBLANES:, :] = z

    cw = cw_ref[...]
    yc = (cw[0:1, :] * zz_ref[:, V7X_SUBLANES - 2:V7X_SUBLANES - 2 + tm, :]
          + cw[1:2, :] * zz_ref[:, V7X_SUBLANES - 1:V7X_SUBLANES - 1 + tm, :]
          + cw[2:3, :] * z)
    g = gb.reshape(nb, tm, D_CONV) * yc
    ms = jnp.mean(g * g, axis=-1, keepdims=True)
    conv_ref[...] = g * lax.rsqrt(ms + RMS_EPS) * ng_ref[...]
    cst_ref[...] = zz_ref[:, tm + V7X_SUBLANES - 2:tm + V7X_SUBLANES, :]


def _inproj(x, ada3, w_in_bf, conv_w, norm_conv_g, past, *, nb, tm):
    bsz, seq, dm = x.shape
    carry = past is None
    if carry:
        past = jnp.zeros((bsz, CONV_WIDTH - 1, D_CONV), F32)
    grid = (bsz // nb, seq // tm)
    d_in = w_in_bf.shape[1]
    body = functools.partial(_inproj_body, nb=nb, tm=tm, carry=carry)
    tile = lambda width: pl.BlockSpec((nb, tm, width), lambda b, s: (b, s, 0))
    in_specs = [tile(dm),
                pl.BlockSpec((nb, 1, dm), lambda b, s: (b, 0, _SHIFT1)),
                pl.BlockSpec((nb, 1, dm), lambda b, s: (b, 0, _SCALE1)),
                pl.BlockSpec((dm, d_in), lambda b, s: (0, 0)),
                pl.BlockSpec((CONV_WIDTH, D_CONV), lambda b, s: (0, 0)),
                pl.BlockSpec((1, D_CONV), lambda b, s: (0, 0)),
                pl.BlockSpec((nb, CONV_WIDTH - 1, D_CONV), lambda b, s: (b, 0, 0))]
    args = [x, ada3, ada3, w_in_bf, conv_w, norm_conv_g.reshape(1, D_CONV), past]
    out_specs = [tile(D_ATTN), tile(D_ATTN), tile(D_ATTN), tile(D_CONV),
                 pl.BlockSpec((nb, CONV_WIDTH - 1, D_CONV), lambda b, s: (b, 0, 0))]
    out_shape = ([jax.ShapeDtypeStruct((bsz, seq, D_ATTN), F32)] * 3
                 + [jax.ShapeDtypeStruct((bsz, seq, D_CONV), F32),
                    jax.ShapeDtypeStruct((bsz, CONV_WIDTH - 1, D_CONV), F32)])
    if carry:
        assert nb == 1
        in_specs.append(pl.BlockSpec((d_in, dm), lambda b, s: (0, 0)))
        args.append(w_in_bf.T)
        out_specs += [pl.BlockSpec((1, D_ATTN, tm), lambda b, s: (b, 0, s))] * 2
        out_shape += [jax.ShapeDtypeStruct((bsz, D_ATTN, seq), F32)] * 2
    return pl.pallas_call(
        body,
        grid=grid,
        in_specs=in_specs,
        out_specs=out_specs,
        out_shape=out_shape,
        scratch_shapes=[pltpu.VMEM((nb, tm + V7X_SUBLANES, D_CONV), F32)],
        compiler_params=_cparams(("parallel", "arbitrary"), 56),
        name="inproj_prompt" if carry else "inproj_sample",
    )(*args)


def _attn_prompt_body(q_ref, k_ref, v_ref, bias_ref, o_ref, ob_ref, lb_ref, *, seq):
    scale = HEAD_DIM ** -0.5
    lane = lax.broadcasted_iota(I32, (Q_BLK, PAIR), 1)
    even = lane < HEAD_DIM
    ones = jnp.ones((Q_BLK, PAIR), BF16)

    def rows(ref, start, d):
        if d == 1:
            return ref[pl.ds(start, Q_BLK), :]
        return ref[pl.ds(start, Q_BLK, stride=d), :]

    def store(ref, br, start, d, val):
        if d == 1:
            ref[br, pl.ds(start, Q_BLK), :] = val
        else:
            ref[br, pl.ds(start, Q_BLK, stride=d), :] = val

    for br, (_, d) in enumerate(BRANCHES):
        nblk = seq // (d * Q_BLK)

        def block(it, carry, br=br, d=d, nblk=nblk):
            bias = bias_ref[br, 0]
            r = it // nblk
            i = it - r * nblk
            cur0 = r + d * Q_BLK * i
            qf = rows(q_ref, cur0, d) * scale
            qs = jnp.concatenate([jnp.where(even, qf, 0.0), jnp.where(even, 0.0, qf)], axis=0).astype(BF16)
            kc = rows(k_ref, cur0, d).astype(BF16)
            vc = rows(v_ref, cur0, d).astype(BF16)
            if nblk == 1:
                s = lax.dot_general(qs, kc, (((1,), (1,)), ((), ())), preferred_element_type=F32)
                s = s + bias[:, Q_BLK:]
                vaug = jnp.concatenate([vc, ones], axis=1)
            else:
                prev0 = r + d * Q_BLK * jnp.maximum(i - 1, 0)
                kp = rows(k_ref, prev0, d).astype(BF16)
                vp = rows(v_ref, prev0, d).astype(BF16)
                kk = jnp.concatenate([kp, kc], axis=0)
                s = lax.dot_general(qs, kk, (((1,), (1,)), ((), ())), preferred_element_type=F32)
                col = lax.broadcasted_iota(I32, (1, 2 * Q_BLK), 1)
                first = jnp.where((col < Q_BLK) & (i == 0), NEG_INF, 0.0)
                s = s + bias + first
                vaug = jnp.concatenate([jnp.concatenate([vp, vc], axis=0),
                                        jnp.concatenate([ones, ones], axis=0)], axis=1)
            m = jnp.max(s, axis=-1, keepdims=True)
            p = jnp.exp(s - m).astype(BF16)
            out = jnp.dot(p, vaug, preferred_element_type=F32)
            num, den = out[:, :PAIR], out[:, PAIR:]
            o = num / den
            lse = m + jnp.log(den)
            store(ob_ref, br, cur0, d, jnp.where(even, o[:Q_BLK], o[Q_BLK:]))
            store(lb_ref, br, cur0, d, jnp.where(even, lse[:Q_BLK], lse[Q_BLK:]))
            return carry

        lax.fori_loop(0, d * nblk, block, 0)

    mt = 2 * Q_BLK

    def merge(i, carry):
        sl = pl.ds(pl.multiple_of(i * mt, mt), mt)
        l0, l1, l2 = lb_ref[0, sl, :], lb_ref[1, sl, :], lb_ref[2, sl, :]
        mx = jnp.maximum(jnp.maximum(l0, l1), l2)
        w0, w1, w2 = jnp.exp(l0 - mx), jnp.exp(l1 - mx), jnp.exp(l2 - mx)
        acc = w0 * ob_ref[0, sl, :] + w1 * ob_ref[1, sl, :] + w2 * ob_ref[2, sl, :]
        o_ref[sl, :] = acc / (w0 + w1 + w2)
        return carry

    lax.fori_loop(0, seq // mt, merge, 0)


def _attn_prompt(q, k, v, bias_p):
    bsz, seq, _ = q.shape
    assert seq % (BRANCHES[-1][1] * Q_BLK) == 0
    body = functools.partial(_attn_prompt_body, seq=seq)
    col = pl.BlockSpec((None, seq, PAIR), lambda b, j: (b, 0, j))
    return pl.pallas_call(
        body,
        grid=(bsz, N_PAIRS),
        in_specs=[col, col, col,
                  pl.BlockSpec((len(BRANCHES), 1, 2 * Q_BLK, 2 * Q_BLK), lambda b, j: (0, j, 0, 0))],
        out_specs=col,
        out_shape=jax.ShapeDtypeStruct((bsz, seq, D_ATTN), F32),
        scratch_shapes=[pltpu.VMEM((len(BRANCHES), seq, PAIR), F32),
                        pltpu.VMEM((len(BRANCHES), seq, PAIR), F32)],
        compiler_params=_cparams(("parallel", "parallel"), 48),
        name="attn_prompt",
    )(q, k, v, bias_p)


def _attn_sample_body(q_ref, kn_ref, vn_ref, ckt_ref, cvt_ref, bc_ref, bn_ref, o_ref):
    scale = HEAD_DIM ** -0.5
    nt = (((1,), (1,)), ((), ()))
    nbr = len(BRANCHES)
    for h in range(N_HEADS):
        rows = slice(h * HEAD_DIM, (h + 1) * HEAD_DIM)
        qh = (q_ref[h] * scale).astype(BF16)
        s_c = jnp.dot(qh, ckt_ref[rows, :].astype(BF16), preferred_element_type=F32)
        s_n = lax.dot_general(qh, kn_ref[h].astype(BF16), nt, preferred_element_type=F32)
        ps, lses, dens = [], [], []
        for br in range(nbr):
            lc = s_c + bc_ref[br, h]
            ln = s_n + bn_ref[br, h]
            m = jnp.maximum(jnp.max(lc, axis=-1, keepdims=True), jnp.max(ln, axis=-1, keepdims=True))
            pc = jnp.exp(lc - m)
            pn = jnp.exp(ln - m)
            den = jnp.sum(pc, axis=-1, keepdims=True) + jnp.sum(pn, axis=-1, keepdims=True)
            ps.append((pc, pn))
            dens.append(den)
            lses.append(m + jnp.log(den))
        mx = jnp.maximum(jnp.maximum(lses[0], lses[1]), lses[2])
        ws = [jnp.exp(l - mx) for l in lses]
        wsum = ws[0] + ws[1] + ws[2]
        coefs = [ws[br] / (wsum * dens[br]) for br in range(nbr)]
        p_c = coefs[0] * ps[0][0] + coefs[1] * ps[1][0] + coefs[2] * ps[2][0]
        p_n = coefs[0] * ps[0][1] + coefs[1] * ps[1][1] + coefs[2] * ps[2][1]
        o = lax.dot_general(p_c.astype(BF16), cvt_ref[rows, :].astype(BF16), nt, preferred_element_type=F32)
        o_ref[h] = o + jnp.dot(p_n.astype(BF16), vn_ref[h].astype(BF16), preferred_element_type=F32)


def _attn_sample(q, k_new, v_new, cache_kt, cache_vt, bias_c, bias_n):
    bsz, _, n_new, _ = q.shape
    n_past = cache_kt.shape[2]
    new = pl.BlockSpec((None, N_HEADS, n_new, HEAD_DIM), lambda b: (b, 0, 0, 0))
    cache = pl.BlockSpec((None, D_ATTN, n_past), lambda b: (b, 0, 0))
    return pl.pallas_call(
        _attn_sample_body,
        grid=(bsz,),
        in_specs=[new, new, new, cache, cache,
                  pl.BlockSpec(bias_c.shape, lambda b: (0, 0, 0, 0)), pl.BlockSpec(bias_n.shape, lambda b: (0, 0, 0, 0))],
        out_specs=new,
        out_shape=jax.ShapeDtypeStruct((bsz, N_HEADS, n_new, HEAD_DIM), F32),
        compiler_params=_cparams(("parallel",), 48),
        name="attn_sample",
    )(q, k_new, v_new, cache_kt, cache_vt, bias_c, bias_n)


def _layer_norm(y, g, b):
    mu = jnp.mean(y, axis=-1, keepdims=True)
    c = y - mu
    var = jnp.mean(c * c, axis=-1, keepdims=True)
    return c * lax.rsqrt(var + LN_EPS) * g + b


def _to_row_tiles(ref, val):
    rows, dm = val.shape
    assert dm == V7X_SUBLANES * V7X_LANES
    for c in range(V7X_SUBLANES):
        ref[pl.ds(c, rows, stride=V7X_SUBLANES), :] = val[:, c * V7X_LANES:(c + 1) * V7X_LANES]


def _from_row_tiles(ref, rows, lead=()):
    chunks = [ref[lead + (pl.ds(c, rows, stride=V7X_SUBLANES), slice(None))] for c in range(V7X_SUBLANES)]
    return jnp.concatenate(chunks, axis=1)


def _outproj_body(*refs, nb, tm, alpha, aliased, n_main):
    if aliased:
        refs = refs[2:]
    (a_ref, c_ref, x_ref, g1_ref, sh2_ref, sc2_ref, wa_ref, wc_ref, ng_ref, lg_ref, lb_ref, x1_ref, u2_ref) = refs
    rows = nb * tm

    @pl.when(pl.program_id(0) < n_main)
    def _():
        a = a_ref[...]
        ms = jnp.mean(a * a, axis=-1, keepdims=True)
        an = (a * lax.rsqrt(ms + RMS_EPS) * ng_ref[...]).reshape(rows, D_ATTN).astype(BF16)
        cn = c_ref[...].reshape(rows, D_CONV).astype(BF16)
        mix = (jnp.dot(an, wa_ref[...], preferred_element_type=F32)
               + jnp.dot(cn, wc_ref[...], preferred_element_type=F32))
        dm = mix.shape[-1]
        y = alpha * x_ref[...] + (1.0 + g1_ref[...]) * mix.reshape(nb, tm, dm)
        x1 = _layer_norm(y, lg_ref[...], lb_ref[...])
        x1_ref[...] = x1.reshape(rows, dm)
        _to_row_tiles(u2_ref, (x1 * (1.0 + sc2_ref[...]) + sh2_ref[...]).reshape(rows, dm))

    @pl.when(pl.program_id(0) >= n_main)
    def _():
        x1_ref[...] = jnp.zeros_like(x1_ref)
        u2_ref[...] = jnp.zeros_like(u2_ref)


def _outproj(attn, conv, x, ada3, w_o_bf, norm_attn_g, ln1_g, ln1_b, *, nb, tm, alpha, n_total, row0, prev=None):
    bsz, seq, dm = x.shape
    rows = nb * tm
    assert row0 % rows == 0
    st = seq // tm
    blk0 = row0 // rows
    n_main = (bsz // nb) * st
    aliased = prev is not None
    n_steps = n_main if aliased else pl.cdiv(n_total, rows)

    def bs_of(i):
        j = jnp.minimum(i, n_main - 1)
        return j // st, j % st

    tile = lambda width: pl.BlockSpec((nb, tm, width), lambda i: bs_of(i) + (0,))
    ada = lambda chunk: pl.BlockSpec((nb, 1, dm), lambda i: (bs_of(i)[0], 0, chunk))
    const = lambda shape: pl.BlockSpec(shape, lambda i: (0,) * len(shape))
    out = pl.BlockSpec((rows, dm), lambda i: (blk0 + i, 0))
    body = functools.partial(_outproj_body, nb=nb, tm=tm, alpha=alpha, aliased=aliased, n_main=n_main)
    in_specs = [tile(D_ATTN), tile(D_CONV), tile(dm), ada(_GATE1), ada(_SHIFT2), ada(_SCALE2),
                const((D_ATTN, dm)), const((D_CONV, dm)), const((1, D_ATTN)), const((1, dm)), const((1, dm))]
    args = [attn, conv, x, ada3, ada3, ada3, w_o_bf[:D_ATTN], w_o_bf[D_ATTN:],
            norm_attn_g.reshape(1, D_ATTN), ln1_g.reshape(1, dm), ln1_b.reshape(1, dm)]
    kwargs = {}
    if aliased:
        in_specs = [pl.BlockSpec(memory_space=pl.ANY)] * 2 + in_specs
        args = list(prev) + args
        kwargs["input_output_aliases"] = {0: 0, 1: 1}
    return pl.pallas_call(
        body,
        grid=(n_steps,),
        in_specs=in_specs,
        out_specs=[out, pl.BlockSpec((rows * V7X_SUBLANES, V7X_LANES), lambda i: (blk0 + i, 0))],
        out_shape=[jax.ShapeDtypeStruct((n_total, dm), F32),
                   jax.ShapeDtypeStruct((n_total * V7X_SUBLANES, V7X_LANES), F32)],
        compiler_params=_cparams(("parallel",), 48),
        name="outproj_sample" if aliased else "outproj_prompt",
        **kwargs,
    )(*args)


def _router_body(u_ref, w_ref, b_ref, ri_ref, rr_ref, rg_ref, cnt_ref, run_ref, *, tm):
    i = pl.program_id(0)

    @pl.when(i == 0)
    def _():
        run_ref[...] = jnp.zeros_like(run_ref)

    logits = jnp.dot(_from_row_tiles(u_ref, tm), w_ref[...], preferred_element_type=F32,
                     precision=lax.Precision.HIGHEST) + b_ref[...]
    lane_i = lax.broadcasted_iota(I32, (tm, V7X_LANES), 1)
    lane = lane_i.astype(F32)
    vals = logits
    tops, idxs = [], []
    for _ in range(TOP_K):
        mk = jnp.max(vals, axis=-1, keepdims=True)
        ik = jnp.min(jnp.where(vals == mk, lane, float(V7X_LANES)), axis=-1, keepdims=True)
        tops.append(mk)
        idxs.append(ik)
        vals = jnp.where(lane == ik, -jnp.inf, vals)
    es = [jnp.exp(t - tops[0]) for t in tops]
    den = es[0] + es[1] + es[2] + es[3]
    sel = jnp.zeros((tm, V7X_LANES), F32)
    for ik in idxs:
        sel = sel + jnp.where(lane == ik, 1.0, 0.0)
    ri = lax.broadcasted_iota(I32, (tm, tm), 0)
    ci = lax.broadcasted_iota(I32, (tm, tm), 1)
    lower = jnp.where(ci < ri, 1.0, 0.0).astype(BF16)
    before = jnp.dot(lower, sel.astype(BF16), preferred_element_type=F32) + run_ref[...]
    out_i = jnp.full((tm, V7X_LANES), -1, I32)
    out_r = jnp.zeros((tm, V7X_LANES), I32)
    out_g = jnp.zeros((tm, V7X_LANES), F32)
    for kk in range(TOP_K):
        rank = jnp.sum(jnp.where(lane == idxs[kk], before, 0.0), axis=-1, keepdims=True)
        out_i = jnp.where(lane_i == kk, idxs[kk].astype(I32), out_i)
        out_r = jnp.where(lane_i == kk, rank.astype(I32), out_r)
        out_g = jnp.where(lane_i == kk, es[kk] / den, out_g)
    ri_ref[...] = out_i
    rr_ref[...] = out_r
    rg_ref[...] = out_g
    run_ref[...] = run_ref[...] + jnp.sum(sel, axis=0, keepdims=True)
    cnt_ref[...] = run_ref[...]


def _router(u2t, router_w, router_b, *, tm):
    n, dm = u2t.shape[0] // V7X_SUBLANES, router_w.shape[0]
    w_pad = jnp.pad(router_w, ((0, 0), (0, V7X_LANES - N_EXPERTS)))
    b_pad = jnp.pad(router_b, (0, V7X_LANES - N_EXPERTS), constant_values=NEG_INF).reshape(1, V7X_LANES)
    tok = pl.BlockSpec((tm, V7X_LANES), lambda i: (i, 0))
    return pl.pallas_call(
        functools.partial(_router_body, tm=tm),
        grid=(n // tm,),
        in_specs=[pl.BlockSpec((tm * V7X_SUBLANES, V7X_LANES), lambda i: (i, 0)),
                  pl.BlockSpec((dm, V7X_LANES), lambda i: (0, 0)),
                  pl.BlockSpec((1, V7X_LANES), lambda i: (0, 0))],
        out_specs=[tok, tok, tok, pl.BlockSpec((1, V7X_LANES), lambda i: (0, 0))],
        out_shape=[jax.ShapeDtypeStruct((n, V7X_LANES), I32), jax.ShapeDtypeStruct((n, V7X_LANES), I32),
                   jax.ShapeDtypeStruct((n, V7X_LANES), F32), jax.ShapeDtypeStruct((1, V7X_LANES), F32)],
        scratch_shapes=[pltpu.VMEM((1, V7X_LANES), F32)],
        compiler_params=_cparams(("arbitrary",), 32),
        name="router",
    )(u2t, w_pad, b_pad)


def _dest_body(ps_ref, ri_ref, rr_ref, d_ref):
    idx = ri_ref[...]
    acc = rr_ref[...]
    for e in range(N_EXPERTS):
        acc = acc + jnp.where(idx == e, ps_ref[e], 0)
    d_ref[...] = acc


def _dest_rows(route_i, route_r, pad_starts, *, tm):
    n = route_i.shape[0]
    tok = pl.BlockSpec((tm, V7X_LANES), lambda i: (i, 0))
    return pl.pallas_call(
        _dest_body,
        grid=(n // tm,),
        in_specs=[pl.BlockSpec(memory_space=pltpu.SMEM), tok, tok],
        out_specs=tok,
        out_shape=jax.ShapeDtypeStruct((n, V7X_LANES), I32),
        compiler_params=_cparams(("parallel",), 32),
        name="dest_rows",
    )(pad_starts, route_i, route_r)


ISSUE_TOKENS = 2


def _row_tile(ref, row):
    return ref.at[pl.ds(pl.multiple_of(row * V7X_SUBLANES, V7X_SUBLANES), V7X_SUBLANES), :]


def _issue_rows(dest_ref, n_tok, start_copy):
    def trip(it, carry):
        base = it * ISSUE_TOKENS
        rows = [dest_ref[0, 0, (base + u) * TOP_K + kk] for u in range(ISSUE_TOKENS) for kk in range(TOP_K)]
        for u in range(ISSUE_TOKENS):
            for kk in range(TOP_K):
                start_copy(base + u, kk, rows[u * TOP_K + kk])
        return carry

    lax.fori_loop(0, n_tok // ISSUE_TOKENS, trip, 0)


def _dispatch_body(dest_ref, u_ref, xs_in, xs_out, sem, *, tm):
    del xs_in

    def start_copy(r, kk, row):
        pltpu.make_async_copy(_row_tile(u_ref, r), _row_tile(xs_out, row), sem).start()

    _issue_rows(dest_ref, tm, start_copy)
    for _ in range(TOP_K):
        pltpu.make_async_copy(u_ref, xs_out.at[pl.ds(0, tm * V7X_SUBLANES), :], sem).wait()


def _dispatch(u2t, dest_sm, n_rows, *, tm):
    n = u2t.shape[0] // V7X_SUBLANES
    return pl.pallas_call(
        functools.partial(_dispatch_body, tm=tm),
        grid=(n // tm,),
        in_specs=[pl.BlockSpec((1, 1, tm * TOP_K), lambda i: (i, 0, 0), memory_space=pltpu.SMEM),
                  pl.BlockSpec((tm * V7X_SUBLANES, V7X_LANES), lambda i: (i, 0)),
                  pl.BlockSpec(memory_space=pl.ANY)],
        out_specs=pl.BlockSpec(memory_space=pl.ANY),
        scratch_shapes=[pltpu.SemaphoreType.DMA(())],
        out_shape=jax.ShapeDtypeStruct((n_rows * V7X_SUBLANES, V7X_LANES), F32),
        input_output_aliases={2: 0},
        compiler_params=_cparams(("arbitrary",), 32),
        name="dispatch",
    )(dest_sm, u2t, jnp.zeros((n_rows * V7X_SUBLANES, V7X_LANES), F32))


def _gmm_body(be_ref, bv_ref, xs_ref, wu_ref, bu_ref, wd_ref, bd_ref, ys_ref, wu_bf, wd_bf, *, bm):
    j = pl.program_id(0)
    e = be_ref[j]
    prev = be_ref[jnp.maximum(j - 1, 0)]
    d_ff = wd_ref.shape[1]
    chunk = 64

    @pl.when((j == 0) | (e != prev))
    def _():
        def cast(c, carry):
            sl = pl.ds(pl.multiple_of(c * chunk, chunk), chunk)
            wu_bf[sl, :] = wu_ref[0, sl, :].astype(BF16)
            wd_bf[sl, :] = wd_ref[0, sl, :].astype(BF16)
            return carry

        lax.fori_loop(0, wu_ref.shape[1] // chunk, cast, 0)

    @pl.when(bv_ref[j] != 0)
    def _():
        x = _from_row_tiles(xs_ref, bm).astype(BF16)
        glu = jnp.dot(x, wu_bf[:, :d_ff], preferred_element_type=F32) + bu_ref[0, :, :d_ff]
        lin = jnp.dot(x, wu_bf[:, d_ff:], preferred_element_type=F32) + bu_ref[0, :, d_ff:]
        glu = jnp.minimum(glu, SWIGLU_LIMIT)
        lin = jnp.clip(lin, -SWIGLU_LIMIT, SWIGLU_LIMIT)
        act = glu * (1.0 / (1.0 + jnp.exp(-SWIGLU_ALPHA * glu))) * (lin + 1.0)
        _to_row_tiles(ys_ref, jnp.dot(act.astype(BF16), wd_bf[...], preferred_element_type=F32) + bd_ref[0])

    @pl.when(bv_ref[j] == 0)
    def _():
        ys_ref[...] = jnp.zeros_like(ys_ref)


def _gmm(xs, blk_e, blk_valid, w_up, b_up, w_down, b_down, *, bm):
    n_rows = xs.shape[0] // V7X_SUBLANES
    n_e, dm, d_up = w_up.shape
    d_ff = w_down.shape[1]
    assert d_ff == dm
    row_tiles = pl.BlockSpec((bm * V7X_SUBLANES, V7X_LANES), lambda j, be, bv: (j, 0))
    grid_spec = pltpu.PrefetchScalarGridSpec(
        num_scalar_prefetch=2,
        grid=(n_rows // bm,),
        in_specs=[row_tiles,
                  pl.BlockSpec((1, dm, d_up), lambda j, be, bv: (be[j], 0, 0)),
                  pl.BlockSpec((1, 1, d_up), lambda j, be, bv: (be[j], 0, 0)),
                  pl.BlockSpec((1, d_ff, dm), lambda j, be, bv: (be[j], 0, 0)),
                  pl.BlockSpec((1, 1, dm), lambda j, be, bv: (be[j], 0, 0))],
        out_specs=row_tiles,
        scratch_shapes=[pltpu.VMEM((dm, d_up), BF16), pltpu.VMEM((d_ff, dm), BF16)],
    )
    return pl.pallas_call(
        functools.partial(_gmm_body, bm=bm),
        grid_spec=grid_spec,
        out_shape=jax.ShapeDtypeStruct(xs.shape, F32),
        compiler_params=_cparams(("arbitrary",), 56),
        name="expert_mlp",
    )(blk_e, blk_valid, xs, w_up, b_up.reshape(n_e, 1, d_up), w_down, b_down.reshape(n_e, 1, dm))


def _combine_body(rt_ref, rtn_ref, ys_hbm, rg_ref, x1_ref, g2_ref, lg_ref, lb_ref, y_ref, rows_ref, sems,
                  *, nb, tm, alpha, n_steps):
    n_tok = nb * tm
    i = pl.program_id(0)
    slot = i % 2

    def issue(dest_ref, s):
        def start_copy(r, kk, row):
            pltpu.make_async_copy(_row_tile(ys_hbm, row), _row_tile(rows_ref.at[s, kk], r), sems.at[s]).start()

        _issue_rows(dest_ref, n_tok, start_copy)

    @pl.when(i == 0)
    def _():
        issue(rt_ref, 0)

    @pl.when(i + 1 < n_steps)
    def _():
        issue(rtn_ref, 1 - slot)

    for kk in range(TOP_K):
        pltpu.make_async_copy(ys_hbm.at[pl.ds(0, n_tok * V7X_SUBLANES), :], rows_ref.at[slot, kk],
                              sems.at[slot]).wait()

    gates = rg_ref[...]
    ffn = gates[:, 0:1] * _from_row_tiles(rows_ref, n_tok, (slot, 0))
    for kk in range(1, TOP_K):
        ffn = ffn + gates[:, kk:kk + 1] * _from_row_tiles(rows_ref, n_tok, (slot, kk))
    dm = ffn.shape[-1]
    y = alpha * x1_ref[...].reshape(nb, tm, dm) + (1.0 + g2_ref[...]) * ffn.reshape(nb, tm, dm)
    y_ref[...] = _layer_norm(y, lg_ref[...], lb_ref[...])


def _combine(ys, dest_sm, route_g, x1_all, ada3, ln2_g, ln2_b, *, bsz, seq, nb, tm, alpha, row0):
    dm = x1_all.shape[1]
    n_tok = nb * tm
    assert row0 % n_tok == 0 and dest_sm.shape[2] == n_tok * TOP_K and n_tok % ISSUE_TOKENS == 0
    blk0 = row0 // n_tok
    st = seq // tm
    n_steps = (bsz // nb) * st
    dest = lambda off: pl.BlockSpec((1, 1, n_tok * TOP_K),
                                    lambda i: (blk0 + jnp.minimum(i + off, n_steps - 1), 0, 0),
                                    memory_space=pltpu.SMEM)
    return pl.pallas_call(
        functools.partial(_combine_body, nb=nb, tm=tm, alpha=alpha, n_steps=n_steps),
        grid=(n_steps,),
        in_specs=[dest(0), dest(1),
                  pl.BlockSpec(memory_space=pl.ANY),
                  pl.BlockSpec((n_tok, V7X_LANES), lambda i: (blk0 + i, 0)),
                  pl.BlockSpec((n_tok, dm), lambda i: (blk0 + i, 0)),
                  pl.BlockSpec((nb, 1, dm), lambda i: (i // st, 0, _GATE2)),
                  pl.BlockSpec((1, dm), lambda i: (0, 0)),
                  pl.BlockSpec((1, dm), lambda i: (0, 0))],
        out_specs=pl.BlockSpec((nb, tm, dm), lambda i: (i // st, i % st, 0)),
        scratch_shapes=[pltpu.VMEM((2, TOP_K, n_tok * V7X_SUBLANES, V7X_LANES), F32),
                        pltpu.SemaphoreType.DMA((2,))],
        out_shape=jax.ShapeDtypeStruct((bsz, seq, dm), F32),
        compiler_params=_cparams(("arbitrary",), 48),
        name="combine_prompt" if row0 == 0 else "combine_sample",
    )(dest_sm, dest_sm, ys, route_g, x1_all, ada3, ln2_g.reshape(1, dm), ln2_b.reshape(1, dm))


def _layer(xp, xs, cache_k, cache_v, conv_past, cp, cs, w_ada, b_ada, w_in, conv_w, norm_attn_g, norm_conv_g,
           w_o, ln1_g, ln1_b, bias_tabs, router_w, router_b, w_up, b_up, w_down, b_down, ln2_g, ln2_b, alpha):
    bp, sp, dm = xp.shape
    bs, ts, _ = xs.shape
    n_p, n_s = bp * sp, bs * ts
    n_tok = n_p + n_s
    assert n_p % TOK_TILE == 0 and n_s == TOK_TILE and sp % ROW_TILE == 0
    bias_p, bias_c, bias_n = bias_tabs

    ada = _ada(jnp.concatenate([cp, cs], axis=0), w_ada, b_ada)
    ada_p = ada[:bp].reshape(bp, 1, -1)
    ada_s = ada[bp:].reshape(bs, 1, -1)
    w_in_bf = w_in.astype(BF16)
    w_o_bf = w_o.astype(BF16)

    qp, kp, vp, convp, cstp, kpt, vpt = _inproj(xp, ada_p, w_in_bf, conv_w, norm_conv_g, None, nb=1, tm=ROW_TILE)
    attn_p = _attn_prompt(qp, kp, vp, bias_p)
    x1_all, u2_all = _outproj(attn_p, convp, xp, ada_p, w_o_bf, norm_attn_g, ln1_g, ln1_b,
                              nb=1, tm=ROW_TILE, alpha=alpha, n_total=n_tok, row0=0)
    qs, ks, vs, convs, csts = _inproj(xs, ada_s, w_in_bf, conv_w, norm_conv_g, conv_past, nb=bs, tm=ts)
    heads = lambda a: jnp.transpose(a.reshape(bs, ts, N_HEADS, HEAD_DIM), (0, 2, 1, 3))
    n_past = cache_k.shape[1]
    feat_major = lambda c: jnp.transpose(c.reshape(bs, n_past, D_ATTN), (0, 2, 1))
    attn_s = _attn_sample(heads(qs), heads(ks), heads(vs), feat_major(cache_k), feat_major(cache_v), bias_c, bias_n)
    attn_s = jnp.transpose(attn_s, (0, 2, 1, 3)).reshape(bs, ts, D_ATTN)
    x1_all, u2_all = _outproj(attn_s, convs, xs, ada_s, w_o_bf, norm_attn_g, ln1_g, ln1_b,
                              nb=bs, tm=ts, alpha=alpha, n_total=n_tok, row0=n_p, prev=(x1_all, u2_all))

    route_i, route_r, route_g, counts = _router(u2_all, router_w, router_b, tm=TOK_TILE)
    counts = counts[0, :N_EXPERTS].astype(I32)
    padded = ((counts + EXPERT_BLK - 1) // EXPERT_BLK) * EXPERT_BLK
    pad_ends = jnp.cumsum(padded)
    pad_starts = (pad_ends - padded).astype(I32)
    n_blocks = n_tok * TOP_K // EXPERT_BLK + N_EXPERTS
    n_rows = n_blocks * EXPERT_BLK
    blk_row = jnp.arange(n_blocks, dtype=I32) * EXPERT_BLK
    blk_valid = (blk_row < pad_ends[-1]).astype(I32)
    blk_e = jnp.sum((blk_row[:, None] >= pad_ends[None, :]).astype(I32), axis=1)
    last_e = jnp.sum((pad_ends[-1] - 1 >= pad_ends).astype(I32))
    blk_e = jnp.where(blk_valid != 0, blk_e, last_e).astype(I32)
    dest = _dest_rows(route_i, route_r, pad_starts, tm=TOK_TILE)
    dest_sm = dest[:, :TOP_K].reshape(n_tok // TOK_TILE, 1, TOK_TILE * TOP_K)

    x_sorted = _dispatch(u2_all, dest_sm, n_rows, tm=TOK_TILE)
    y_sorted = _gmm(x_sorted, blk_e, blk_valid, w_up, b_up, w_down, b_down, bm=EXPERT_BLK)
    yp = _combine(y_sorted, dest_sm, route_g, x1_all, ada_p, ln2_g, ln2_b,
                  bsz=bp, seq=sp, nb=1, tm=TOK_TILE, alpha=alpha, row0=0)
    ys_out = _combine(y_sorted, dest_sm, route_g, x1_all, ada_s, ln2_g, ln2_b,
                      bsz=bs, seq=ts, nb=bs, tm=ts, alpha=alpha, row0=n_p)
    return yp, ys_out, kpt, vpt, cstp, ks, vs, csts


def kernel(x_prompt, x_sample, cache_k, cache_v, state_conv, c_prompt, c_sample, w_ada, b_ada, w_in, conv_w,
           norm_attn_g, norm_conv_g, w_o, ln1_g, ln1_b, rel_bias, router_w, router_b, w_up, b_up, w_down, b_down,
           ln2_g, ln2_b):
    depth = w_ada.shape[0]
    alpha = (2 * depth) ** 0.25
    xp, xs = x_prompt, x_sample
    bp, sp, _ = xp.shape
    bs, ts, _ = xs.shape
    n_keep = min(BRANCHES[-1][0], sp)
    bias_tabs = _bias_tables(rel_bias, ts, cache_k.shape[2])
    outs = [[] for _ in range(6)]
    for l in range(depth):
        xp, xs, kp, vp, cstp, ks, vs, csts = _layer(
            xp, xs, cache_k[l], cache_v[l], state_conv[l], c_prompt, c_sample, w_ada[l], b_ada[l], w_in[l],
            conv_w[l], norm_attn_g[l], norm_conv_g[l], w_o[l], ln1_g[l], ln1_b[l], bias_tabs, router_w[l],
            router_b[l], w_up[l], b_up[l], w_down[l], b_down[l], ln2_g[l], ln2_b[l], alpha)
        kp = jnp.transpose(kp, (0, 2, 1)).reshape(bp, sp, N_HEADS, HEAD_DIM)[:, -n_keep:]
        vp = jnp.transpose(vp, (0, 2, 1)).reshape(bp, sp, N_HEADS, HEAD_DIM)[:, -n_keep:]
        for lst, val in zip(outs, (kp, vp, cstp, ks.reshape(bs, ts, N_HEADS, HEAD_DIM),
                                   vs.reshape(bs, ts, N_HEADS, HEAD_DIM), csts)):
            lst.append(val)
    return (xp, xs) + tuple(jnp.stack(o) for o in outs)
```

```python
import functools
import math

import numpy as np
import jax
import jax.numpy as jnp
from jax import lax
from jax.experimental import pallas as pl
from jax.experimental.pallas import tpu as pltpu

F32 = jnp.float32
BF16 = jnp.bfloat16
I32 = jnp.int32

HEAD_DIM = 64
N_HEADS = 12
D_ATTN = N_HEADS * HEAD_DIM
D_CONV = 256
CONV_WIDTH = 3
BRANCHES = ((128, 1), (512, 4), (2048, 16))
NUM_BUCKETS = 32
MAX_DISTANCE = 2048
N_EXPERTS = 32
TOP_K = 4
SWIGLU_LIMIT = 7.0
SWIGLU_ALPHA = 1.702
LN_EPS = 1e-5
RMS_EPS = 1e-6
NEG_INF = -1e30

V7X_LANES = 128
V7X_SUBLANES = 8
V7X_VMEM_BYTES = 64 * 1024 * 1024

Q_BLK = 128
PAIR = 2 * HEAD_DIM
N_PAIRS = N_HEADS // 2
TOK_TILE = 256
ROW_TILE = 512
EXPERT_BLK = 256
ATTN_UNROLL = 8


def _cparams(sem, vmem_mb):
    return pltpu.CompilerParams(dimension_semantics=sem, vmem_limit_bytes=vmem_mb * 1024 * 1024)


def _t5_bucket_np(dist):
    dist = np.asarray(dist, np.int64)
    max_exact = NUM_BUCKETS // 2
    df = np.maximum(dist, max_exact).astype(np.float32)
    large = max_exact + (np.log(df / np.float32(max_exact)) / np.float32(math.log(MAX_DISTANCE / max_exact))
                         * np.float32(NUM_BUCKETS - max_exact)).astype(np.int32)
    return np.where(dist < max_exact, dist, np.minimum(large, NUM_BUCKETS - 1)).astype(np.int32)


def _prompt_bucket_index():
    a = np.arange(Q_BLK)[:, None]
    c = np.arange(2 * Q_BLK)[None, :]
    step = Q_BLK + a - c
    valid = (step >= 0) & (step <= Q_BLK)
    out = []
    for _, d in BRANCHES:
        out.append(np.where(valid, _t5_bucket_np(np.clip(step, 0, Q_BLK) * d), -1))
    return np.stack(out).astype(np.int32)


def _sample_bucket_index(n_new, n_past):
    t = np.arange(n_new)[:, None]
    out_c, out_n = [], []
    for w, d in BRANCHES:
        dist_c = n_past + t - np.arange(n_past)[None, :]
        dist_n = t - np.arange(n_new)[None, :]
        for dist, out in ((dist_c, out_c), (dist_n, out_n)):
            valid = (dist >= 0) & (dist <= w) & (dist % d == 0)
            out.append(np.where(valid, _t5_bucket_np(np.clip(dist, 0, w)), -1))
    return np.stack(out_c).astype(np.int32), np.stack(out_n).astype(np.int32)


def _bias_body(rb_ref, idx_ref, out_ref):
    idx = idx_ref[0]
    for h in range(N_HEADS):
        acc = jnp.where(idx < 0, NEG_INF, 0.0).astype(F32)
        for b in range(NUM_BUCKETS):
            acc = acc + jnp.where(idx == b, rb_ref[b * N_HEADS + h], 0.0)
        out_ref[0, h] = acc


def _bias_expand(rel_bias, idx_np, name):
    nbr, r, c = idx_np.shape
    return pl.pallas_call(
        _bias_body,
        grid=(nbr,),
        in_specs=[pl.BlockSpec(memory_space=pltpu.SMEM),
                  pl.BlockSpec((1, r, c), lambda i: (i, 0, 0))],
        out_specs=pl.BlockSpec((1, N_HEADS, r, c), lambda i: (i, 0, 0, 0)),
        out_shape=jax.ShapeDtypeStruct((nbr, N_HEADS, r, c), F32),
        name=name,
    )(rel_bias.reshape(-1), jnp.asarray(idx_np))


def _bias_tables(rel_bias, n_new, n_past):
    bias_p = _bias_expand(rel_bias, _prompt_bucket_index(), "bias_prompt")
    bias_p = bias_p.reshape(len(BRANCHES), N_PAIRS, 2 * Q_BLK, 2 * Q_BLK)
    ic, inw = _sample_bucket_index(n_new, n_past)
    return bias_p, _bias_expand(rel_bias, ic, "bias_cache"), _bias_expand(rel_bias, inw, "bias_new")


def _ada_body(c_ref, w_ref, b_ref, o_ref):
    c = c_ref[...]
    s = c * (1.0 / (1.0 + jnp.exp(-c)))
    o_ref[...] = jnp.dot(s.astype(BF16), w_ref[...].astype(BF16), preferred_element_type=F32) + b_ref[...]


def _ada(c_all, w_ada, b_ada):
    n, dm = c_all.shape
    n_out = w_ada.shape[1]
    tn = dm
    return pl.pallas_call(
        _ada_body,
        grid=(n_out // tn,),
        in_specs=[pl.BlockSpec((n, dm), lambda j: (0, 0)),
                  pl.BlockSpec((dm, tn), lambda j: (0, j)),
                  pl.BlockSpec((1, tn), lambda j: (0, j))],
        out_specs=pl.BlockSpec((n, tn), lambda j: (0, j)),
        out_shape=jax.ShapeDtypeStruct((n, n_out), F32),
        compiler_params=_cparams(("parallel",), 32),
        name="ada",
    )(c_all, w_ada, b_ada.reshape(1, n_out))


_SHIFT1, _SCALE1, _GATE1, _SHIFT2, _SCALE2, _GATE2 = range(6)


def _inproj_body(*refs, nb, tm, carry):
    if carry:
        (x_ref, sh_ref, sc_ref, w_ref, cw_ref, ng_ref, past_ref, wt_ref,
         q_ref, k_ref, v_ref, conv_ref, cst_ref, kt_ref, vt_ref, zz_ref) = refs
    else:
        (x_ref, sh_ref, sc_ref, w_ref, cw_ref, ng_ref, past_ref,
         q_ref, k_ref, v_ref, conv_ref, cst_ref, zz_ref) = refs
    dm = x_ref.shape[-1]
    rows = nb * tm
    u = x_ref[...] * (1.0 + sc_ref[...]) + sh_ref[...]
    u = u.reshape(rows, dm).astype(BF16)

    def proj(lo, width):
        return jnp.dot(u, w_ref[:, lo:lo + width], preferred_element_type=F32)

    q_ref[...] = proj(0, D_ATTN).reshape(nb, tm, D_ATTN)
    k_ref[...] = proj(D_ATTN, D_ATTN).reshape(nb, tm, D_ATTN)
    v_ref[...] = proj(2 * D_ATTN, D_ATTN).reshape(nb, tm, D_ATTN)
    if carry:
        nt = (((1,), (1,)), ((), ()))
        kt_ref[0] = lax.dot_general(wt_ref[D_ATTN:2 * D_ATTN, :], u, nt, preferred_element_type=F32)
        vt_ref[0] = lax.dot_general(wt_ref[2 * D_ATTN:3 * D_ATTN, :], u, nt, preferred_element_type=F32)
    gb = proj(3 * D_ATTN, D_CONV)
    gc = proj(3 * D_ATTN + D_CONV, D_CONV)
    hh = proj(3 * D_ATTN + 2 * D_CONV, D_CONV)
    z = (gc * hh).reshape(nb, tm, D_CONV)

    if carry:
        s = pl.program_id(1)

        @pl.when(s == 0)
        def _():
            zz_ref[:, 0:V7X_SUBLANES, :] = jnp.zeros((nb, V7X_SUBLANES, D_CONV), F32)

        @pl.when(s > 0)
        def _():
            zz_ref[:, 0:V7X_SUBLANES, :] = zz_ref[:, tm:tm + V7X_SUBLANES, :]
    else:
        zz_ref[:, V7X_SUBLANES - 2:V7X_SUBLANES, :] = past_ref[...]
    zz_ref[:, V7X_SUBLANES:, :] = z

    cw = cw_ref[...]
    yc = (cw[0:1, :] * zz_ref[:, V7X_SUBLANES - 2:V7X_SUBLANES - 2 + tm, :]
          + cw[1:2, :] * zz_ref[:, V7X_SUBLANES - 1:V7X_SUBLANES - 1 + tm, :]
          + cw[2:3, :] * z)
    g = gb.reshape(nb, tm, D_CONV) * yc
    ms = jnp.mean(g * g, axis=-1, keepdims=True)
    conv_ref[...] = g * lax.rsqrt(ms + RMS_EPS) * ng_ref[...]
    cst_ref[...] = zz_ref[:, tm + V7X_SUBLANES - 2:tm + V7X_SUBLANES, :]


def _inproj(x, ada3, w_in_bf, conv_w, norm_conv_g, past, *, nb, tm):
    bsz, seq, dm = x.shape
    carry = past is None
    if carry:
        past = jnp.zeros((bsz, CONV_WIDTH - 1, D_CONV), F32)
    grid = (bsz // nb, seq // tm)
    d_in = w_in_bf.shape[1]
    body = functools.partial(_inproj_body, nb=nb, tm=tm, carry=carry)
    tile = lambda width: pl.BlockSpec((nb, tm, width), lambda b, s: (b, s, 0))
    in_specs = [tile(dm),
                pl.BlockSpec((nb, 1, dm), lambda b, s: (b, 0, _SHIFT1)),
                pl.BlockSpec((nb, 1, dm), lambda b, s: (b, 0, _SCALE1)),
                pl.BlockSpec((dm, d_in), lambda b, s: (0, 0)),
                pl.BlockSpec((CONV_WIDTH, D_CONV), lambda b, s: (0, 0)),
                pl.BlockSpec((1, D_CONV), lambda b, s: (0, 0)),
                pl.BlockSpec((nb, CONV_WIDTH - 1, D_CONV), lambda b, s: (b, 0, 0))]
    args = [x, ada3, ada3, w_in_bf, conv_w, norm_conv_g.reshape(1, D_CONV), past]
    out_specs = [tile(D_ATTN), tile(D_ATTN), tile(D_ATTN), tile(D_CONV),
                 pl.BlockSpec((nb, CONV_WIDTH - 1, D_CONV), lambda b, s: (b, 0, 0))]
    out_shape = ([jax.ShapeDtypeStruct((bsz, seq, D_ATTN), F32)] * 3
                 + [jax.ShapeDtypeStruct((bsz, seq, D_CONV), F32),
                    jax.ShapeDtypeStruct((bsz, CONV_WIDTH - 1, D_CONV), F32)])
    if carry:
        assert nb == 1
        in_specs.append(pl.BlockSpec((d_in, dm), lambda b, s: (0, 0)))
        args.append(w_in_bf.T)
        out_specs += [pl.BlockSpec((1, D_ATTN, tm), lambda b, s: (b, 0, s))] * 2
        out_shape += [jax.ShapeDtypeStruct((bsz, D_ATTN, seq), F32)] * 2
    return pl.pallas_call(
        body,
        grid=grid,
        in_specs=in_specs,
        out_specs=out_specs,
        out_shape=out_shape,
        scratch_shapes=[pltpu.VMEM((nb, tm + V7X_SUBLANES, D_CONV), F32)],
        compiler_params=_cparams(("parallel", "arbitrary"), 56),
        name="inproj_prompt" if carry else "inproj_sample",
    )(*args)


def _attn_prompt_body(q_ref, k_ref, v_ref, bias_ref, o_ref, ob_ref, lb_ref, *, seq):
    scale = HEAD_DIM ** -0.5
    lane = lax.broadcasted_iota(I32, (Q_BLK, PAIR), 1)
    even = lane < HEAD_DIM
    ones = jnp.ones((Q_BLK, PAIR), BF16)

    def rows(ref, start, d):
        if d == 1:
            return ref[pl.ds(start, Q_BLK), :]
        return ref[pl.ds(start, Q_BLK, stride=d), :]

    def store(ref, br, start, d, val):
        if d == 1:
            ref[br, pl.ds(start, Q_BLK), :] = val
        else:
            ref[br, pl.ds(start, Q_BLK, stride=d), :] = val

    for br, (_, d) in enumerate(BRANCHES):
        nblk = seq // (d * Q_BLK)

        def block(it, carry, br=br, d=d, nblk=nblk):
            bias = bias_ref[br, 0]
            r = it // nblk
            i = it - r * nblk
            cur0 = r + d * Q_BLK * i
            qf = rows(q_ref, cur0, d) * scale
            qs = jnp.concatenate([jnp.where(even, qf, 0.0), jnp.where(even, 0.0, qf)], axis=0).astype(BF16)
            kc = rows(k_ref, cur0, d).astype(BF16)
            vc = rows(v_ref, cur0, d).astype(BF16)
            if nblk == 1:
                s = lax.dot_general(qs, kc, (((1,), (1,)), ((), ())), preferred_element_type=F32)
                s = s + bias[:, Q_BLK:]
                vaug = jnp.concatenate([vc, ones], axis=1)
            else:
                prev0 = r + d * Q_BLK * jnp.maximum(i - 1, 0)
                kp = rows(k_ref, prev0, d).astype(BF16)
                vp = rows(v_ref, prev0, d).astype(BF16)
                kk = jnp.concatenate([kp, kc], axis=0)
                s = lax.dot_general(qs, kk, (((1,), (1,)), ((), ())), preferred_element_type=F32)
                col = lax.broadcasted_iota(I32, (1, 2 * Q_BLK), 1)
                first = jnp.where((col < Q_BLK) & (i == 0), NEG_INF, 0.0)
                s = s + bias + first
                vaug = jnp.concatenate([jnp.concatenate([vp, vc], axis=0),
                                        jnp.concatenate([ones, ones], axis=0)], axis=1)
            m = jnp.max(s, axis=-1, keepdims=True)
            p = jnp.exp(s - m).astype(BF16)
            out = jnp.dot(p, vaug, preferred_element_type=F32)
            num, den = out[:, :PAIR], out[:, PAIR:]
            o = num / den
            lse = m + jnp.log(den)
            store(ob_ref, br, cur0, d, jnp.where(even, o[:Q_BLK], o[Q_BLK:]))
            store(lb_ref, br, cur0, d, jnp.where(even, lse[:Q_BLK], lse[Q_BLK:]))
            return carry

        lax.fori_loop(0, d * nblk, block, 0, unroll=ATTN_UNROLL)

    mt = 2 * Q_BLK

    def merge(i, carry):
        sl = pl.ds(pl.multiple_of(i * mt, mt), mt)
        l0, l1, l2 = lb_ref[0, sl, :], lb_ref[1, sl, :], lb_ref[2, sl, :]
        mx = jnp.maximum(jnp.maximum(l0, l1), l2)
        w0, w1, w2 = jnp.exp(l0 - mx), jnp.exp(l1 - mx), jnp.exp(l2 - mx)
        acc = w0 * ob_ref[0, sl, :] + w1 * ob_ref[1, sl, :] + w2 * ob_ref[2, sl, :]
        o_ref[sl, :] = acc / (w0 + w1 + w2)
        return carry

    lax.fori_loop(0, seq // mt, merge, 0)


def _attn_prompt(q, k, v, bias_p):
    bsz, seq, _ = q.shape
    assert seq % (BRANCHES[-1][1] * Q_BLK) == 0
    body = functools.partial(_attn_prompt_body, seq=seq)
    col = pl.BlockSpec((None, seq, PAIR), lambda b, j: (b, 0, j))
    return pl.pallas_call(
        body,
        grid=(bsz, N_PAIRS),
        in_specs=[col, col, col,
                  pl.BlockSpec((len(BRANCHES), 1, 2 * Q_BLK, 2 * Q_BLK), lambda b, j: (0, j, 0, 0))],
        out_specs=col,
        out_shape=jax.ShapeDtypeStruct((bsz, seq, D_ATTN), F32),
        scratch_shapes=[pltpu.VMEM((len(BRANCHES), seq, PAIR), F32),
                        pltpu.VMEM((len(BRANCHES), seq, PAIR), F32)],
        compiler_params=_cparams(("parallel", "parallel"), 48),
        name="attn_prompt",
    )(q, k, v, bias_p)


def _attn_sample_body(q_ref, kn_ref, vn_ref, ckt_ref, cvt_ref, bc_ref, bn_ref, o_ref):
    scale = HEAD_DIM ** -0.5
    nt = (((1,), (1,)), ((), ()))
    nbr = len(BRANCHES)
    for h in range(N_HEADS):
        rows = slice(h * HEAD_DIM, (h + 1) * HEAD_DIM)
        qh = (q_ref[h] * scale).astype(BF16)
        s_c = jnp.dot(qh, ckt_ref[rows, :].astype(BF16), preferred_element_type=F32)
        s_n = lax.dot_general(qh, kn_ref[h].astype(BF16), nt, preferred_element_type=F32)
        ps, lses, dens = [], [], []
        for br in range(nbr):
            lc = s_c + bc_ref[br, h]
            ln = s_n + bn_ref[br, h]
            m = jnp.maximum(jnp.max(lc, axis=-1, keepdims=True), jnp.max(ln, axis=-1, keepdims=True))
            pc = jnp.exp(lc - m)
            pn = jnp.exp(ln - m)
            den = jnp.sum(pc, axis=-1, keepdims=True) + jnp.sum(pn, axis=-1, keepdims=True)
            ps.append((pc, pn))
            dens.append(den)
            lses.append(m + jnp.log(den))
        mx = jnp.maximum(jnp.maximum(lses[0], lses[1]), lses[2])
        ws = [jnp.exp(l - mx) for l in lses]
        wsum = ws[0] + ws[1] + ws[2]
        coefs = [ws[br] / (wsum * dens[br]) for br in range(nbr)]
        p_c = coefs[0] * ps[0][0] + coefs[1] * ps[1][0] + coefs[2] * ps[2][0]
        p_n = coefs[0] * ps[0][1] + coefs[1] * ps[1][1] + coefs[2] * ps[2][1]
        o = lax.dot_general(p_c.astype(BF16), cvt_ref[rows, :].astype(BF16), nt, preferred_element_type=F32)
        o_ref[h] = o + jnp.dot(p_n.astype(BF16), vn_ref[h].astype(BF16), preferred_element_type=F32)


def _attn_sample(q, k_new, v_new, cache_kt, cache_vt, bias_c, bias_n):
    bsz, _, n_new, _ = q.shape
    n_past = cache_kt.shape[2]
    new = pl.BlockSpec((None, N_HEADS, n_new, HEAD_DIM), lambda b: (b, 0, 0, 0))
    cache = pl.BlockSpec((None, D_ATTN, n_past), lambda b: (b, 0, 0))
    return pl.pallas_call(
        _attn_sample_body,
        grid=(bsz,),
        in_specs=[new, new, new, cache, cache,
                  pl.BlockSpec(bias_c.shape, lambda b: (0, 0, 0, 0)), pl.BlockSpec(bias_n.shape, lambda b: (0, 0, 0, 0))],
        out_specs=new,
        out_shape=jax.ShapeDtypeStruct((bsz, N_HEADS, n_new, HEAD_DIM), F32),
        compiler_params=_cparams(("parallel",), 48),
        name="attn_sample",
    )(q, k_new, v_new, cache_kt, cache_vt, bias_c, bias_n)


def _layer_norm(y, g, b):
    mu = jnp.mean(y, axis=-1, keepdims=True)
    c = y - mu
    var = jnp.mean(c * c, axis=-1, keepdims=True)
    return c * lax.rsqrt(var + LN_EPS) * g + b


def _to_row_tiles(ref, val):
    rows, dm = val.shape
    assert dm == V7X_SUBLANES * V7X_LANES
    for c in range(V7X_SUBLANES):
        ref[pl.ds(c, rows, stride=V7X_SUBLANES), :] = val[:, c * V7X_LANES:(c + 1) * V7X_LANES]


def _from_row_tiles(ref, rows, lead=()):
    chunks = [ref[lead + (pl.ds(c, rows, stride=V7X_SUBLANES), slice(None))] for c in range(V7X_SUBLANES)]
    return jnp.concatenate(chunks, axis=1)


def _outproj_body(*refs, nb, tm, alpha, aliased, n_main):
    if aliased:
        refs = refs[2:]
    (a_ref, c_ref, x_ref, g1_ref, sh2_ref, sc2_ref, wa_ref, wc_ref, ng_ref, lg_ref, lb_ref, x1_ref, u2_ref) = refs
    rows = nb * tm

    @pl.when(pl.program_id(0) < n_main)
    def _():
        a = a_ref[...]
        ms = jnp.mean(a * a, axis=-1, keepdims=True)
        an = (a * lax.rsqrt(ms + RMS_EPS) * ng_ref[...]).reshape(rows, D_ATTN).astype(BF16)
        cn = c_ref[...].reshape(rows, D_CONV).astype(BF16)
        mix = (jnp.dot(an, wa_ref[...], preferred_element_type=F32)
               + jnp.dot(cn, wc_ref[...], preferred_element_type=F32))
        dm = mix.shape[-1]
        y = alpha * x_ref[...] + (1.0 + g1_ref[...]) * mix.reshape(nb, tm, dm)
        x1 = _layer_norm(y, lg_ref[...], lb_ref[...])
        x1_ref[...] = x1.reshape(rows, dm)
        _to_row_tiles(u2_ref, (x1 * (1.0 + sc2_ref[...]) + sh2_ref[...]).reshape(rows, dm))

    @pl.when(pl.program_id(0) >= n_main)
    def _():
        x1_ref[...] = jnp.zeros_like(x1_ref)
        u2_ref[...] = jnp.zeros_like(u2_ref)


def _outproj(attn, conv, x, ada3, w_o_bf, norm_attn_g, ln1_g, ln1_b, *, nb, tm, alpha, n_total, row0, prev=None):
    bsz, seq, dm = x.shape
    rows = nb * tm
    assert row0 % rows == 0
    st = seq // tm
    blk0 = row0 // rows
    n_main = (bsz // nb) * st
    aliased = prev is not None
    n_steps = n_main if aliased else pl.cdiv(n_total, rows)

    def bs_of(i):
        j = jnp.minimum(i, n_main - 1)
        return j // st, j % st

    tile = lambda width: pl.BlockSpec((nb, tm, width), lambda i: bs_of(i) + (0,))
    ada = lambda chunk: pl.BlockSpec((nb, 1, dm), lambda i: (bs_of(i)[0], 0, chunk))
    const = lambda shape: pl.BlockSpec(shape, lambda i: (0,) * len(shape))
    out = pl.BlockSpec((rows, dm), lambda i: (blk0 + i, 0))
    body = functools.partial(_outproj_body, nb=nb, tm=tm, alpha=alpha, aliased=aliased, n_main=n_main)
    in_specs = [tile(D_ATTN), tile(D_CONV), tile(dm), ada(_GATE1), ada(_SHIFT2), ada(_SCALE2),
                const((D_ATTN, dm)), const((D_CONV, dm)), const((1, D_ATTN)), const((1, dm)), const((1, dm))]
    args = [attn, conv, x, ada3, ada3, ada3, w_o_bf[:D_ATTN], w_o_bf[D_ATTN:],
            norm_attn_g.reshape(1, D_ATTN), ln1_g.reshape(1, dm), ln1_b.reshape(1, dm)]
    kwargs = {}
    if aliased:
        in_specs = [pl.BlockSpec(memory_space=pl.ANY)] * 2 + in_specs
        args = list(prev) + args
        kwargs["input_output_aliases"] = {0: 0, 1: 1}
    return pl.pallas_call(
        body,
        grid=(n_steps,),
        in_specs=in_specs,
        out_specs=[out, pl.BlockSpec((rows * V7X_SUBLANES, V7X_LANES), lambda i: (blk0 + i, 0))],
        out_shape=[jax.ShapeDtypeStruct((n_total, dm), F32),
                   jax.ShapeDtypeStruct((n_total * V7X_SUBLANES, V7X_LANES), F32)],
        compiler_params=_cparams(("parallel",), 48),
        name="outproj_sample" if aliased else "outproj_prompt",
        **kwargs,
    )(*args)


def _router_body(u_ref, w_ref, b_ref, ri_ref, rr_ref, rg_ref, cnt_ref, run_ref, *, tm):
    i = pl.program_id(0)

    @pl.when(i == 0)
    def _():
        run_ref[...] = jnp.zeros_like(run_ref)

    logits = jnp.dot(_from_row_tiles(u_ref, tm), w_ref[...], preferred_element_type=F32,
                     precision=lax.Precision.HIGHEST) + b_ref[...]
    lane_i = lax.broadcasted_iota(I32, (tm, V7X_LANES), 1)
    lane = lane_i.astype(F32)
    vals = logits
    tops, idxs = [], []
    for _ in range(TOP_K):
        mk = jnp.max(vals, axis=-1, keepdims=True)
        ik = jnp.min(jnp.where(vals == mk, lane, float(V7X_LANES)), axis=-1, keepdims=True)
        tops.append(mk)
        idxs.append(ik)
        vals = jnp.where(lane == ik, -jnp.inf, vals)
    es = [jnp.exp(t - tops[0]) for t in tops]
    den = es[0] + es[1] + es[2] + es[3]
    sel = jnp.zeros((tm, V7X_LANES), F32)
    for ik in idxs:
        sel = sel + jnp.where(lane == ik, 1.0, 0.0)
    ri = lax.broadcasted_iota(I32, (tm, tm), 0)
    ci = lax.broadcasted_iota(I32, (tm, tm), 1)
    lower = jnp.where(ci < ri, 1.0, 0.0).astype(BF16)
    before = jnp.dot(lower, sel.astype(BF16), preferred_element_type=F32) + run_ref[...]
    out_i = jnp.full((tm, V7X_LANES), -1, I32)
    out_r = jnp.zeros((tm, V7X_LANES), I32)
    out_g = jnp.zeros((tm, V7X_LANES), F32)
    for kk in range(TOP_K):
        rank = jnp.sum(jnp.where(lane == idxs[kk], before, 0.0), axis=-1, keepdims=True)
        out_i = jnp.where(lane_i == kk, idxs[kk].astype(I32), out_i)
        out_r = jnp.where(lane_i == kk, rank.astype(I32), out_r)
        out_g = jnp.where(lane_i == kk, es[kk] / den, out_g)
    ri_ref[...] = out_i
    rr_ref[...] = out_r
    rg_ref[...] = out_g
    run_ref[...] = run_ref[...] + jnp.sum(sel, axis=0, keepdims=True)
    cnt_ref[...] = run_ref[...]


def _router(u2t, router_w, router_b, *, tm):
    n, dm = u2t.shape[0] // V7X_SUBLANES, router_w.shape[0]
    w_pad = jnp.pad(router_w, ((0, 0), (0, V7X_LANES - N_EXPERTS)))
    b_pad = jnp.pad(router_b, (0, V7X_LANES - N_EXPERTS), constant_values=NEG_INF).reshape(1, V7X_LANES)
    tok = pl.BlockSpec((tm, V7X_LANES), lambda i: (i, 0))
    return pl.pallas_call(
        functools.partial(_router_body, tm=tm),
        grid=(n // tm,),
        in_specs=[pl.BlockSpec((tm * V7X_SUBLANES, V7X_LANES), lambda i: (i, 0)),
                  pl.BlockSpec((dm, V7X_LANES), lambda i: (0, 0)),
                  pl.BlockSpec((1, V7X_LANES), lambda i: (0, 0))],
        out_specs=[tok, tok, tok, pl.BlockSpec((1, V7X_LANES), lambda i: (0, 0))],
        out_shape=[jax.ShapeDtypeStruct((n, V7X_LANES), I32), jax.ShapeDtypeStruct((n, V7X_LANES), I32),
                   jax.ShapeDtypeStruct((n, V7X_LANES), F32), jax.ShapeDtypeStruct((1, V7X_LANES), F32)],
        scratch_shapes=[pltpu.VMEM((1, V7X_LANES), F32)],
        compiler_params=_cparams(("arbitrary",), 32),
        name="router",
    )(u2t, w_pad, b_pad)


def _dest_body(ps_ref, ri_ref, rr_ref, d_ref):
    idx = ri_ref[...]
    acc = rr_ref[...]
    for e in range(N_EXPERTS):
        acc = acc + jnp.where(idx == e, ps_ref[e], 0)
    d_ref[...] = acc


def _dest_rows(route_i, route_r, pad_starts, *, tm):
    n = route_i.shape[0]
    tok = pl.BlockSpec((tm, V7X_LANES), lambda i: (i, 0))
    return pl.pallas_call(
        _dest_body,
        grid=(n // tm,),
        in_specs=[pl.BlockSpec(memory_space=pltpu.SMEM), tok, tok],
        out_specs=tok,
        out_shape=jax.ShapeDtypeStruct((n, V7X_LANES), I32),
        compiler_params=_cparams(("parallel",), 32),
        name="dest_rows",
    )(pad_starts, route_i, route_r)


ISSUE_TOKENS = 2


def _row_tile(ref, row):
    return ref.at[pl.ds(pl.multiple_of(row * V7X_SUBLANES, V7X_SUBLANES), V7X_SUBLANES), :]


def _issue_rows(dest_ref, n_tok, start_copy):
    def trip(it, carry):
        base = it * ISSUE_TOKENS
        rows = [dest_ref[0, 0, (base + u) * TOP_K + kk] for u in range(ISSUE_TOKENS) for kk in range(TOP_K)]
        for u in range(ISSUE_TOKENS):
            for kk in range(TOP_K):
                start_copy(base + u, kk, rows[u * TOP_K + kk])
        return carry

    lax.fori_loop(0, n_tok // ISSUE_TOKENS, trip, 0)


def _dispatch_body(fs_ref, fl_ref, dest_ref, u_ref, xs_out, zero_ref, sem, fill_sem, *, tm, n_fill):
    i = pl.program_id(0)

    def start_copy(r, kk, row):
        pltpu.make_async_copy(_row_tile(u_ref, r), _row_tile(xs_out, row), sem).start()

    _issue_rows(dest_ref, tm, start_copy)

    @pl.when(i == pl.num_programs(0) - 1)
    def _():
        zero_ref[...] = jnp.zeros_like(zero_ref)

        def run(e, carry):
            def one(r, c):
                pltpu.make_async_copy(zero_ref, _row_tile(xs_out, fs_ref[e] + r), fill_sem).start()
                return c

            return lax.fori_loop(0, fl_ref[e], one, carry)

        lax.fori_loop(0, fs_ref.shape[0], run, 0)
        pltpu.make_async_copy(xs_out.at[pl.ds(0, n_fill * V7X_SUBLANES), :],
                              xs_out.at[pl.ds(0, n_fill * V7X_SUBLANES), :], fill_sem).wait()

    for _ in range(TOP_K):
        pltpu.make_async_copy(u_ref, xs_out.at[pl.ds(0, tm * V7X_SUBLANES), :], sem).wait()


def _dispatch(u2t, dest_sm, fill_start, fill_len, n_rows, *, tm):
    n = u2t.shape[0] // V7X_SUBLANES
    n_fill = n_rows - n * TOP_K
    grid_spec = pltpu.PrefetchScalarGridSpec(
        num_scalar_prefetch=2,
        grid=(n // tm,),
        in_specs=[pl.BlockSpec((1, 1, tm * TOP_K), lambda i, fs, fl: (i, 0, 0), memory_space=pltpu.SMEM),
                  pl.BlockSpec((tm * V7X_SUBLANES, V7X_LANES), lambda i, fs, fl: (i, 0))],
        out_specs=pl.BlockSpec(memory_space=pl.ANY),
        scratch_shapes=[pltpu.VMEM((V7X_SUBLANES, V7X_LANES), F32),
                        pltpu.SemaphoreType.DMA(()), pltpu.SemaphoreType.DMA(())],
    )
    return pl.pallas_call(
        functools.partial(_dispatch_body, tm=tm, n_fill=n_fill),
        grid_spec=grid_spec,
        out_shape=jax.ShapeDtypeStruct((n_rows * V7X_SUBLANES, V7X_LANES), F32),
        compiler_params=_cparams(("arbitrary",), 32),
        name="dispatch",
    )(fill_start, fill_len, dest_sm, u2t)


def _gmm_body(be_ref, bv_ref, bf_ref, bs_ref, bn_ref, xs_ref, wu_hbm, bu_ref, wd_hbm, bd_ref, ys_ref,
              wu_f32, wd_f32, wu_bf, wd_bf, sems, *, bm):
    j = pl.program_id(0)
    e = be_ref[j]
    s = bs_ref[j]
    d_ff = wd_hbm.shape[1]
    chunk = 64

    def fetch(ex, slot):
        return (pltpu.make_async_copy(wu_hbm.at[ex], wu_f32.at[slot], sems.at[0, slot]),
                pltpu.make_async_copy(wd_hbm.at[ex], wd_f32.at[slot], sems.at[1, slot]))

    @pl.when(bf_ref[j] != 0)
    def _():
        @pl.when(j == 0)
        def _():
            for c in fetch(e, s):
                c.start()

        for c in fetch(e, s):
            c.wait()

        @pl.when(bn_ref[j] >= 0)
        def _():
            for c in fetch(bn_ref[j], 1 - s):
                c.start()

        def cast(c, carry):
            sl = pl.ds(pl.multiple_of(c * chunk, chunk), chunk)
            wu_bf[sl, :] = wu_f32[s, sl, :].astype(BF16)
            wd_bf[sl, :] = wd_f32[s, sl, :].astype(BF16)
            return carry

        lax.fori_loop(0, wu_hbm.shape[1] // chunk, cast, 0)

    @pl.when(bv_ref[j] != 0)
    def _():
        x = _from_row_tiles(xs_ref, bm).astype(BF16)
        glu = jnp.dot(x, wu_bf[:, :d_ff], preferred_element_type=F32) + bu_ref[0, :, :d_ff]
        lin = jnp.dot(x, wu_bf[:, d_ff:], preferred_element_type=F32) + bu_ref[0, :, d_ff:]
        glu = jnp.minimum(glu, SWIGLU_LIMIT)
        lin = jnp.clip(lin, -SWIGLU_LIMIT, SWIGLU_LIMIT)
        act = glu * (1.0 / (1.0 + jnp.exp(-SWIGLU_ALPHA * glu))) * (lin + 1.0)
        _to_row_tiles(ys_ref, jnp.dot(act.astype(BF16), wd_bf[...], preferred_element_type=F32) + bd_ref[0])

    @pl.when(bv_ref[j] == 0)
    def _():
        ys_ref[...] = jnp.zeros_like(ys_ref)


def _gmm(xs, blk_e, blk_valid, blk_first, blk_slot, blk_next, w_up, b_up, w_down, b_down, *, bm):
    n_rows = xs.shape[0] // V7X_SUBLANES
    n_e, dm, d_up = w_up.shape
    d_ff = w_down.shape[1]
    assert d_ff == dm
    row_tiles = pl.BlockSpec((bm * V7X_SUBLANES, V7X_LANES), lambda j, *_: (j, 0))
    grid_spec = pltpu.PrefetchScalarGridSpec(
        num_scalar_prefetch=5,
        grid=(n_rows // bm,),
        in_specs=[row_tiles,
                  pl.BlockSpec(memory_space=pl.ANY),
                  pl.BlockSpec((1, 1, d_up), lambda j, be, *_: (be[j], 0, 0)),
                  pl.BlockSpec(memory_space=pl.ANY),
                  pl.BlockSpec((1, 1, dm), lambda j, be, *_: (be[j], 0, 0))],
        out_specs=row_tiles,
        scratch_shapes=[pltpu.VMEM((2, dm, d_up), F32), pltpu.VMEM((2, d_ff, dm), F32),
                        pltpu.VMEM((dm, d_up), BF16), pltpu.VMEM((d_ff, dm), BF16),
                        pltpu.SemaphoreType.DMA((2, 2))],
    )
    return pl.pallas_call(
        functools.partial(_gmm_body, bm=bm),
        grid_spec=grid_spec,
        out_shape=jax.ShapeDtypeStruct(xs.shape, F32),
        compiler_params=_cparams(("arbitrary",), 56),
        name="expert_mlp",
    )(blk_e, blk_valid, blk_first, blk_slot, blk_next, xs, w_up, b_up.reshape(n_e, 1, d_up), w_down,
      b_down.reshape(n_e, 1, dm))


def _combine_body(rt_ref, rtn_ref, ys_hbm, rg_ref, x1_ref, g2_ref, lg_ref, lb_ref, y_ref, rows_ref, sems,
                  *, nb, tm, alpha, n_steps):
    n_tok = nb * tm
    i = pl.program_id(0)
    slot = i % 2

    def issue(dest_ref, s):
        def start_copy(r, kk, row):
            pltpu.make_async_copy(_row_tile(ys_hbm, row), _row_tile(rows_ref.at[s, kk], r), sems.at[s]).start()

        _issue_rows(dest_ref, n_tok, start_copy)

    @pl.when(i == 0)
    def _():
        issue(rt_ref, 0)

    @pl.when(i + 1 < n_steps)
    def _():
        issue(rtn_ref, 1 - slot)

    for kk in range(TOP_K):
        pltpu.make_async_copy(ys_hbm.at[pl.ds(0, n_tok * V7X_SUBLANES), :], rows_ref.at[slot, kk],
                              sems.at[slot]).wait()

    gates = rg_ref[...]
    ffn = gates[:, 0:1] * _from_row_tiles(rows_ref, n_tok, (slot, 0))
    for kk in range(1, TOP_K):
        ffn = ffn + gates[:, kk:kk + 1] * _from_row_tiles(rows_ref, n_tok, (slot, kk))
    dm = ffn.shape[-1]
    y = alpha * x1_ref[...].reshape(nb, tm, dm) + (1.0 + g2_ref[...]) * ffn.reshape(nb, tm, dm)
    y_ref[...] = _layer_norm(y, lg_ref[...], lb_ref[...])


def _combine(ys, dest_sm, route_g, x1_all, ada3, ln2_g, ln2_b, *, bsz, seq, nb, tm, alpha, row0):
    dm = x1_all.shape[1]
    n_tok = nb * tm
    assert row0 % n_tok == 0 and dest_sm.shape[2] == n_tok * TOP_K and n_tok % ISSUE_TOKENS == 0
    blk0 = row0 // n_tok
    st = seq // tm
    n_steps = (bsz // nb) * st
    dest = lambda off: pl.BlockSpec((1, 1, n_tok * TOP_K),
                                    lambda i: (blk0 + jnp.minimum(i + off, n_steps - 1), 0, 0),
                                    memory_space=pltpu.SMEM)
    return pl.pallas_call(
        functools.partial(_combine_body, nb=nb, tm=tm, alpha=alpha, n_steps=n_steps),
        grid=(n_steps,),
        in_specs=[dest(0), dest(1),
                  pl.BlockSpec(memory_space=pl.ANY),
                  pl.BlockSpec((n_tok, V7X_LANES), lambda i: (blk0 + i, 0)),
                  pl.BlockSpec((n_tok, dm), lambda i: (blk0 + i, 0)),
                  pl.BlockSpec((nb, 1, dm), lambda i: (i // st, 0, _GATE2)),
                  pl.BlockSpec((1, dm), lambda i: (0, 0)),
                  pl.BlockSpec((1, dm), lambda i: (0, 0))],
        out_specs=pl.BlockSpec((nb, tm, dm), lambda i: (i // st, i % st, 0)),
        scratch_shapes=[pltpu.VMEM((2, TOP_K, n_tok * V7X_SUBLANES, V7X_LANES), F32),
                        pltpu.SemaphoreType.DMA((2,))],
        out_shape=jax.ShapeDtypeStruct((bsz, seq, dm), F32),
        compiler_params=_cparams(("arbitrary",), 48),
        name="combine_prompt" if row0 == 0 else "combine_sample",
    )(dest_sm, dest_sm, ys, route_g, x1_all, ada3, ln2_g.reshape(1, dm), ln2_b.reshape(1, dm))


def _layer(xp, xs, cache_k, cache_v, conv_past, cp, cs, w_ada, b_ada, w_in, conv_w, norm_attn_g, norm_conv_g,
           w_o, ln1_g, ln1_b, bias_tabs, router_w, router_b, w_up, b_up, w_down, b_down, ln2_g, ln2_b, alpha):
    bp, sp, dm = xp.shape
    bs, ts, _ = xs.shape
    n_p, n_s = bp * sp, bs * ts
    n_tok = n_p + n_s
    assert n_p % TOK_TILE == 0 and n_s == TOK_TILE and sp % ROW_TILE == 0
    bias_p, bias_c, bias_n = bias_tabs

    ada = _ada(jnp.concatenate([cp, cs], axis=0), w_ada, b_ada)
    ada_p = ada[:bp].reshape(bp, 1, -1)
    ada_s = ada[bp:].reshape(bs, 1, -1)
    w_in_bf = w_in.astype(BF16)
    w_o_bf = w_o.astype(BF16)

    qp, kp, vp, convp, cstp, kpt, vpt = _inproj(xp, ada_p, w_in_bf, conv_w, norm_conv_g, None, nb=1, tm=ROW_TILE)
    attn_p = _attn_prompt(qp, kp, vp, bias_p)
    x1_all, u2_all = _outproj(attn_p, convp, xp, ada_p, w_o_bf, norm_attn_g, ln1_g, ln1_b,
                              nb=1, tm=ROW_TILE, alpha=alpha, n_total=n_tok, row0=0)
    qs, ks, vs, convs, csts = _inproj(xs, ada_s, w_in_bf, conv_w, norm_conv_g, conv_past, nb=bs, tm=ts)
    heads = lambda a: jnp.transpose(a.reshape(bs, ts, N_HEADS, HEAD_DIM), (0, 2, 1, 3))
    n_past = cache_k.shape[1]
    feat_major = lambda c: jnp.transpose(c.reshape(bs, n_past, D_ATTN), (0, 2, 1))
    attn_s = _attn_sample(heads(qs), heads(ks), heads(vs), feat_major(cache_k), feat_major(cache_v), bias_c, bias_n)
    attn_s = jnp.transpose(attn_s, (0, 2, 1, 3)).reshape(bs, ts, D_ATTN)
    x1_all, u2_all = _outproj(attn_s, convs, xs, ada_s, w_o_bf, norm_attn_g, ln1_g, ln1_b,
                              nb=bs, tm=ts, alpha=alpha, n_total=n_tok, row0=n_p, prev=(x1_all, u2_all))

    route_i, route_r, route_g, counts = _router(u2_all, router_w, router_b, tm=TOK_TILE)
    counts = counts[0, :N_EXPERTS].astype(I32)
    padded = ((counts + EXPERT_BLK - 1) // EXPERT_BLK) * EXPERT_BLK
    pad_ends = jnp.cumsum(padded)
    pad_starts = (pad_ends - padded).astype(I32)
    n_blocks = n_tok * TOP_K // EXPERT_BLK + N_EXPERTS
    n_rows = n_blocks * EXPERT_BLK
    blk_row = jnp.arange(n_blocks, dtype=I32) * EXPERT_BLK
    blk_valid = (blk_row < pad_ends[-1]).astype(I32)
    blk_e = jnp.sum((blk_row[:, None] >= pad_ends[None, :]).astype(I32), axis=1)
    last_e = jnp.sum((pad_ends[-1] - 1 >= pad_ends).astype(I32))
    blk_e = jnp.where(blk_valid != 0, blk_e, last_e).astype(I32)
    ar = jnp.arange(N_EXPERTS, dtype=I32)
    has_rows = padded > 0
    slot_of_e = (jnp.cumsum(has_rows.astype(I32)) - 1) % 2
    next_of_e = jnp.min(jnp.where((ar[None, :] > ar[:, None]) & has_rows[None, :], ar[None, :], N_EXPERTS), axis=1)
    next_of_e = jnp.where(next_of_e < N_EXPERTS, next_of_e, -1)
    blk_first = (blk_valid * (blk_row == pad_starts[blk_e]).astype(I32)).astype(I32)
    blk_slot = slot_of_e[blk_e].astype(I32)
    blk_next = next_of_e[blk_e].astype(I32)
    dest = _dest_rows(route_i, route_r, pad_starts, tm=TOK_TILE)
    dest_sm = dest[:, :TOP_K].reshape(n_tok // TOK_TILE, 1, TOK_TILE * TOP_K)

    fill_start = jnp.concatenate([pad_starts + counts, pad_ends[-1:]]).astype(I32)
    fill_len = jnp.concatenate([padded - counts, n_rows - pad_ends[-1:]]).astype(I32)
    x_sorted = _dispatch(u2_all, dest_sm, fill_start, fill_len, n_rows, tm=TOK_TILE)
    y_sorted = _gmm(x_sorted, blk_e, blk_valid, blk_first, blk_slot, blk_next, w_up, b_up, w_down, b_down,
                    bm=EXPERT_BLK)
    yp = _combine(y_sorted, dest_sm, route_g, x1_all, ada_p, ln2_g, ln2_b,
                  bsz=bp, seq=sp, nb=1, tm=TOK_TILE, alpha=alpha, row0=0)
    ys_out = _combine(y_sorted, dest_sm, route_g, x1_all, ada_s, ln2_g, ln2_b,
                      bsz=bs, seq=ts, nb=bs, tm=ts, alpha=alpha, row0=n_p)
    return yp, ys_out, kpt, vpt, cstp, ks, vs, csts


def kernel(x_prompt, x_sample, cache_k, cache_v, state_conv, c_prompt, c_sample, w_ada, b_ada, w_in, conv_w,
           norm_attn_g, norm_conv_g, w_o, ln1_g, ln1_b, rel_bias, router_w, router_b, w_up, b_up, w_down, b_down,
           ln2_g, ln2_b):
    depth = w_ada.shape[0]
    alpha = (2 * depth) ** 0.25
    xp, xs = x_prompt, x_sample
    bp, sp, _ = xp.shape
    bs, ts, _ = xs.shape
    n_keep = min(BRANCHES[-1][0], sp)
    bias_tabs = _bias_tables(rel_bias, ts, cache_k.shape[2])
    outs = [[] for _ in range(6)]
    for l in range(depth):
        xp, xs, kp, vp, cstp, ks, vs, csts = _layer(
            xp, xs, cache_k[l], cache_v[l], state_conv[l], c_prompt, c_sample, w_ada[l], b_ada[l], w_in[l],
            conv_w[l], norm_attn_g[l], norm_conv_g[l], w_o[l], ln1_g[l], ln1_b[l], bias_tabs, router_w[l],
            router_b[l], w_up[l], b_up[l], w_down[l], b_down[l], ln2_g[l], ln2_b[l], alpha)
        kp = jnp.transpose(kp, (0, 2, 1)).reshape(bp, sp, N_HEADS, HEAD_DIM)[:, -n_keep:]
        vp = jnp.transpose(vp, (0, 2, 1)).reshape(bp, sp, N_HEADS, HEAD_DIM)[:, -n_keep:]
        for lst, val in zip(outs, (kp, vp, cstp, ks.reshape(bs, ts, N_HEADS, HEAD_DIM),
                                   vs.reshape(bs, ts, N_HEADS, HEAD_DIM), csts)):
            lst.append(val)
    return (xp, xs) + tuple(jnp.stack(o) for o in outs)
```

```python
import functools
import math

import numpy as np
import jax
import jax.numpy as jnp
from jax import lax
from jax.experimental import pallas as pl
from jax.experimental.pallas import tpu as pltpu

F32 = jnp.float32
BF16 = jnp.bfloat16
I32 = jnp.int32

HEAD_DIM = 64
N_HEADS = 12
D_ATTN = N_HEADS * HEAD_DIM
D_CONV = 256
CONV_WIDTH = 3
BRANCHES = ((128, 1), (512, 4), (2048, 16))
NUM_BUCKETS = 32
MAX_DISTANCE = 2048
N_EXPERTS = 32
TOP_K = 4
SWIGLU_LIMIT = 7.0
SWIGLU_ALPHA = 1.702
LN_EPS = 1e-5
RMS_EPS = 1e-6
NEG_INF = -1e30

V7X_LANES = 128
V7X_SUBLANES = 8
V7X_VMEM_BYTES = 64 * 1024 * 1024

Q_BLK = 128
PAIR = 2 * HEAD_DIM
N_PAIRS = N_HEADS // 2
TOK_TILE = 256
ROW_TILE = 512
EXPERT_BLK = 256
ATTN_UNROLL = 16


def _cparams(sem, vmem_mb):
    return pltpu.CompilerParams(dimension_semantics=sem, vmem_limit_bytes=vmem_mb * 1024 * 1024)


def _t5_bucket_np(dist):
    dist = np.asarray(dist, np.int64)
    max_exact = NUM_BUCKETS // 2
    df = np.maximum(dist, max_exact).astype(np.float32)
    large = max_exact + (np.log(df / np.float32(max_exact)) / np.float32(math.log(MAX_DISTANCE / max_exact))
                         * np.float32(NUM_BUCKETS - max_exact)).astype(np.int32)
    return np.where(dist < max_exact, dist, np.minimum(large, NUM_BUCKETS - 1)).astype(np.int32)


def _prompt_bucket_index():
    a = np.arange(Q_BLK)[:, None]
    c = np.arange(2 * Q_BLK)[None, :]
    step = Q_BLK + a - c
    valid = (step >= 0) & (step <= Q_BLK)
    out = []
    for _, d in BRANCHES:
        out.append(np.where(valid, _t5_bucket_np(np.clip(step, 0, Q_BLK) * d), -1))
    return np.stack(out).astype(np.int32)


def _sample_bucket_index(n_new, n_past):
    t = np.arange(n_new)[:, None]
    out_c, out_n = [], []
    for w, d in BRANCHES:
        dist_c = n_past + t - np.arange(n_past)[None, :]
        dist_n = t - np.arange(n_new)[None, :]
        for dist, out in ((dist_c, out_c), (dist_n, out_n)):
            valid = (dist >= 0) & (dist <= w) & (dist % d == 0)
            out.append(np.where(valid, _t5_bucket_np(np.clip(dist, 0, w)), -1))
    return np.stack(out_c).astype(np.int32), np.stack(out_n).astype(np.int32)


def _bias_body(rb_ref, idx_ref, out_ref):
    idx = idx_ref[0]
    for h in range(N_HEADS):
        acc = jnp.where(idx < 0, NEG_INF, 0.0).astype(F32)
        for b in range(NUM_BUCKETS):
            acc = acc + jnp.where(idx == b, rb_ref[b * N_HEADS + h], 0.0)
        out_ref[0, h] = acc


def _bias_expand(rel_bias, idx_np, name):
    nbr, r, c = idx_np.shape
    return pl.pallas_call(
        _bias_body,
        grid=(nbr,),
        in_specs=[pl.BlockSpec(memory_space=pltpu.SMEM),
                  pl.BlockSpec((1, r, c), lambda i: (i, 0, 0))],
        out_specs=pl.BlockSpec((1, N_HEADS, r, c), lambda i: (i, 0, 0, 0)),
        out_shape=jax.ShapeDtypeStruct((nbr, N_HEADS, r, c), F32),
        name=name,
    )(rel_bias.reshape(-1), jnp.asarray(idx_np))


def _bias_tables(rel_bias, n_new, n_past):
    bias_p = _bias_expand(rel_bias, _prompt_bucket_index(), "bias_prompt")
    bias_p = bias_p.reshape(len(BRANCHES), N_PAIRS, 2 * Q_BLK, 2 * Q_BLK)
    ic, inw = _sample_bucket_index(n_new, n_past)
    return bias_p, _bias_expand(rel_bias, ic, "bias_cache"), _bias_expand(rel_bias, inw, "bias_new")


def _ada_body(c_ref, w_ref, b_ref, o_ref):
    c = c_ref[...]
    s = c * (1.0 / (1.0 + jnp.exp(-c)))
    o_ref[...] = jnp.dot(s.astype(BF16), w_ref[...].astype(BF16), preferred_element_type=F32) + b_ref[...]


def _ada(c_all, w_ada, b_ada):
    n, dm = c_all.shape
    n_out = w_ada.shape[1]
    tn = dm
    return pl.pallas_call(
        _ada_body,
        grid=(n_out // tn,),
        in_specs=[pl.BlockSpec((n, dm), lambda j: (0, 0)),
                  pl.BlockSpec((dm, tn), lambda j: (0, j)),
                  pl.BlockSpec((1, tn), lambda j: (0, j))],
        out_specs=pl.BlockSpec((n, tn), lambda j: (0, j)),
        out_shape=jax.ShapeDtypeStruct((n, n_out), F32),
        compiler_params=_cparams(("parallel",), 32),
        name="ada",
    )(c_all, w_ada, b_ada.reshape(1, n_out))


_SHIFT1, _SCALE1, _GATE1, _SHIFT2, _SCALE2, _GATE2 = range(6)


def _inproj_body(*refs, nb, tm, carry):
    if carry:
        (x_ref, sh_ref, sc_ref, w_ref, cw_ref, ng_ref, past_ref, wt_ref,
         q_ref, k_ref, v_ref, conv_ref, cst_ref, kt_ref, vt_ref, zz_ref) = refs
    else:
        (x_ref, sh_ref, sc_ref, w_ref, cw_ref, ng_ref, past_ref,
         q_ref, k_ref, v_ref, conv_ref, cst_ref, zz_ref) = refs
    dm = x_ref.shape[-1]
    rows = nb * tm
    u = x_ref[...] * (1.0 + sc_ref[...]) + sh_ref[...]
    u = u.reshape(rows, dm).astype(BF16)

    def proj(lo, width):
        return jnp.dot(u, w_ref[:, lo:lo + width], preferred_element_type=F32)

    q_ref[...] = proj(0, D_ATTN).reshape(nb, tm, D_ATTN)
    k_ref[...] = proj(D_ATTN, D_ATTN).reshape(nb, tm, D_ATTN)
    v_ref[...] = proj(2 * D_ATTN, D_ATTN).reshape(nb, tm, D_ATTN)
    if carry:
        nt = (((1,), (1,)), ((), ()))
        kt_ref[0] = lax.dot_general(wt_ref[D_ATTN:2 * D_ATTN, :], u, nt, preferred_element_type=F32)
        vt_ref[0] = lax.dot_general(wt_ref[2 * D_ATTN:3 * D_ATTN, :], u, nt, preferred_element_type=F32)
    gb = proj(3 * D_ATTN, D_CONV)
    gc = proj(3 * D_ATTN + D_CONV, D_CONV)
    hh = proj(3 * D_ATTN + 2 * D_CONV, D_CONV)
    z = (gc * hh).reshape(nb, tm, D_CONV)

    if carry:
        s = pl.program_id(1)

        @pl.when(s == 0)
        def _():
            zz_ref[:, 0:V7X_SUBLANES, :] = jnp.zeros((nb, V7X_SUBLANES, D_CONV), F32)

        @pl.when(s > 0)
        def _():
            zz_ref[:, 0:V7X_SUBLANES, :] = zz_ref[:, tm:tm + V7X_SUBLANES, :]
    else:
        zz_ref[:, V7X_SUBLANES - 2:V7X_SUBLANES, :] = past_ref[...]
    zz_ref[:, V7X_SUBLANES:, :] = z

    cw = cw_ref[...]
    yc = (cw[0:1, :] * zz_ref[:, V7X_SUBLANES - 2:V7X_SUBLANES - 2 + tm, :]
          + cw[1:2, :] * zz_ref[:, V7X_SUBLANES - 1:V7X_SUBLANES - 1 + tm, :]
          + cw[2:3, :] * z)
    g = gb.reshape(nb, tm, D_CONV) * yc
    ms = jnp.mean(g * g, axis=-1, keepdims=True)
    conv_ref[...] = g * lax.rsqrt(ms + RMS_EPS) * ng_ref[...]
    cst_ref[...] = zz_ref[:, tm + V7X_SUBLANES - 2:tm + V7X_SUBLANES, :]


def _inproj(x, ada3, w_in_bf, conv_w, norm_conv_g, past, *, nb, tm):
    bsz, seq, dm = x.shape
    carry = past is None
    if carry:
        past = jnp.zeros((bsz, CONV_WIDTH - 1, D_CONV), F32)
    grid = (bsz // nb, seq // tm)
    d_in = w_in_bf.shape[1]
    body = functools.partial(_inproj_body, nb=nb, tm=tm, carry=carry)
    tile = lambda width: pl.BlockSpec((nb, tm, width), lambda b, s: (b, s, 0))
    in_specs = [tile(dm),
                pl.BlockSpec((nb, 1, dm), lambda b, s: (b, 0, _SHIFT1)),
                pl.BlockSpec((nb, 1, dm), lambda b, s: (b, 0, _SCALE1)),
                pl.BlockSpec((dm, d_in), lambda b, s: (0, 0)),
                pl.BlockSpec((CONV_WIDTH, D_CONV), lambda b, s: (0, 0)),
                pl.BlockSpec((1, D_CONV), lambda b, s: (0, 0)),
                pl.BlockSpec((nb, CONV_WIDTH - 1, D_CONV), lambda b, s: (b, 0, 0))]
    args = [x, ada3, ada3, w_in_bf, conv_w, norm_conv_g.reshape(1, D_CONV), past]
    out_specs = [tile(D_ATTN), tile(D_ATTN), tile(D_ATTN), tile(D_CONV),
                 pl.BlockSpec((nb, CONV_WIDTH - 1, D_CONV), lambda b, s: (b, 0, 0))]
    out_shape = ([jax.ShapeDtypeStruct((bsz, seq, D_ATTN), F32)] * 3
                 + [jax.ShapeDtypeStruct((bsz, seq, D_CONV), F32),
                    jax.ShapeDtypeStruct((bsz, CONV_WIDTH - 1, D_CONV), F32)])
    if carry:
        assert nb == 1
        in_specs.append(pl.BlockSpec((d_in, dm), lambda b, s: (0, 0)))
        args.append(w_in_bf.T)
        out_specs += [pl.BlockSpec((1, D_ATTN, tm), lambda b, s: (b, 0, s))] * 2
        out_shape += [jax.ShapeDtypeStruct((bsz, D_ATTN, seq), F32)] * 2
    return pl.pallas_call(
        body,
        grid=grid,
        in_specs=in_specs,
        out_specs=out_specs,
        out_shape=out_shape,
        scratch_shapes=[pltpu.VMEM((nb, tm + V7X_SUBLANES, D_CONV), F32)],
        compiler_params=_cparams(("parallel", "arbitrary"), 56),
        name="inproj_prompt" if carry else "inproj_sample",
    )(*args)


def _attn_prompt_body(q_ref, k_ref, v_ref, bias_ref, o_ref, ob_ref, lb_ref, *, seq):
    scale = HEAD_DIM ** -0.5
    lane = lax.broadcasted_iota(I32, (Q_BLK, PAIR), 1)
    even = lane < HEAD_DIM
    ones = jnp.ones((Q_BLK, PAIR), BF16)

    def rows(ref, start, d):
        if d == 1:
            return ref[pl.ds(start, Q_BLK), :]
        return ref[pl.ds(start, Q_BLK, stride=d), :]

    def store(ref, br, start, d, val):
        if d == 1:
            ref[br, pl.ds(start, Q_BLK), :] = val
        else:
            ref[br, pl.ds(start, Q_BLK, stride=d), :] = val

    for br, (_, d) in enumerate(BRANCHES):
        nblk = seq // (d * Q_BLK)

        def block(it, carry, br=br, d=d, nblk=nblk):
            bias = bias_ref[br, 0]
            r = it // nblk
            i = it - r * nblk
            cur0 = r + d * Q_BLK * i
            qf = rows(q_ref, cur0, d) * scale
            qs = jnp.concatenate([jnp.where(even, qf, 0.0), jnp.where(even, 0.0, qf)], axis=0).astype(BF16)
            kc = rows(k_ref, cur0, d).astype(BF16)
            vc = rows(v_ref, cur0, d).astype(BF16)
            if nblk == 1:
                s = lax.dot_general(qs, kc, (((1,), (1,)), ((), ())), preferred_element_type=F32)
                s = s + bias[:, Q_BLK:]
                vaug = jnp.concatenate([vc, ones], axis=1)
            else:
                prev0 = r + d * Q_BLK * jnp.maximum(i - 1, 0)
                kp = rows(k_ref, prev0, d).astype(BF16)
                vp = rows(v_ref, prev0, d).astype(BF16)
                kk = jnp.concatenate([kp, kc], axis=0)
                s = lax.dot_general(qs, kk, (((1,), (1,)), ((), ())), preferred_element_type=F32)
                col = lax.broadcasted_iota(I32, (1, 2 * Q_BLK), 1)
                first = jnp.where((col < Q_BLK) & (i == 0), NEG_INF, 0.0)
                s = s + bias + first
                vaug = jnp.concatenate([jnp.concatenate([vp, vc], axis=0),
                                        jnp.concatenate([ones, ones], axis=0)], axis=1)
            m = jnp.max(s, axis=-1, keepdims=True)
            p = jnp.exp(s - m).astype(BF16)
            out = jnp.dot(p, vaug, preferred_element_type=F32)
            num, den = out[:, :PAIR], out[:, PAIR:]
            o = num / den
            lse = m + jnp.log(den)
            store(ob_ref, br, cur0, d, jnp.where(even, o[:Q_BLK], o[Q_BLK:]))
            store(lb_ref, br, cur0, d, jnp.where(even, lse[:Q_BLK], lse[Q_BLK:]))
            return carry

        lax.fori_loop(0, d * nblk, block, 0, unroll=ATTN_UNROLL)

    mt = 2 * Q_BLK

    def merge(i, carry):
        sl = pl.ds(pl.multiple_of(i * mt, mt), mt)
        l0, l1, l2 = lb_ref[0, sl, :], lb_ref[1, sl, :], lb_ref[2, sl, :]
        mx = jnp.maximum(jnp.maximum(l0, l1), l2)
        w0, w1, w2 = jnp.exp(l0 - mx), jnp.exp(l1 - mx), jnp.exp(l2 - mx)
        acc = w0 * ob_ref[0, sl, :] + w1 * ob_ref[1, sl, :] + w2 * ob_ref[2, sl, :]
        o_ref[sl, :] = acc / (w0 + w1 + w2)
        return carry

    lax.fori_loop(0, seq // mt, merge, 0)


def _attn_prompt(q, k, v, bias_p):
    bsz, seq, _ = q.shape
    assert seq % (BRANCHES[-1][1] * Q_BLK) == 0
    body = functools.partial(_attn_prompt_body, seq=seq)
    col = pl.BlockSpec((None, seq, PAIR), lambda b, j: (b, 0, j))
    return pl.pallas_call(
        body,
        grid=(bsz, N_PAIRS),
        in_specs=[col, col, col,
                  pl.BlockSpec((len(BRANCHES), 1, 2 * Q_BLK, 2 * Q_BLK), lambda b, j: (0, j, 0, 0))],
        out_specs=col,
        out_shape=jax.ShapeDtypeStruct((bsz, seq, D_ATTN), F32),
        scratch_shapes=[pltpu.VMEM((len(BRANCHES), seq, PAIR), F32),
                        pltpu.VMEM((len(BRANCHES), seq, PAIR), F32)],
        compiler_params=_cparams(("parallel", "parallel"), 48),
        name="attn_prompt",
    )(q, k, v, bias_p)


def _attn_sample_body(q_ref, kn_ref, vn_ref, ckt_ref, cvt_ref, bc_ref, bn_ref, o_ref):
    scale = HEAD_DIM ** -0.5
    nt = (((1,), (1,)), ((), ()))
    nbr = len(BRANCHES)
    for h in range(N_HEADS):
        rows = slice(h * HEAD_DIM, (h + 1) * HEAD_DIM)
        qh = (q_ref[h] * scale).astype(BF16)
        s_c = jnp.dot(qh, ckt_ref[rows, :].astype(BF16), preferred_element_type=F32)
        s_n = lax.dot_general(qh, kn_ref[h].astype(BF16), nt, preferred_element_type=F32)
        ps, lses, dens = [], [], []
        for br in range(nbr):
            lc = s_c + bc_ref[br, h]
            ln = s_n + bn_ref[br, h]
            m = jnp.maximum(jnp.max(lc, axis=-1, keepdims=True), jnp.max(ln, axis=-1, keepdims=True))
            pc = jnp.exp(lc - m)
            pn = jnp.exp(ln - m)
            den = jnp.sum(pc, axis=-1, keepdims=True) + jnp.sum(pn, axis=-1, keepdims=True)
            ps.append((pc, pn))
            dens.append(den)
            lses.append(m + jnp.log(den))
        mx = jnp.maximum(jnp.maximum(lses[0], lses[1]), lses[2])
        ws = [jnp.exp(l - mx) for l in lses]
        wsum = ws[0] + ws[1] + ws[2]
        coefs = [ws[br] / (wsum * dens[br]) for br in range(nbr)]
        p_c = coefs[0] * ps[0][0] + coefs[1] * ps[1][0] + coefs[2] * ps[2][0]
        p_n = coefs[0] * ps[0][1] + coefs[1] * ps[1][1] + coefs[2] * ps[2][1]
        o = lax.dot_general(p_c.astype(BF16), cvt_ref[rows, :].astype(BF16), nt, preferred_element_type=F32)
        o_ref[h] = o + jnp.dot(p_n.astype(BF16), vn_ref[h].astype(BF16), preferred_element_type=F32)


def _attn_sample(q, k_new, v_new, cache_kt, cache_vt, bias_c, bias_n):
    bsz, _, n_new, _ = q.shape
    n_past = cache_kt.shape[2]
    new = pl.BlockSpec((None, N_HEADS, n_new, HEAD_DIM), lambda b: (b, 0, 0, 0))
    cache = pl.BlockSpec((None, D_ATTN, n_past), lambda b: (b, 0, 0))
    return pl.pallas_call(
        _attn_sample_body,
        grid=(bsz,),
        in_specs=[new, new, new, cache, cache,
                  pl.BlockSpec(bias_c.shape, lambda b: (0, 0, 0, 0)), pl.BlockSpec(bias_n.shape, lambda b: (0, 0, 0, 0))],
        out_specs=new,
        out_shape=jax.ShapeDtypeStruct((bsz, N_HEADS, n_new, HEAD_DIM), F32),
        compiler_params=_cparams(("parallel",), 48),
        name="attn_sample",
    )(q, k_new, v_new, cache_kt, cache_vt, bias_c, bias_n)


def _layer_norm(y, g, b):
    mu = jnp.mean(y, axis=-1, keepdims=True)
    c = y - mu
    var = jnp.mean(c * c, axis=-1, keepdims=True)
    return c * lax.rsqrt(var + LN_EPS) * g + b


def _to_row_tiles(ref, val):
    rows, dm = val.shape
    assert dm == V7X_SUBLANES * V7X_LANES
    for c in range(V7X_SUBLANES):
        ref[pl.ds(c, rows, stride=V7X_SUBLANES), :] = val[:, c * V7X_LANES:(c + 1) * V7X_LANES]


def _from_row_tiles(ref, rows, lead=()):
    chunks = [ref[lead + (pl.ds(c, rows, stride=V7X_SUBLANES), slice(None))] for c in range(V7X_SUBLANES)]
    return jnp.concatenate(chunks, axis=1)


def _outproj_body(*refs, nb, tm, alpha, aliased, n_main):
    if aliased:
        refs = refs[2:]
    (a_ref, c_ref, x_ref, g1_ref, sh2_ref, sc2_ref, wa_ref, wc_ref, ng_ref, lg_ref, lb_ref, x1_ref, u2_ref) = refs
    rows = nb * tm

    @pl.when(pl.program_id(0) < n_main)
    def _():
        a = a_ref[...]
        ms = jnp.mean(a * a, axis=-1, keepdims=True)
        an = (a * lax.rsqrt(ms + RMS_EPS) * ng_ref[...]).reshape(rows, D_ATTN).astype(BF16)
        cn = c_ref[...].reshape(rows, D_CONV).astype(BF16)
        mix = (jnp.dot(an, wa_ref[...], preferred_element_type=F32)
               + jnp.dot(cn, wc_ref[...], preferred_element_type=F32))
        dm = mix.shape[-1]
        y = alpha * x_ref[...] + (1.0 + g1_ref[...]) * mix.reshape(nb, tm, dm)
        x1 = _layer_norm(y, lg_ref[...], lb_ref[...])
        x1_ref[...] = x1.reshape(rows, dm)
        u2_ref[...] = (x1 * (1.0 + sc2_ref[...]) + sh2_ref[...]).reshape(rows, dm)

    @pl.when(pl.program_id(0) >= n_main)
    def _():
        x1_ref[...] = jnp.zeros_like(x1_ref)
        u2_ref[...] = jnp.zeros_like(u2_ref)


def _outproj(attn, conv, x, ada3, w_o_bf, norm_attn_g, ln1_g, ln1_b, *, nb, tm, alpha, n_total, row0, prev=None):
    bsz, seq, dm = x.shape
    rows = nb * tm
    assert row0 % rows == 0
    st = seq // tm
    blk0 = row0 // rows
    n_main = (bsz // nb) * st
    aliased = prev is not None
    n_steps = n_main if aliased else pl.cdiv(n_total, rows)

    def bs_of(i):
        j = jnp.minimum(i, n_main - 1)
        return j // st, j % st

    tile = lambda width: pl.BlockSpec((nb, tm, width), lambda i: bs_of(i) + (0,))
    ada = lambda chunk: pl.BlockSpec((nb, 1, dm), lambda i: (bs_of(i)[0], 0, chunk))
    const = lambda shape: pl.BlockSpec(shape, lambda i: (0,) * len(shape))
    out = pl.BlockSpec((rows, dm), lambda i: (blk0 + i, 0))
    body = functools.partial(_outproj_body, nb=nb, tm=tm, alpha=alpha, aliased=aliased, n_main=n_main)
    in_specs = [tile(D_ATTN), tile(D_CONV), tile(dm), ada(_GATE1), ada(_SHIFT2), ada(_SCALE2),
                const((D_ATTN, dm)), const((D_CONV, dm)), const((1, D_ATTN)), const((1, dm)), const((1, dm))]
    args = [attn, conv, x, ada3, ada3, ada3, w_o_bf[:D_ATTN], w_o_bf[D_ATTN:],
            norm_attn_g.reshape(1, D_ATTN), ln1_g.reshape(1, dm), ln1_b.reshape(1, dm)]
    kwargs = {}
    if aliased:
        in_specs = [pl.BlockSpec(memory_space=pl.ANY)] * 2 + in_specs
        args = list(prev) + args
        kwargs["input_output_aliases"] = {0: 0, 1: 1}
    return pl.pallas_call(
        body,
        grid=(n_steps,),
        in_specs=in_specs,
        out_specs=[out, out],
        out_shape=[jax.ShapeDtypeStruct((n_total, dm), F32)] * 2,
        compiler_params=_cparams(("parallel",), 48),
        name="outproj_sample" if aliased else "outproj_prompt",
        **kwargs,
    )(*args)


def _router_body(u_ref, w_ref, b_ref, ri_ref, rr_ref, rg_ref, cnt_ref, run_ref, *, tm):
    i = pl.program_id(0)

    @pl.when(i == 0)
    def _():
        run_ref[...] = jnp.zeros_like(run_ref)

    logits = jnp.dot(u_ref[...], w_ref[...], preferred_element_type=F32,
                     precision=lax.Precision.HIGHEST) + b_ref[...]
    lane_i = lax.broadcasted_iota(I32, (tm, V7X_LANES), 1)
    lane = lane_i.astype(F32)
    vals = logits
    tops, idxs = [], []
    for _ in range(TOP_K):
        mk = jnp.max(vals, axis=-1, keepdims=True)
        ik = jnp.min(jnp.where(vals == mk, lane, float(V7X_LANES)), axis=-1, keepdims=True)
        tops.append(mk)
        idxs.append(ik)
        vals = jnp.where(lane == ik, -jnp.inf, vals)
    es = [jnp.exp(t - tops[0]) for t in tops]
    den = es[0] + es[1] + es[2] + es[3]
    sel = jnp.zeros((tm, V7X_LANES), F32)
    for ik in idxs:
        sel = sel + jnp.where(lane == ik, 1.0, 0.0)
    ri = lax.broadcasted_iota(I32, (tm, tm), 0)
    ci = lax.broadcasted_iota(I32, (tm, tm), 1)
    lower = jnp.where(ci < ri, 1.0, 0.0).astype(BF16)
    before = jnp.dot(lower, sel.astype(BF16), preferred_element_type=F32) + run_ref[...]
    out_i = jnp.full((tm, V7X_LANES), -1, I32)
    out_r = jnp.zeros((tm, V7X_LANES), I32)
    out_g = jnp.zeros((tm, V7X_LANES), F32)
    for kk in range(TOP_K):
        rank = jnp.sum(jnp.where(lane == idxs[kk], before, 0.0), axis=-1, keepdims=True)
        out_i = jnp.where(lane_i == kk, idxs[kk].astype(I32), out_i)
        out_r = jnp.where(lane_i == kk, rank.astype(I32), out_r)
        out_g = jnp.where(lane_i == kk, es[kk] / den, out_g)
    ri_ref[...] = out_i
    rr_ref[...] = out_r
    rg_ref[...] = out_g
    run_ref[...] = run_ref[...] + jnp.sum(sel, axis=0, keepdims=True)
    cnt_ref[...] = run_ref[...]


def _router(u2, router_w, router_b, *, tm):
    n, dm = u2.shape
    w_pad = jnp.pad(router_w, ((0, 0), (0, V7X_LANES - N_EXPERTS)))
    b_pad = jnp.pad(router_b, (0, V7X_LANES - N_EXPERTS), constant_values=NEG_INF).reshape(1, V7X_LANES)
    tok = pl.BlockSpec((tm, V7X_LANES), lambda i: (i, 0))
    return pl.pallas_call(
        functools.partial(_router_body, tm=tm),
        grid=(n // tm,),
        in_specs=[pl.BlockSpec((tm, dm), lambda i: (i, 0)),
                  pl.BlockSpec((dm, V7X_LANES), lambda i: (0, 0)),
                  pl.BlockSpec((1, V7X_LANES), lambda i: (0, 0))],
        out_specs=[tok, tok, tok, pl.BlockSpec((1, V7X_LANES), lambda i: (0, 0))],
        out_shape=[jax.ShapeDtypeStruct((n, V7X_LANES), I32), jax.ShapeDtypeStruct((n, V7X_LANES), I32),
                   jax.ShapeDtypeStruct((n, V7X_LANES), F32), jax.ShapeDtypeStruct((1, V7X_LANES), F32)],
        scratch_shapes=[pltpu.VMEM((1, V7X_LANES), F32)],
        compiler_params=_cparams(("arbitrary",), 32),
        name="router",
    )(u2, w_pad, b_pad)


def _dest_body(ps_ref, ri_ref, rr_ref, d_ref):
    idx = ri_ref[...]
    acc = rr_ref[...]
    for e in range(N_EXPERTS):
        acc = acc + jnp.where(idx == e, ps_ref[e], 0)
    d_ref[...] = acc


def _dest_rows(expert, rank, pad_starts):
    return pl.pallas_call(
        _dest_body,
        in_specs=[pl.BlockSpec(memory_space=pltpu.SMEM), pl.BlockSpec(memory_space=pltpu.VMEM),
                  pl.BlockSpec(memory_space=pltpu.VMEM)],
        out_specs=pl.BlockSpec(memory_space=pltpu.VMEM),
        out_shape=jax.ShapeDtypeStruct(expert.shape, I32),
        name="dest_rows",
    )(pad_starts, expert, rank)


ISSUE_TOKENS = 2


def _row_tile(ref, row):
    return ref.at[pl.ds(pl.multiple_of(row * V7X_SUBLANES, V7X_SUBLANES), V7X_SUBLANES), :]


def _issue_rows(dest_ref, n_tok, start_copy):
    def trip(it, carry):
        base = it * ISSUE_TOKENS
        rows = [dest_ref[0, 0, (base + u) * TOP_K + kk] for u in range(ISSUE_TOKENS) for kk in range(TOP_K)]
        for u in range(ISSUE_TOKENS):
            for kk in range(TOP_K):
                start_copy(base + u, kk, rows[u * TOP_K + kk])
        return carry

    lax.fori_loop(0, n_tok // ISSUE_TOKENS, trip, 0)


DISPATCH_SLOTS = 3


FILL_UNROLL = 8


def _dispatch_body(fr_ref, dest_ref, u_ref, xs_out, stage_ref, zero_ref, sems, fill_sem, *, tm, n_fill):
    i = pl.program_id(0)
    last = pl.num_programs(0) - 1
    slot = i % DISPATCH_SLOTS

    def wait_step(s):
        for _ in range(TOP_K):
            pltpu.make_async_copy(stage_ref.at[s], xs_out.at[pl.ds(0, tm * V7X_SUBLANES), :], sems.at[s]).wait()

    @pl.when(i == 0)
    def _():
        zero_ref[...] = jnp.zeros_like(zero_ref)

        def trip(it, carry):
            rows = [fr_ref[it * FILL_UNROLL + u] for u in range(FILL_UNROLL)]
            for row in rows:
                pltpu.make_async_copy(zero_ref, _row_tile(xs_out, row), fill_sem).start()
            return carry

        lax.fori_loop(0, n_fill // FILL_UNROLL, trip, 0)

    @pl.when(i >= DISPATCH_SLOTS - 1)
    def _():
        wait_step((i + 1) % DISPATCH_SLOTS)

    _to_row_tiles(stage_ref.at[slot], u_ref[...])

    def start_copy(r, kk, row):
        pltpu.make_async_copy(_row_tile(stage_ref.at[slot], r), _row_tile(xs_out, row), sems.at[slot]).start()

    _issue_rows(dest_ref, tm, start_copy)

    @pl.when(i == last)
    def _():
        for back in range(DISPATCH_SLOTS - 1):
            @pl.when(i >= back)
            def _():
                wait_step((i - back) % DISPATCH_SLOTS)
        pltpu.make_async_copy(xs_out.at[pl.ds(0, n_fill * V7X_SUBLANES), :],
                              xs_out.at[pl.ds(0, n_fill * V7X_SUBLANES), :], fill_sem).wait()


def _dispatch(u2, dest_sm, fill_rows, n_rows, *, tm):
    n, dm = u2.shape
    n_fill = fill_rows.shape[0]
    assert n_fill == n_rows - n * TOP_K and n_fill % FILL_UNROLL == 0
    grid_spec = pltpu.PrefetchScalarGridSpec(
        num_scalar_prefetch=1,
        grid=(n // tm,),
        in_specs=[pl.BlockSpec((1, 1, tm * TOP_K), lambda i, fr: (i, 0, 0), memory_space=pltpu.SMEM),
                  pl.BlockSpec((tm, dm), lambda i, fr: (i, 0))],
        out_specs=pl.BlockSpec(memory_space=pl.ANY),
        scratch_shapes=[pltpu.VMEM((DISPATCH_SLOTS, tm * V7X_SUBLANES, V7X_LANES), F32),
                        pltpu.VMEM((V7X_SUBLANES, V7X_LANES), F32),
                        pltpu.SemaphoreType.DMA((DISPATCH_SLOTS,)), pltpu.SemaphoreType.DMA(())],
    )
    return pl.pallas_call(
        functools.partial(_dispatch_body, tm=tm, n_fill=n_fill),
        grid_spec=grid_spec,
        out_shape=jax.ShapeDtypeStruct((n_rows * V7X_SUBLANES, V7X_LANES), F32),
        compiler_params=_cparams(("arbitrary",), 32),
        name="dispatch",
    )(fill_rows, dest_sm, u2)


def _gmm_body(be_ref, bv_ref, bf_ref, bs_ref, bn_ref, xs_ref, wu_hbm, bu_ref, wd_hbm, bd_ref, ys_ref,
              wu_f32, wd_f32, wu_bf, wd_bf, sems, *, bm):
    j = pl.program_id(0)
    e = be_ref[j]
    s = bs_ref[j]
    d_ff = wd_hbm.shape[1]
    chunk = 64

    def fetch(ex, slot):
        return (pltpu.make_async_copy(wu_hbm.at[ex], wu_f32.at[slot], sems.at[0, slot]),
                pltpu.make_async_copy(wd_hbm.at[ex], wd_f32.at[slot], sems.at[1, slot]))

    @pl.when(bf_ref[j] != 0)
    def _():
        @pl.when(j == 0)
        def _():
            for c in fetch(e, s):
                c.start()

        for c in fetch(e, s):
            c.wait()

        @pl.when(bn_ref[j] >= 0)
        def _():
            for c in fetch(bn_ref[j], 1 - s):
                c.start()

        def cast(c, carry):
            sl = pl.ds(pl.multiple_of(c * chunk, chunk), chunk)
            wu_bf[sl, :] = wu_f32[s, sl, :].astype(BF16)
            wd_bf[sl, :] = wd_f32[s, sl, :].astype(BF16)
            return carry

        lax.fori_loop(0, wu_hbm.shape[1] // chunk, cast, 0)

    @pl.when(bv_ref[j] != 0)
    def _():
        x = _from_row_tiles(xs_ref, bm).astype(BF16)
        glu = jnp.dot(x, wu_bf[:, :d_ff], preferred_element_type=F32) + bu_ref[0, :, :d_ff]
        lin = jnp.dot(x, wu_bf[:, d_ff:], preferred_element_type=F32) + bu_ref[0, :, d_ff:]
        glu = jnp.minimum(glu, SWIGLU_LIMIT)
        lin = jnp.clip(lin, -SWIGLU_LIMIT, SWIGLU_LIMIT)
        act = glu * (1.0 / (1.0 + jnp.exp(-SWIGLU_ALPHA * glu))) * (lin + 1.0)
        _to_row_tiles(ys_ref, jnp.dot(act.astype(BF16), wd_bf[...], preferred_element_type=F32) + bd_ref[0])

    @pl.when(bv_ref[j] == 0)
    def _():
        ys_ref[...] = jnp.zeros_like(ys_ref)


def _gmm(xs, blk_e, blk_valid, blk_first, blk_slot, blk_next, w_up, b_up, w_down, b_down, *, bm):
    n_rows = xs.shape[0] // V7X_SUBLANES
    n_e, dm, d_up = w_up.shape
    d_ff = w_down.shape[1]
    assert d_ff == dm
    row_tiles = pl.BlockSpec((bm * V7X_SUBLANES, V7X_LANES), lambda j, *_: (j, 0))
    grid_spec = pltpu.PrefetchScalarGridSpec(
        num_scalar_prefetch=5,
        grid=(n_rows // bm,),
        in_specs=[row_tiles,
                  pl.BlockSpec(memory_space=pl.ANY),
                  pl.BlockSpec((1, 1, d_up), lambda j, be, *_: (be[j], 0, 0)),
                  pl.BlockSpec(memory_space=pl.ANY),
                  pl.BlockSpec((1, 1, dm), lambda j, be, *_: (be[j], 0, 0))],
        out_specs=row_tiles,
        scratch_shapes=[pltpu.VMEM((2, dm, d_up), F32), pltpu.VMEM((2, d_ff, dm), F32),
                        pltpu.VMEM((dm, d_up), BF16), pltpu.VMEM((d_ff, dm), BF16),
                        pltpu.SemaphoreType.DMA((2, 2))],
    )
    return pl.pallas_call(
        functools.partial(_gmm_body, bm=bm),
        grid_spec=grid_spec,
        out_shape=jax.ShapeDtypeStruct(xs.shape, F32),
        compiler_params=_cparams(("arbitrary",), 56),
        name="expert_mlp",
    )(blk_e, blk_valid, blk_first, blk_slot, blk_next, xs, w_up, b_up.reshape(n_e, 1, d_up), w_down,
      b_down.reshape(n_e, 1, dm))


def _combine_body(rt_ref, rtn_ref, ys_hbm, rg_ref, x1_ref, g2_ref, lg_ref, lb_ref, y_ref, rows_ref, sems,
                  *, nb, tm, alpha, n_steps):
    n_tok = nb * tm
    i = pl.program_id(0)
    slot = i % 2

    def issue(dest_ref, s):
        def start_copy(r, kk, row):
            pltpu.make_async_copy(_row_tile(ys_hbm, row), _row_tile(rows_ref.at[s, kk], r), sems.at[s]).start()

        _issue_rows(dest_ref, n_tok, start_copy)

    @pl.when(i == 0)
    def _():
        issue(rt_ref, 0)

    @pl.when(i + 1 < n_steps)
    def _():
        issue(rtn_ref, 1 - slot)

    for kk in range(TOP_K):
        pltpu.make_async_copy(ys_hbm.at[pl.ds(0, n_tok * V7X_SUBLANES), :], rows_ref.at[slot, kk],
                              sems.at[slot]).wait()

    gates = rg_ref[...]
    ffn = gates[:, 0:1] * _from_row_tiles(rows_ref, n_tok, (slot, 0))
    for kk in range(1, TOP_K):
        ffn = ffn + gates[:, kk:kk + 1] * _from_row_tiles(rows_ref, n_tok, (slot, kk))
    dm = ffn.shape[-1]
    y = alpha * x1_ref[...].reshape(nb, tm, dm) + (1.0 + g2_ref[...]) * ffn.reshape(nb, tm, dm)
    y_ref[...] = _layer_norm(y, lg_ref[...], lb_ref[...])


def _combine(ys, dest_sm, route_g, x1_all, ada3, ln2_g, ln2_b, *, bsz, seq, nb, tm, alpha, row0):
    dm = x1_all.shape[1]
    n_tok = nb * tm
    assert row0 % n_tok == 0 and dest_sm.shape[2] == n_tok * TOP_K and n_tok % ISSUE_TOKENS == 0
    blk0 = row0 // n_tok
    st = seq // tm
    n_steps = (bsz // nb) * st
    dest = lambda off: pl.BlockSpec((1, 1, n_tok * TOP_K),
                                    lambda i: (blk0 + jnp.minimum(i + off, n_steps - 1), 0, 0),
                                    memory_space=pltpu.SMEM)
    return pl.pallas_call(
        functools.partial(_combine_body, nb=nb, tm=tm, alpha=alpha, n_steps=n_steps),
        grid=(n_steps,),
        in_specs=[dest(0), dest(1),
                  pl.BlockSpec(memory_space=pl.ANY),
                  pl.BlockSpec((n_tok, V7X_LANES), lambda i: (blk0 + i, 0)),
                  pl.BlockSpec((n_tok, dm), lambda i: (blk0 + i, 0)),
                  pl.BlockSpec((nb, 1, dm), lambda i: (i // st, 0, _GATE2)),
                  pl.BlockSpec((1, dm), lambda i: (0, 0)),
                  pl.BlockSpec((1, dm), lambda i: (0, 0))],
        out_specs=pl.BlockSpec((nb, tm, dm), lambda i: (i // st, i % st, 0)),
        scratch_shapes=[pltpu.VMEM((2, TOP_K, n_tok * V7X_SUBLANES, V7X_LANES), F32),
                        pltpu.SemaphoreType.DMA((2,))],
        out_shape=jax.ShapeDtypeStruct((bsz, seq, dm), F32),
        compiler_params=_cparams(("arbitrary",), 48),
        name="combine_prompt" if row0 == 0 else "combine_sample",
    )(dest_sm, dest_sm, ys, route_g, x1_all, ada3, ln2_g.reshape(1, dm), ln2_b.reshape(1, dm))


def _layer(xp, xs, cache_k, cache_v, conv_past, cp, cs, w_ada, b_ada, w_in, conv_w, norm_attn_g, norm_conv_g,
           w_o, ln1_g, ln1_b, bias_tabs, router_w, router_b, w_up, b_up, w_down, b_down, ln2_g, ln2_b, alpha):
    bp, sp, dm = xp.shape
    bs, ts, _ = xs.shape
    n_p, n_s = bp * sp, bs * ts
    n_tok = n_p + n_s
    assert n_p % TOK_TILE == 0 and n_s == TOK_TILE and sp % ROW_TILE == 0
    bias_p, bias_c, bias_n = bias_tabs

    ada = _ada(jnp.concatenate([cp, cs], axis=0), w_ada, b_ada)
    ada_p = ada[:bp].reshape(bp, 1, -1)
    ada_s = ada[bp:].reshape(bs, 1, -1)
    w_in_bf = w_in.astype(BF16)
    w_o_bf = w_o.astype(BF16)

    qp, kp, vp, convp, cstp, kpt, vpt = _inproj(xp, ada_p, w_in_bf, conv_w, norm_conv_g, None, nb=1, tm=ROW_TILE)
    attn_p = _attn_prompt(qp, kp, vp, bias_p)
    x1_all, u2_all = _outproj(attn_p, convp, xp, ada_p, w_o_bf, norm_attn_g, ln1_g, ln1_b,
                              nb=1, tm=ROW_TILE, alpha=alpha, n_total=n_tok, row0=0)
    qs, ks, vs, convs, csts = _inproj(xs, ada_s, w_in_bf, conv_w, norm_conv_g, conv_past, nb=bs, tm=ts)
    heads = lambda a: jnp.transpose(a.reshape(bs, ts, N_HEADS, HEAD_DIM), (0, 2, 1, 3))
    n_past = cache_k.shape[1]
    feat_major = lambda c: jnp.transpose(c.reshape(bs, n_past, D_ATTN), (0, 2, 1))
    attn_s = _attn_sample(heads(qs), heads(ks), heads(vs), feat_major(cache_k), feat_major(cache_v), bias_c, bias_n)
    attn_s = jnp.transpose(attn_s, (0, 2, 1, 3)).reshape(bs, ts, D_ATTN)
    x1_all, u2_all = _outproj(attn_s, convs, xs, ada_s, w_o_bf, norm_attn_g, ln1_g, ln1_b,
                              nb=bs, tm=ts, alpha=alpha, n_total=n_tok, row0=n_p, prev=(x1_all, u2_all))

    route_i, route_r, route_g, counts = _router(u2_all, router_w, router_b, tm=TOK_TILE)
    counts = counts[0, :N_EXPERTS].astype(I32)
    padded = ((counts + EXPERT_BLK - 1) // EXPERT_BLK) * EXPERT_BLK
    pad_ends = jnp.cumsum(padded)
    pad_starts = (pad_ends - padded).astype(I32)
    n_blocks = n_tok * TOP_K // EXPERT_BLK + N_EXPERTS
    n_rows = n_blocks * EXPERT_BLK
    blk_row = jnp.arange(n_blocks, dtype=I32) * EXPERT_BLK
    blk_valid = (blk_row < pad_ends[-1]).astype(I32)
    blk_e = jnp.sum((blk_row[:, None] >= pad_ends[None, :]).astype(I32), axis=1)
    last_e = jnp.sum((pad_ends[-1] - 1 >= pad_ends).astype(I32))
    blk_e = jnp.where(blk_valid != 0, blk_e, last_e).astype(I32)
    ar = jnp.arange(N_EXPERTS, dtype=I32)
    has_rows = padded > 0
    slot_of_e = (jnp.cumsum(has_rows.astype(I32)) - 1) % 2
    next_of_e = jnp.min(jnp.where((ar[None, :] > ar[:, None]) & has_rows[None, :], ar[None, :], N_EXPERTS), axis=1)
    next_of_e = jnp.where(next_of_e < N_EXPERTS, next_of_e, -1)
    is_e = blk_e[:, None] == ar[None, :]
    pick = lambda per_expert: jnp.sum(jnp.where(is_e, per_expert[None, :], 0), axis=1).astype(I32)
    blk_first = (blk_valid * (blk_row == pick(pad_starts)).astype(I32)).astype(I32)
    blk_slot = pick(slot_of_e)
    blk_next = pick(next_of_e)
    dense = lambda a: a[:, :TOP_K].reshape(n_tok * TOP_K // V7X_LANES, V7X_LANES)
    dest = _dest_rows(dense(route_i), dense(route_r), pad_starts)
    dest_sm = dest.reshape(n_tok // TOK_TILE, 1, TOK_TILE * TOP_K)

    run_start = jnp.concatenate([pad_starts + counts, pad_ends[-1:]])
    run_len = jnp.concatenate([padded - counts, n_rows - pad_ends[-1:]])
    run_k0 = jnp.cumsum(run_len) - run_len
    k = jnp.arange(n_rows - n_tok * TOP_K, dtype=I32)[:, None]
    in_run = (k >= run_k0[None, :]) & (k < (run_k0 + run_len)[None, :])
    fill_rows = jnp.sum(jnp.where(in_run, run_start[None, :] + k - run_k0[None, :], 0), axis=1).astype(I32)
    x_sorted = _dispatch(u2_all, dest_sm, fill_rows, n_rows, tm=TOK_TILE)
    y_sorted = _gmm(x_sorted, blk_e, blk_valid, blk_first, blk_slot, blk_next, w_up, b_up, w_down, b_down,
                    bm=EXPERT_BLK)
    yp = _combine(y_sorted, dest_sm, route_g, x1_all, ada_p, ln2_g, ln2_b,
                  bsz=bp, seq=sp, nb=1, tm=TOK_TILE, alpha=alpha, row0=0)
    ys_out = _combine(y_sorted, dest_sm, route_g, x1_all, ada_s, ln2_g, ln2_b,
                      bsz=bs, seq=ts, nb=bs, tm=ts, alpha=alpha, row0=n_p)
    return yp, ys_out, kpt, vpt, cstp, ks, vs, csts


def kernel(x_prompt, x_sample, cache_k, cache_v, state_conv, c_prompt, c_sample, w_ada, b_ada, w_in, conv_w,
           norm_attn_g, norm_conv_g, w_o, ln1_g, ln1_b, rel_bias, router_w, router_b, w_up, b_up, w_down, b_down,
           ln2_g, ln2_b):
    depth = w_ada.shape[0]
    alpha = (2 * depth) ** 0.25
    xp, xs = x_prompt, x_sample
    bp, sp, _ = xp.shape
    bs, ts, _ = xs.shape
    n_keep = min(BRANCHES[-1][0], sp)
    bias_tabs = _bias_tables(rel_bias, ts, cache_k.shape[2])
    outs = [[] for _ in range(6)]
    for l in range(depth):
        xp, xs, kp, vp, cstp, ks, vs, csts = _layer(
            xp, xs, cache_k[l], cache_v[l], state_conv[l], c_prompt, c_sample, w_ada[l], b_ada[l], w_in[l],
            conv_w[l], norm_attn_g[l], norm_conv_g[l], w_o[l], ln1_g[l], ln1_b[l], bias_tabs, router_w[l],
            router_b[l], w_up[l], b_up[l], w_down[l], b_down[l], ln2_g[l], ln2_b[l], alpha)
        kp = jnp.transpose(kp, (0, 2, 1)).reshape(bp, sp, N_HEADS, HEAD_DIM)[:, -n_keep:]
        vp = jnp.transpose(vp, (0, 2, 1)).reshape(bp, sp, N_HEADS, HEAD_DIM)[:, -n_keep:]
        for lst, val in zip(outs, (kp, vp, cstp, ks.reshape(bs, ts, N_HEADS, HEAD_DIM),
                                   vs.reshape(bs, ts, N_HEADS, HEAD_DIM), csts)):
            lst.append(val)
    return (xp, xs) + tuple(jnp.stack(o) for o in outs)
```

```python
import functools
import math

import numpy as np
import jax
import jax.numpy as jnp
from jax import lax
from jax.experimental import pallas as pl
from jax.experimental.pallas import tpu as pltpu

F32 = jnp.float32
BF16 = jnp.bfloat16
I32 = jnp.int32

HEAD_DIM = 64
N_HEADS = 12
D_ATTN = N_HEADS * HEAD_DIM
D_CONV = 256
CONV_WIDTH = 3
BRANCHES = ((128, 1), (512, 4), (2048, 16))
NUM_BUCKETS = 32
MAX_DISTANCE = 2048
N_EXPERTS = 32
TOP_K = 4
SWIGLU_LIMIT = 7.0
SWIGLU_ALPHA = 1.702
LN_EPS = 1e-5
RMS_EPS = 1e-6
NEG_INF = -1e30

V7X_LANES = 128
V7X_SUBLANES = 8
V7X_VMEM_BYTES = 64 * 1024 * 1024
DMA_QUEUES = 2

Q_BLK = 128
PAIR = 2 * HEAD_DIM
N_PAIRS = N_HEADS // 2
TOK_TILE = 256
ROW_TILE = 512
EXPERT_BLK = 256
EXPERT_ROWS = 256
ATTN_UNROLL = 16


def _cparams(sem, vmem_mb):
    return pltpu.CompilerParams(dimension_semantics=sem, vmem_limit_bytes=vmem_mb * 1024 * 1024)


def _t5_bucket_np(dist):
    dist = np.asarray(dist, np.int64)
    max_exact = NUM_BUCKETS // 2
    df = np.maximum(dist, max_exact).astype(np.float32)
    large = max_exact + (np.log(df / np.float32(max_exact)) / np.float32(math.log(MAX_DISTANCE / max_exact))
                         * np.float32(NUM_BUCKETS - max_exact)).astype(np.int32)
    return np.where(dist < max_exact, dist, np.minimum(large, NUM_BUCKETS - 1)).astype(np.int32)


def _prompt_bucket_index():
    a = np.arange(Q_BLK)[:, None]
    c = np.arange(2 * Q_BLK)[None, :]
    step = Q_BLK + a - c
    valid = (step >= 0) & (step <= Q_BLK)
    out = []
    for _, d in BRANCHES:
        out.append(np.where(valid, _t5_bucket_np(np.clip(step, 0, Q_BLK) * d), -1))
    return np.stack(out).astype(np.int32)


def _sample_bucket_index(n_new, n_past):
    t = np.arange(n_new)[:, None]
    out_c, out_n = [], []
    for w, d in BRANCHES:
        dist_c = n_past + t - np.arange(n_past)[None, :]
        dist_n = t - np.arange(n_new)[None, :]
        for dist, out in ((dist_c, out_c), (dist_n, out_n)):
            valid = (dist >= 0) & (dist <= w) & (dist % d == 0)
            out.append(np.where(valid, _t5_bucket_np(np.clip(dist, 0, w)), -1))
    return np.stack(out_c).astype(np.int32), np.stack(out_n).astype(np.int32)


def _bias_body(rb_ref, idx_ref, out_ref):
    idx = idx_ref[0]
    for h in range(N_HEADS):
        acc = jnp.where(idx < 0, NEG_INF, 0.0).astype(F32)
        for b in range(NUM_BUCKETS):
            acc = acc + jnp.where(idx == b, rb_ref[b * N_HEADS + h], 0.0)
        out_ref[0, h] = acc


def _bias_expand(rel_bias, idx_np, name):
    nbr, r, c = idx_np.shape
    return pl.pallas_call(
        _bias_body,
        grid=(nbr,),
        in_specs=[pl.BlockSpec(memory_space=pltpu.SMEM),
                  pl.BlockSpec((1, r, c), lambda i: (i, 0, 0))],
        out_specs=pl.BlockSpec((1, N_HEADS, r, c), lambda i: (i, 0, 0, 0)),
        out_shape=jax.ShapeDtypeStruct((nbr, N_HEADS, r, c), F32),
        name=name,
    )(rel_bias.reshape(-1), jnp.asarray(idx_np))


def _bias_tables(rel_bias, n_new, n_past):
    bias_p = _bias_expand(rel_bias, _prompt_bucket_index(), "bias_prompt")
    bias_p = bias_p.reshape(len(BRANCHES), N_PAIRS, 2 * Q_BLK, 2 * Q_BLK)
    ic, inw = _sample_bucket_index(n_new, n_past)
    return bias_p, _bias_expand(rel_bias, ic, "bias_cache"), _bias_expand(rel_bias, inw, "bias_new")


def _ada_body(c_ref, w_ref, b_ref, o_ref):
    c = c_ref[...]
    s = c * (1.0 / (1.0 + jnp.exp(-c)))
    o_ref[...] = jnp.dot(s.astype(BF16), w_ref[...].astype(BF16), preferred_element_type=F32) + b_ref[...]


def _ada(c_all, w_ada, b_ada):
    n, dm = c_all.shape
    n_out = w_ada.shape[1]
    tn = dm
    return pl.pallas_call(
        _ada_body,
        grid=(n_out // tn,),
        in_specs=[pl.BlockSpec((n, dm), lambda j: (0, 0)),
                  pl.BlockSpec((dm, tn), lambda j: (0, j)),
                  pl.BlockSpec((1, tn), lambda j: (0, j))],
        out_specs=pl.BlockSpec((n, tn), lambda j: (0, j)),
        out_shape=jax.ShapeDtypeStruct((n, n_out), F32),
        compiler_params=_cparams(("parallel",), 32),
        name="ada",
    )(c_all, w_ada, b_ada.reshape(1, n_out))


_SHIFT1, _SCALE1, _GATE1, _SHIFT2, _SCALE2, _GATE2 = range(6)


def _inproj_body(*refs, nb, tm, carry):
    if carry:
        (x_ref, sh_ref, sc_ref, w_ref, cw_ref, ng_ref, past_ref,
         q_ref, k_ref, v_ref, conv_ref, cst_ref, kt_ref, vt_ref, zz_ref) = refs
    else:
        (x_ref, sh_ref, sc_ref, w_ref, cw_ref, ng_ref, past_ref,
         q_ref, k_ref, v_ref, conv_ref, cst_ref, zz_ref) = refs
    dm = x_ref.shape[-1]
    rows = nb * tm
    u = x_ref[...] * (1.0 + sc_ref[...]) + sh_ref[...]
    u = u.reshape(rows, dm).astype(BF16)

    def proj(lo, width):
        return jnp.dot(u, w_ref[:, lo:lo + width], preferred_element_type=F32)

    q_ref[...] = proj(0, D_ATTN).reshape(nb, tm, D_ATTN)
    k = proj(D_ATTN, D_ATTN)
    v = proj(2 * D_ATTN, D_ATTN)
    k_ref[...] = k.reshape(nb, tm, D_ATTN)
    v_ref[...] = v.reshape(nb, tm, D_ATTN)
    if carry:
        kt_ref[0] = k.T
        vt_ref[0] = v.T
    gb = proj(3 * D_ATTN, D_CONV)
    gc = proj(3 * D_ATTN + D_CONV, D_CONV)
    hh = proj(3 * D_ATTN + 2 * D_CONV, D_CONV)
    z = (gc * hh).reshape(nb, tm, D_CONV)

    if carry:
        s = pl.program_id(1)

        @pl.when(s == 0)
        def _():
            zz_ref[:, 0:V7X_SUBLANES, :] = jnp.zeros((nb, V7X_SUBLANES, D_CONV), F32)

        @pl.when(s > 0)
        def _():
            zz_ref[:, 0:V7X_SUBLANES, :] = zz_ref[:, tm:tm + V7X_SUBLANES, :]
    else:
        zz_ref[:, V7X_SUBLANES - 2:V7X_SUBLANES, :] = past_ref[...]
    zz_ref[:, V7X_SUBLANES:, :] = z

    cw = cw_ref[...]
    yc = (cw[0:1, :] * zz_ref[:, V7X_SUBLANES - 2:V7X_SUBLANES - 2 + tm, :]
          + cw[1:2, :] * zz_ref[:, V7X_SUBLANES - 1:V7X_SUBLANES - 1 + tm, :]
          + cw[2:3, :] * z)
    g = gb.reshape(nb, tm, D_CONV) * yc
    ms = jnp.mean(g * g, axis=-1, keepdims=True)
    conv_ref[...] = g * lax.rsqrt(ms + RMS_EPS) * ng_ref[...]
    cst_ref[...] = zz_ref[:, tm + V7X_SUBLANES - 2:tm + V7X_SUBLANES, :]


def _inproj(x, ada3, w_in_bf, conv_w, norm_conv_g, past, *, nb, tm):
    bsz, seq, dm = x.shape
    carry = past is None
    if carry:
        past = jnp.zeros((bsz, CONV_WIDTH - 1, D_CONV), F32)
    grid = (bsz // nb, seq // tm)
    d_in = w_in_bf.shape[1]
    body = functools.partial(_inproj_body, nb=nb, tm=tm, carry=carry)
    tile = lambda width: pl.BlockSpec((nb, tm, width), lambda b, s: (b, s, 0))
    in_specs = [tile(dm),
                pl.BlockSpec((nb, 1, dm), lambda b, s: (b, 0, _SHIFT1)),
                pl.BlockSpec((nb, 1, dm), lambda b, s: (b, 0, _SCALE1)),
                pl.BlockSpec((dm, d_in), lambda b, s: (0, 0)),
                pl.BlockSpec((CONV_WIDTH, D_CONV), lambda b, s: (0, 0)),
                pl.BlockSpec((1, D_CONV), lambda b, s: (0, 0)),
                pl.BlockSpec((nb, CONV_WIDTH - 1, D_CONV), lambda b, s: (b, 0, 0))]
    args = [x, ada3, ada3, w_in_bf, conv_w, norm_conv_g.reshape(1, D_CONV), past]
    out_specs = [tile(D_ATTN), tile(D_ATTN), tile(D_ATTN), tile(D_CONV),
                 pl.BlockSpec((nb, CONV_WIDTH - 1, D_CONV), lambda b, s: (b, 0, 0))]
    out_shape = ([jax.ShapeDtypeStruct((bsz, seq, D_ATTN), F32)] * 3
                 + [jax.ShapeDtypeStruct((bsz, seq, D_CONV), F32),
                    jax.ShapeDtypeStruct((bsz, CONV_WIDTH - 1, D_CONV), F32)])
    if carry:
        assert nb == 1
        out_specs += [pl.BlockSpec((1, D_ATTN, tm), lambda b, s: (b, 0, s))] * 2
        out_shape += [jax.ShapeDtypeStruct((bsz, D_ATTN, seq), F32)] * 2
    return pl.pallas_call(
        body,
        grid=grid,
        in_specs=in_specs,
        out_specs=out_specs,
        out_shape=out_shape,
        scratch_shapes=[pltpu.VMEM((nb, tm + V7X_SUBLANES, D_CONV), F32)],
        compiler_params=_cparams(("parallel", "arbitrary"), 56),
        name="inproj_prompt" if carry else "inproj_sample",
    )(*args)


def _attn_prompt_body(q_ref, k_ref, v_ref, bias_ref, o_ref, ob_ref, lb_ref, *, seq):
    scale = HEAD_DIM ** -0.5
    lane = lax.broadcasted_iota(I32, (Q_BLK, PAIR), 1)
    even = lane < HEAD_DIM
    ones = jnp.ones((Q_BLK, PAIR), BF16)

    def rows(ref, start, d):
        if d == 1:
            return ref[pl.ds(start, Q_BLK), :]
        return ref[pl.ds(start, Q_BLK, stride=d), :]

    def store(ref, br, start, d, val):
        if d == 1:
            ref[br, pl.ds(start, Q_BLK), :] = val
        else:
            ref[br, pl.ds(start, Q_BLK, stride=d), :] = val

    for br, (_, d) in enumerate(BRANCHES):
        nblk = seq // (d * Q_BLK)

        def block(it, carry, br=br, d=d, nblk=nblk):
            bias = bias_ref[br, 0]
            r = it // nblk
            i = it - r * nblk
            cur0 = r + d * Q_BLK * i
            qf = rows(q_ref, cur0, d) * scale
            qs = jnp.concatenate([jnp.where(even, qf, 0.0), jnp.where(even, 0.0, qf)], axis=0).astype(BF16)
            kc = rows(k_ref, cur0, d).astype(BF16)
            vc = rows(v_ref, cur0, d).astype(BF16)
            if nblk == 1:
                s = lax.dot_general(qs, kc, (((1,), (1,)), ((), ())), preferred_element_type=F32)
                s = s + bias[:, Q_BLK:]
                vaug = jnp.concatenate([vc, ones], axis=1)
            else:
                prev0 = r + d * Q_BLK * jnp.maximum(i - 1, 0)
                kp = rows(k_ref, prev0, d).astype(BF16)
                vp = rows(v_ref, prev0, d).astype(BF16)
                kk = jnp.concatenate([kp, kc], axis=0)
                s = lax.dot_general(qs, kk, (((1,), (1,)), ((), ())), preferred_element_type=F32)
                col = lax.broadcasted_iota(I32, (1, 2 * Q_BLK), 1)
                first = jnp.where((col < Q_BLK) & (i == 0), NEG_INF, 0.0)
                s = s + bias + first
                vaug = jnp.concatenate([jnp.concatenate([vp, vc], axis=0),
                                        jnp.concatenate([ones, ones], axis=0)], axis=1)
            m = jnp.max(s, axis=-1, keepdims=True)
            p = jnp.exp(s - m).astype(BF16)
            out = jnp.dot(p, vaug, preferred_element_type=F32)
            num, den = out[:, :PAIR], out[:, PAIR:]
            o = num / den
            lse = m + jnp.log(den)
            store(ob_ref, br, cur0, d, jnp.where(even, o[:Q_BLK], o[Q_BLK:]))
            store(lb_ref, br, cur0, d, jnp.where(even, lse[:Q_BLK], lse[Q_BLK:]))
            return carry

        lax.fori_loop(0, d * nblk, block, 0, unroll=ATTN_UNROLL)

    mt = 2 * Q_BLK

    def merge(i, carry):
        sl = pl.ds(pl.multiple_of(i * mt, mt), mt)
        l0, l1, l2 = lb_ref[0, sl, :], lb_ref[1, sl, :], lb_ref[2, sl, :]
        mx = jnp.maximum(jnp.maximum(l0, l1), l2)
        w0, w1, w2 = jnp.exp(l0 - mx), jnp.exp(l1 - mx), jnp.exp(l2 - mx)
        acc = w0 * ob_ref[0, sl, :] + w1 * ob_ref[1, sl, :] + w2 * ob_ref[2, sl, :]
        o_ref[sl, :] = acc / (w0 + w1 + w2)
        return carry

    lax.fori_loop(0, seq // mt, merge, 0)


def _attn_prompt(q, k, v, bias_p):
    bsz, seq, _ = q.shape
    assert seq % (BRANCHES[-1][1] * Q_BLK) == 0
    body = functools.partial(_attn_prompt_body, seq=seq)
    col = pl.BlockSpec((None, seq, PAIR), lambda b, j: (b, 0, j))
    return pl.pallas_call(
        body,
        grid=(bsz, N_PAIRS),
        in_specs=[col, col, col,
                  pl.BlockSpec((len(BRANCHES), 1, 2 * Q_BLK, 2 * Q_BLK), lambda b, j: (0, j, 0, 0))],
        out_specs=col,
        out_shape=jax.ShapeDtypeStruct((bsz, seq, D_ATTN), F32),
        scratch_shapes=[pltpu.VMEM((len(BRANCHES), seq, PAIR), F32),
                        pltpu.VMEM((len(BRANCHES), seq, PAIR), F32)],
        compiler_params=_cparams(("parallel", "parallel"), 48),
        name="attn_prompt",
    )(q, k, v, bias_p)


def _attn_sample_body(q_ref, kn_ref, vn_ref, ckt_ref, cvt_ref, bc_ref, bn_ref, o_ref):
    scale = HEAD_DIM ** -0.5
    nt = (((1,), (1,)), ((), ()))
    nbr = len(BRANCHES)
    for h in range(N_HEADS):
        rows = slice(h * HEAD_DIM, (h + 1) * HEAD_DIM)
        qh = (q_ref[h] * scale).astype(BF16)
        s_c = jnp.dot(qh, ckt_ref[rows, :].astype(BF16), preferred_element_type=F32)
        s_n = lax.dot_general(qh, kn_ref[h].astype(BF16), nt, preferred_element_type=F32)
        ps, lses, dens = [], [], []
        for br in range(nbr):
            lc = s_c + bc_ref[br, h]
            ln = s_n + bn_ref[br, h]
            m = jnp.maximum(jnp.max(lc, axis=-1, keepdims=True), jnp.max(ln, axis=-1, keepdims=True))
            pc = jnp.exp(lc - m)
            pn = jnp.exp(ln - m)
            den = jnp.sum(pc, axis=-1, keepdims=True) + jnp.sum(pn, axis=-1, keepdims=True)
            ps.append((pc, pn))
            dens.append(den)
            lses.append(m + jnp.log(den))
        mx = jnp.maximum(jnp.maximum(lses[0], lses[1]), lses[2])
        ws = [jnp.exp(l - mx) for l in lses]
        wsum = ws[0] + ws[1] + ws[2]
        coefs = [ws[br] / (wsum * dens[br]) for br in range(nbr)]
        p_c = coefs[0] * ps[0][0] + coefs[1] * ps[1][0] + coefs[2] * ps[2][0]
        p_n = coefs[0] * ps[0][1] + coefs[1] * ps[1][1] + coefs[2] * ps[2][1]
        o = lax.dot_general(p_c.astype(BF16), cvt_ref[rows, :].astype(BF16), nt, preferred_element_type=F32)
        o_ref[h] = o + jnp.dot(p_n.astype(BF16), vn_ref[h].astype(BF16), preferred_element_type=F32)


def _attn_sample(q, k_new, v_new, cache_kt, cache_vt, bias_c, bias_n):
    bsz, _, n_new, _ = q.shape
    n_past = cache_kt.shape[2]
    new = pl.BlockSpec((None, N_HEADS, n_new, HEAD_DIM), lambda b: (b, 0, 0, 0))
    cache = pl.BlockSpec((None, D_ATTN, n_past), lambda b: (b, 0, 0))
    return pl.pallas_call(
        _attn_sample_body,
        grid=(bsz,),
        in_specs=[new, new, new, cache, cache,
                  pl.BlockSpec(bias_c.shape, lambda b: (0, 0, 0, 0)), pl.BlockSpec(bias_n.shape, lambda b: (0, 0, 0, 0))],
        out_specs=new,
        out_shape=jax.ShapeDtypeStruct((bsz, N_HEADS, n_new, HEAD_DIM), F32),
        compiler_params=_cparams(("parallel",), 48),
        name="attn_sample",
    )(q, k_new, v_new, cache_kt, cache_vt, bias_c, bias_n)


def _layer_norm(y, g, b):
    mu = jnp.mean(y, axis=-1, keepdims=True)
    c = y - mu
    var = jnp.mean(c * c, axis=-1, keepdims=True)
    return c * lax.rsqrt(var + LN_EPS) * g + b


def _to_row_tiles(ref, val):
    rows, dm = val.shape
    assert dm == V7X_SUBLANES * V7X_LANES
    for c in range(V7X_SUBLANES):
        ref[pl.ds(c, rows, stride=V7X_SUBLANES), :] = val[:, c * V7X_LANES:(c + 1) * V7X_LANES]


def _from_row_tiles(ref, rows, lead=()):
    chunks = [ref[lead + (pl.ds(c, rows, stride=V7X_SUBLANES), slice(None))] for c in range(V7X_SUBLANES)]
    return jnp.concatenate(chunks, axis=1)


def _outproj_body(*refs, nb, tm, alpha, aliased, n_main):
    if aliased:
        refs = refs[2:]
    (a_ref, c_ref, x_ref, g1_ref, sh2_ref, sc2_ref, wa_ref, wc_ref, ng_ref, lg_ref, lb_ref, x1_ref, u2_ref) = refs
    rows = nb * tm

    @pl.when(pl.program_id(0) < n_main)
    def _():
        a = a_ref[...]
        ms = jnp.mean(a * a, axis=-1, keepdims=True)
        an = (a * lax.rsqrt(ms + RMS_EPS) * ng_ref[...]).reshape(rows, D_ATTN).astype(BF16)
        cn = c_ref[...].reshape(rows, D_CONV).astype(BF16)
        mix = (jnp.dot(an, wa_ref[...], preferred_element_type=F32)
               + jnp.dot(cn, wc_ref[...], preferred_element_type=F32))
        dm = mix.shape[-1]
        y = alpha * x_ref[...] + (1.0 + g1_ref[...]) * mix.reshape(nb, tm, dm)
        x1 = _layer_norm(y, lg_ref[...], lb_ref[...])
        x1_ref[...] = x1.reshape(rows, dm)
        u2_ref[...] = (x1 * (1.0 + sc2_ref[...]) + sh2_ref[...]).reshape(rows, dm)

    @pl.when(pl.program_id(0) >= n_main)
    def _():
        x1_ref[...] = jnp.zeros_like(x1_ref)
        u2_ref[...] = jnp.zeros_like(u2_ref)


def _outproj(attn, conv, x, ada3, w_o_bf, norm_attn_g, ln1_g, ln1_b, *, nb, tm, alpha, n_total, row0, prev=None):
    bsz, seq, dm = x.shape
    rows = nb * tm
    assert row0 % rows == 0
    st = seq // tm
    blk0 = row0 // rows
    n_main = (bsz // nb) * st
    aliased = prev is not None
    n_steps = n_main if aliased else pl.cdiv(n_total, rows)

    def bs_of(i):
        j = jnp.minimum(i, n_main - 1)
        return j // st, j % st

    tile = lambda width: pl.BlockSpec((nb, tm, width), lambda i: bs_of(i) + (0,))
    ada = lambda chunk: pl.BlockSpec((nb, 1, dm), lambda i: (bs_of(i)[0], 0, chunk))
    const = lambda shape: pl.BlockSpec(shape, lambda i: (0,) * len(shape))
    out = pl.BlockSpec((rows, dm), lambda i: (blk0 + i, 0))
    body = functools.partial(_outproj_body, nb=nb, tm=tm, alpha=alpha, aliased=aliased, n_main=n_main)
    in_specs = [tile(D_ATTN), tile(D_CONV), tile(dm), ada(_GATE1), ada(_SHIFT2), ada(_SCALE2),
                const((D_ATTN, dm)), const((D_CONV, dm)), const((1, D_ATTN)), const((1, dm)), const((1, dm))]
    args = [attn, conv, x, ada3, ada3, ada3, w_o_bf[:D_ATTN], w_o_bf[D_ATTN:],
            norm_attn_g.reshape(1, D_ATTN), ln1_g.reshape(1, dm), ln1_b.reshape(1, dm)]
    kwargs = {}
    if aliased:
        in_specs = [pl.BlockSpec(memory_space=pl.ANY)] * 2 + in_specs
        args = list(prev) + args
        kwargs["input_output_aliases"] = {0: 0, 1: 1}
    return pl.pallas_call(
        body,
        grid=(n_steps,),
        in_specs=in_specs,
        out_specs=[out, out],
        out_shape=[jax.ShapeDtypeStruct((n_total, dm), F32)] * 2,
        compiler_params=_cparams(("parallel",), 48),
        name="outproj_sample" if aliased else "outproj_prompt",
        **kwargs,
    )(*args)


def _router_body(u_ref, w_ref, b_ref, ri_ref, rr_ref, rg_ref, cnt_ref, run_ref, *, tm):
    i = pl.program_id(0)

    @pl.when(i == 0)
    def _():
        run_ref[...] = jnp.zeros_like(run_ref)

    logits = jnp.dot(u_ref[...], w_ref[...], preferred_element_type=F32,
                     precision=lax.Precision.HIGHEST) + b_ref[...]
    lane_i = lax.broadcasted_iota(I32, (tm, V7X_LANES), 1)
    lane = lane_i.astype(F32)
    vals = logits
    tops, idxs = [], []
    for _ in range(TOP_K):
        mk = jnp.max(vals, axis=-1, keepdims=True)
        ik = jnp.min(jnp.where(vals == mk, lane, float(V7X_LANES)), axis=-1, keepdims=True)
        tops.append(mk)
        idxs.append(ik)
        vals = jnp.where(lane == ik, -jnp.inf, vals)
    es = [jnp.exp(t - tops[0]) for t in tops]
    den = es[0] + es[1] + es[2] + es[3]
    sel = jnp.zeros((tm, V7X_LANES), F32)
    for ik in idxs:
        sel = sel + jnp.where(lane == ik, 1.0, 0.0)
    ri = lax.broadcasted_iota(I32, (tm, tm), 0)
    ci = lax.broadcasted_iota(I32, (tm, tm), 1)
    lower = jnp.where(ci < ri, 1.0, 0.0).astype(BF16)
    before = jnp.dot(lower, sel.astype(BF16), preferred_element_type=F32) + run_ref[...]
    out_i = jnp.full((tm, V7X_LANES), -1, I32)
    out_r = jnp.zeros((tm, V7X_LANES), I32)
    out_g = jnp.zeros((tm, V7X_LANES), F32)
    for kk in range(TOP_K):
        rank = jnp.sum(jnp.where(lane == idxs[kk], before, 0.0), axis=-1, keepdims=True)
        out_i = jnp.where(lane_i == kk, idxs[kk].astype(I32), out_i)
        out_r = jnp.where(lane_i == kk, rank.astype(I32), out_r)
        out_g = jnp.where(lane_i == kk, es[kk] / den, out_g)
    ri_ref[...] = out_i
    rr_ref[...] = out_r
    rg_ref[...] = out_g
    run_ref[...] = run_ref[...] + jnp.sum(sel, axis=0, keepdims=True)
    cnt_ref[...] = run_ref[...]


def _router(u2, router_w, router_b, *, tm):
    n, dm = u2.shape
    w_pad = jnp.pad(router_w, ((0, 0), (0, V7X_LANES - N_EXPERTS)))
    b_pad = jnp.pad(router_b, (0, V7X_LANES - N_EXPERTS), constant_values=NEG_INF).reshape(1, V7X_LANES)
    tok = pl.BlockSpec((tm, V7X_LANES), lambda i: (i, 0))
    return pl.pallas_call(
        functools.partial(_router_body, tm=tm),
        grid=(n // tm,),
        in_specs=[pl.BlockSpec((tm, dm), lambda i: (i, 0)),
                  pl.BlockSpec((dm, V7X_LANES), lambda i: (0, 0)),
                  pl.BlockSpec((1, V7X_LANES), lambda i: (0, 0))],
        out_specs=[tok, tok, tok, pl.BlockSpec((1, V7X_LANES), lambda i: (0, 0))],
        out_shape=[jax.ShapeDtypeStruct((n, V7X_LANES), I32), jax.ShapeDtypeStruct((n, V7X_LANES), I32),
                   jax.ShapeDtypeStruct((n, V7X_LANES), F32), jax.ShapeDtypeStruct((1, V7X_LANES), F32)],
        scratch_shapes=[pltpu.VMEM((1, V7X_LANES), F32)],
        compiler_params=_cparams(("arbitrary",), 32),
        name="router",
    )(u2, w_pad, b_pad)


def _dest_body(ps_ref, ri_ref, rr_ref, d_ref):
    idx = ri_ref[...]
    acc = rr_ref[...]
    for e in range(N_EXPERTS):
        acc = acc + jnp.where(idx == e, ps_ref[e], 0)
    d_ref[...] = acc


def _dest_rows(expert, rank, pad_starts):
    return pl.pallas_call(
        _dest_body,
        in_specs=[pl.BlockSpec(memory_space=pltpu.SMEM), pl.BlockSpec(memory_space=pltpu.VMEM),
                  pl.BlockSpec(memory_space=pltpu.VMEM)],
        out_specs=pl.BlockSpec(memory_space=pltpu.VMEM),
        out_shape=jax.ShapeDtypeStruct(expert.shape, I32),
        name="dest_rows",
    )(pad_starts, expert, rank)


ISSUE_TOKENS = 2


def _row_tile(ref, row):
    return ref.at[pl.ds(pl.multiple_of(row * V7X_SUBLANES, V7X_SUBLANES), V7X_SUBLANES), :]


def _issue_rows(dest_ref, n_tok, start_copy):
    def trip(it, carry):
        base = it * ISSUE_TOKENS
        rows = [dest_ref[0, 0, (base + u) * TOP_K + kk] for u in range(ISSUE_TOKENS) for kk in range(TOP_K)]
        for u in range(ISSUE_TOKENS):
            for kk in range(TOP_K):
                start_copy(base + u, kk, rows[u * TOP_K + kk])
        return carry

    lax.fori_loop(0, n_tok // ISSUE_TOKENS, trip, 0)


DISPATCH_SLOTS = 3


FILL_UNROLL = 8


def _dispatch_body(fr_ref, dest_ref, u_ref, xs_out, stage_ref, zero_ref, sems, fill_sem, *, tm, n_fill):
    i = pl.program_id(0)
    last = pl.num_programs(0) - 1
    slot = i % DISPATCH_SLOTS

    def wait_step(s):
        for _ in range(TOP_K):
            pltpu.make_async_copy(stage_ref.at[s], xs_out.at[pl.ds(0, tm * V7X_SUBLANES), :], sems.at[s]).wait()

    @pl.when(i == 0)
    def _():
        zero_ref[...] = jnp.zeros_like(zero_ref)

        def trip(it, carry):
            rows = [fr_ref[it * FILL_UNROLL + u] for u in range(FILL_UNROLL)]
            for u, row in enumerate(rows):
                pltpu.make_async_copy(zero_ref, _row_tile(xs_out, row), fill_sem).start(priority=u % DMA_QUEUES)
            return carry

        lax.fori_loop(0, n_fill // FILL_UNROLL, trip, 0)

    @pl.when(i >= DISPATCH_SLOTS - 1)
    def _():
        wait_step((i + 1) % DISPATCH_SLOTS)

    _to_row_tiles(stage_ref.at[slot], u_ref[...])

    def start_copy(r, kk, row):
        pltpu.make_async_copy(_row_tile(stage_ref.at[slot], r), _row_tile(xs_out, row),
                              sems.at[slot]).start(priority=kk % DMA_QUEUES)

    _issue_rows(dest_ref, tm, start_copy)

    @pl.when(i == last)
    def _():
        for back in range(DISPATCH_SLOTS - 1):
            @pl.when(i >= back)
            def _():
                wait_step((i - back) % DISPATCH_SLOTS)
        pltpu.make_async_copy(xs_out.at[pl.ds(0, n_fill * V7X_SUBLANES), :],
                              xs_out.at[pl.ds(0, n_fill * V7X_SUBLANES), :], fill_sem).wait()


def _dispatch(u2, dest_sm, fill_rows, n_rows, *, tm):
    n, dm = u2.shape
    n_fill = fill_rows.shape[0]
    assert n_fill == n_rows - n * TOP_K and n_fill % FILL_UNROLL == 0
    grid_spec = pltpu.PrefetchScalarGridSpec(
        num_scalar_prefetch=1,
        grid=(n // tm,),
        in_specs=[pl.BlockSpec((1, 1, tm * TOP_K), lambda i, fr: (i, 0, 0), memory_space=pltpu.SMEM),
                  pl.BlockSpec((tm, dm), lambda i, fr: (i, 0))],
        out_specs=pl.BlockSpec(memory_space=pl.ANY),
        scratch_shapes=[pltpu.VMEM((DISPATCH_SLOTS, tm * V7X_SUBLANES, V7X_LANES), F32),
                        pltpu.VMEM((V7X_SUBLANES, V7X_LANES), F32),
                        pltpu.SemaphoreType.DMA((DISPATCH_SLOTS,)), pltpu.SemaphoreType.DMA(())],
    )
    return pl.pallas_call(
        functools.partial(_dispatch_body, tm=tm, n_fill=n_fill),
        grid_spec=grid_spec,
        out_shape=jax.ShapeDtypeStruct((n_rows * V7X_SUBLANES, V7X_LANES), F32),
        compiler_params=_cparams(("arbitrary",), 32),
        name="dispatch",
    )(fill_rows, dest_sm, u2)


def _gmm_body(be_ref, bv_ref, bf_ref, bs_ref, bn_ref, xs_ref, wu_hbm, bu_ref, wd_hbm, bd_ref, ys_ref,
              wu_f32, wd_f32, wu_bf, wd_bf, sems, *, bm):
    j = pl.program_id(0)
    e = be_ref[j]
    s = bs_ref[j]
    d_ff = wd_hbm.shape[1]
    chunk = 64

    def fetch(ex, slot):
        return (pltpu.make_async_copy(wu_hbm.at[ex], wu_f32.at[slot], sems.at[0, slot]),
                pltpu.make_async_copy(wd_hbm.at[ex], wd_f32.at[slot], sems.at[1, slot]))

    @pl.when(bf_ref[j] != 0)
    def _():
        @pl.when(j == 0)
        def _():
            for c in fetch(e, s):
                c.start()

        for c in fetch(e, s):
            c.wait()

        @pl.when(bn_ref[j] >= 0)
        def _():
            for c in fetch(bn_ref[j], 1 - s):
                c.start()

        def cast(c, carry):
            sl = pl.ds(pl.multiple_of(c * chunk, chunk), chunk)
            wu_bf[sl, :] = wu_f32[s, sl, :].astype(BF16)
            wd_bf[sl, :] = wd_f32[s, sl, :].astype(BF16)
            return carry

        lax.fori_loop(0, wu_hbm.shape[1] // chunk, cast, 0)

    @pl.when(bv_ref[j] != 0)
    def _():
        for g in range(bm // EXPERT_ROWS):
            tiles = pl.ds(g * EXPERT_ROWS * V7X_SUBLANES, EXPERT_ROWS * V7X_SUBLANES)
            x = _from_row_tiles(xs_ref.at[tiles, :], EXPERT_ROWS).astype(BF16)
            glu = jnp.dot(x, wu_bf[:, :d_ff], preferred_element_type=F32) + bu_ref[0, :, :d_ff]
            lin = jnp.dot(x, wu_bf[:, d_ff:], preferred_element_type=F32) + bu_ref[0, :, d_ff:]
            glu = jnp.minimum(glu, SWIGLU_LIMIT)
            lin = jnp.clip(lin, -SWIGLU_LIMIT, SWIGLU_LIMIT)
            act = glu * (1.0 / (1.0 + jnp.exp(-SWIGLU_ALPHA * glu))) * (lin + 1.0)
            _to_row_tiles(ys_ref.at[tiles, :],
                          jnp.dot(act.astype(BF16), wd_bf[...], preferred_element_type=F32) + bd_ref[0])

    @pl.when(bv_ref[j] == 0)
    def _():
        ys_ref[...] = jnp.zeros_like(ys_ref)


def _gmm(xs, blk_e, blk_valid, blk_first, blk_slot, blk_next, w_up, b_up, w_down, b_down, *, bm):
    n_rows = xs.shape[0] // V7X_SUBLANES
    n_e, dm, d_up = w_up.shape
    d_ff = w_down.shape[1]
    assert d_ff == dm
    row_tiles = pl.BlockSpec((bm * V7X_SUBLANES, V7X_LANES), lambda j, *_: (j, 0))
    grid_spec = pltpu.PrefetchScalarGridSpec(
        num_scalar_prefetch=5,
        grid=(n_rows // bm,),
        in_specs=[row_tiles,
                  pl.BlockSpec(memory_space=pl.ANY),
                  pl.BlockSpec((1, 1, d_up), lambda j, be, *_: (be[j], 0, 0)),
                  pl.BlockSpec(memory_space=pl.ANY),
                  pl.BlockSpec((1, 1, dm), lambda j, be, *_: (be[j], 0, 0))],
        out_specs=row_tiles,
        scratch_shapes=[pltpu.VMEM((2, dm, d_up), F32), pltpu.VMEM((2, d_ff, dm), F32),
                        pltpu.VMEM((dm, d_up), BF16), pltpu.VMEM((d_ff, dm), BF16),
                        pltpu.SemaphoreType.DMA((2, 2))],
    )
    return pl.pallas_call(
        functools.partial(_gmm_body, bm=bm),
        grid_spec=grid_spec,
        out_shape=jax.ShapeDtypeStruct(xs.shape, F32),
        compiler_params=_cparams(("arbitrary",), 56),
        name="expert_mlp",
    )(blk_e, blk_valid, blk_first, blk_slot, blk_next, xs, w_up, b_up.reshape(n_e, 1, d_up), w_down,
      b_down.reshape(n_e, 1, dm))


def _combine_body(rt_ref, rtn_ref, ys_hbm, rg_ref, x1_ref, g2_ref, lg_ref, lb_ref, y_ref, rows_ref, sems,
                  *, nb, tm, alpha, n_steps):
    n_tok = nb * tm
    i = pl.program_id(0)
    slot = i % 2

    def issue(dest_ref, s):
        def start_copy(r, kk, row):
            pltpu.make_async_copy(_row_tile(ys_hbm, row), _row_tile(rows_ref.at[s, kk], r),
                                  sems.at[s]).start(priority=kk % DMA_QUEUES)

        _issue_rows(dest_ref, n_tok, start_copy)

    @pl.when(i == 0)
    def _():
        issue(rt_ref, 0)

    @pl.when(i + 1 < n_steps)
    def _():
        issue(rtn_ref, 1 - slot)

    for kk in range(TOP_K):
        pltpu.make_async_copy(ys_hbm.at[pl.ds(0, n_tok * V7X_SUBLANES), :], rows_ref.at[slot, kk],
                              sems.at[slot]).wait()

    gates = rg_ref[...]
    ffn = gates[:, 0:1] * _from_row_tiles(rows_ref, n_tok, (slot, 0))
    for kk in range(1, TOP_K):
        ffn = ffn + gates[:, kk:kk + 1] * _from_row_tiles(rows_ref, n_tok, (slot, kk))
    dm = ffn.shape[-1]
    y = alpha * x1_ref[...].reshape(nb, tm, dm) + (1.0 + g2_ref[...]) * ffn.reshape(nb, tm, dm)
    y_ref[...] = _layer_norm(y, lg_ref[...], lb_ref[...])


def _combine(ys, dest_sm, route_g, x1_all, ada3, ln2_g, ln2_b, *, bsz, seq, nb, tm, alpha, row0):
    dm = x1_all.shape[1]
    n_tok = nb * tm
    assert row0 % n_tok == 0 and dest_sm.shape[2] == n_tok * TOP_K and n_tok % ISSUE_TOKENS == 0
    blk0 = row0 // n_tok
    st = seq // tm
    n_steps = (bsz // nb) * st
    dest = lambda off: pl.BlockSpec((1, 1, n_tok * TOP_K),
                                    lambda i: (blk0 + jnp.minimum(i + off, n_steps - 1), 0, 0),
                                    memory_space=pltpu.SMEM)
    return pl.pallas_call(
        functools.partial(_combine_body, nb=nb, tm=tm, alpha=alpha, n_steps=n_steps),
        grid=(n_steps,),
        in_specs=[dest(0), dest(1),
                  pl.BlockSpec(memory_space=pl.ANY),
                  pl.BlockSpec((n_tok, V7X_LANES), lambda i: (blk0 + i, 0)),
                  pl.BlockSpec((n_tok, dm), lambda i: (blk0 + i, 0)),
                  pl.BlockSpec((nb, 1, dm), lambda i: (i // st, 0, _GATE2)),
                  pl.BlockSpec((1, dm), lambda i: (0, 0)),
                  pl.BlockSpec((1, dm), lambda i: (0, 0))],
        out_specs=pl.BlockSpec((nb, tm, dm), lambda i: (i // st, i % st, 0)),
        scratch_shapes=[pltpu.VMEM((2, TOP_K, n_tok * V7X_SUBLANES, V7X_LANES), F32),
                        pltpu.SemaphoreType.DMA((2,))],
        out_shape=jax.ShapeDtypeStruct((bsz, seq, dm), F32),
        compiler_params=_cparams(("arbitrary",), 48),
        name="combine_prompt" if row0 == 0 else "combine_sample",
    )(dest_sm, dest_sm, ys, route_g, x1_all, ada3, ln2_g.reshape(1, dm), ln2_b.reshape(1, dm))


def _layer(xp, xs, cache_k, cache_v, conv_past, cp, cs, w_ada, b_ada, w_in, conv_w, norm_attn_g, norm_conv_g,
           w_o, ln1_g, ln1_b, bias_tabs, router_w, router_b, w_up, b_up, w_down, b_down, ln2_g, ln2_b, alpha):
    bp, sp, dm = xp.shape
    bs, ts, _ = xs.shape
    n_p, n_s = bp * sp, bs * ts
    n_tok = n_p + n_s
    assert n_p % TOK_TILE == 0 and n_s == TOK_TILE and sp % ROW_TILE == 0
    bias_p, bias_c, bias_n = bias_tabs

    ada = _ada(jnp.concatenate([cp, cs], axis=0), w_ada, b_ada)
    ada_p = ada[:bp].reshape(bp, 1, -1)
    ada_s = ada[bp:].reshape(bs, 1, -1)
    w_in_bf = w_in.astype(BF16)
    w_o_bf = w_o.astype(BF16)

    qp, kp, vp, convp, cstp, kpt, vpt = _inproj(xp, ada_p, w_in_bf, conv_w, norm_conv_g, None, nb=1, tm=ROW_TILE)
    attn_p = _attn_prompt(qp, kp, vp, bias_p)
    x1_all, u2_all = _outproj(attn_p, convp, xp, ada_p, w_o_bf, norm_attn_g, ln1_g, ln1_b,
                              nb=1, tm=ROW_TILE, alpha=alpha, n_total=n_tok, row0=0)
    qs, ks, vs, convs, csts = _inproj(xs, ada_s, w_in_bf, conv_w, norm_conv_g, conv_past, nb=bs, tm=ts)
    heads = lambda a: jnp.transpose(a.reshape(bs, ts, N_HEADS, HEAD_DIM), (0, 2, 1, 3))
    n_past = cache_k.shape[1]
    feat_major = lambda c: jnp.transpose(c.reshape(bs, n_past, D_ATTN), (0, 2, 1))
    attn_s = _attn_sample(heads(qs), heads(ks), heads(vs), feat_major(cache_k), feat_major(cache_v), bias_c, bias_n)
    attn_s = jnp.transpose(attn_s, (0, 2, 1, 3)).reshape(bs, ts, D_ATTN)
    x1_all, u2_all = _outproj(attn_s, convs, xs, ada_s, w_o_bf, norm_attn_g, ln1_g, ln1_b,
                              nb=bs, tm=ts, alpha=alpha, n_total=n_tok, row0=n_p, prev=(x1_all, u2_all))

    route_i, route_r, route_g, counts = _router(u2_all, router_w, router_b, tm=TOK_TILE)
    counts = counts[0, :N_EXPERTS].astype(I32)
    padded = ((counts + EXPERT_BLK - 1) // EXPERT_BLK) * EXPERT_BLK
    pad_ends = jnp.cumsum(padded)
    pad_starts = (pad_ends - padded).astype(I32)
    n_blocks = n_tok * TOP_K // EXPERT_BLK + N_EXPERTS
    n_rows = n_blocks * EXPERT_BLK
    blk_row = jnp.arange(n_blocks, dtype=I32) * EXPERT_BLK
    blk_valid = (blk_row < pad_ends[-1]).astype(I32)
    blk_e = jnp.sum((blk_row[:, None] >= pad_ends[None, :]).astype(I32), axis=1)
    last_e = jnp.sum((pad_ends[-1] - 1 >= pad_ends).astype(I32))
    blk_e = jnp.where(blk_valid != 0, blk_e, last_e).astype(I32)
    ar = jnp.arange(N_EXPERTS, dtype=I32)
    has_rows = padded > 0
    slot_of_e = (jnp.cumsum(has_rows.astype(I32)) - 1) % 2
    next_of_e = jnp.min(jnp.where((ar[None, :] > ar[:, None]) & has_rows[None, :], ar[None, :], N_EXPERTS), axis=1)
    next_of_e = jnp.where(next_of_e < N_EXPERTS, next_of_e, -1)
    is_e = blk_e[:, None] == ar[None, :]
    pick = lambda per_expert: jnp.sum(jnp.where(is_e, per_expert[None, :], 0), axis=1).astype(I32)
    blk_first = (blk_valid * (blk_row == pick(pad_starts)).astype(I32)).astype(I32)
    blk_slot = pick(slot_of_e)
    blk_next = pick(next_of_e)
    dense = lambda a: a[:, :TOP_K].reshape(n_tok * TOP_K // V7X_LANES, V7X_LANES)
    dest = _dest_rows(dense(route_i), dense(route_r), pad_starts)
    dest_sm = dest.reshape(n_tok // TOK_TILE, 1, TOK_TILE * TOP_K)

    run_start = jnp.concatenate([pad_starts + counts, pad_ends[-1:]])
    run_len = jnp.concatenate([padded - counts, n_rows - pad_ends[-1:]])
    run_k0 = jnp.cumsum(run_len) - run_len
    k = jnp.arange(n_rows - n_tok * TOP_K, dtype=I32)[:, None]
    in_run = (k >= run_k0[None, :]) & (k < (run_k0 + run_len)[None, :])
    fill_rows = jnp.sum(jnp.where(in_run, run_start[None, :] + k - run_k0[None, :], 0), axis=1).astype(I32)
    x_sorted = _dispatch(u2_all, dest_sm, fill_rows, n_rows, tm=TOK_TILE)
    y_sorted = _gmm(x_sorted, blk_e, blk_valid, blk_first, blk_slot, blk_next, w_up, b_up, w_down, b_down,
                    bm=EXPERT_BLK)
    yp = _combine(y_sorted, dest_sm, route_g, x1_all, ada_p, ln2_g, ln2_b,
                  bsz=bp, seq=sp, nb=1, tm=TOK_TILE, alpha=alpha, row0=0)
    ys_out = _combine(y_sorted, dest_sm, route_g, x1_all, ada_s, ln2_g, ln2_b,
                      bsz=bs, seq=ts, nb=bs, tm=ts, alpha=alpha, row0=n_p)
    return yp, ys_out, kpt, vpt, cstp, ks, vs, csts


def kernel(x_prompt, x_sample, cache_k, cache_v, state_conv, c_prompt, c_sample, w_ada, b_ada, w_in, conv_w,
           norm_attn_g, norm_conv_g, w_o, ln1_g, ln1_b, rel_bias, router_w, router_b, w_up, b_up, w_down, b_down,
           ln2_g, ln2_b):
    depth = w_ada.shape[0]
    alpha = (2 * depth) ** 0.25
    xp, xs = x_prompt, x_sample
    bp, sp, _ = xp.shape
    bs, ts, _ = xs.shape
    n_keep = min(BRANCHES[-1][0], sp)
    bias_tabs = _bias_tables(rel_bias, ts, cache_k.shape[2])
    outs = [[] for _ in range(6)]
    for l in range(depth):
        xp, xs, kp, vp, cstp, ks, vs, csts = _layer(
            xp, xs, cache_k[l], cache_v[l], state_conv[l], c_prompt, c_sample, w_ada[l], b_ada[l], w_in[l],
            conv_w[l], norm_attn_g[l], norm_conv_g[l], w_o[l], ln1_g[l], ln1_b[l], bias_tabs, router_w[l],
            router_b[l], w_up[l], b_up[l], w_down[l], b_down[l], ln2_g[l], ln2_b[l], alpha)
        kp = jnp.transpose(kp, (0, 2, 1)).reshape(bp, sp, N_HEADS, HEAD_DIM)[:, -n_keep:]
        vp = jnp.transpose(vp, (0, 2, 1)).reshape(bp, sp, N_HEADS, HEAD_DIM)[:, -n_keep:]
        for lst, val in zip(outs, (kp, vp, cstp, ks.reshape(bs, ts, N_HEADS, HEAD_DIM),
                                   vs.reshape(bs, ts, N_HEADS, HEAD_DIM), csts)):
            lst.append(val)
    return (xp, xs) + tuple(jnp.stack(o) for o in outs)
```

```python
import functools
import math

import numpy as np
import jax
import jax.numpy as jnp
from jax import lax
from jax.experimental import pallas as pl
from jax.experimental.pallas import tpu as pltpu

F32 = jnp.float32
BF16 = jnp.bfloat16
I32 = jnp.int32

HEAD_DIM = 64
N_HEADS = 12
D_ATTN = N_HEADS * HEAD_DIM
D_CONV = 256
CONV_WIDTH = 3
BRANCHES = ((128, 1), (512, 4), (2048, 16))
NUM_BUCKETS = 32
MAX_DISTANCE = 2048
N_EXPERTS = 32
TOP_K = 4
SWIGLU_LIMIT = 7.0
SWIGLU_ALPHA = 1.702
LN_EPS = 1e-5
RMS_EPS = 1e-6
NEG_INF = -1e30

V7X_LANES = 128
V7X_SUBLANES = 8
V7X_VMEM_BYTES = 64 * 1024 * 1024
DMA_QUEUES = 2

Q_BLK = 128
PAIR = 2 * HEAD_DIM
N_PAIRS = N_HEADS // 2
TOK_TILE = 256
ROW_TILE = 512
OUT_TILE = 512
EXPERT_BLK = 256
EXPERT_ROWS = 256
ATTN_UNROLL = 16


def _cparams(sem, vmem_mb):
    return pltpu.CompilerParams(dimension_semantics=sem, vmem_limit_bytes=vmem_mb * 1024 * 1024)


def _t5_bucket_np(dist):
    dist = np.asarray(dist, np.int64)
    max_exact = NUM_BUCKETS // 2
    df = np.maximum(dist, max_exact).astype(np.float32)
    large = max_exact + (np.log(df / np.float32(max_exact)) / np.float32(math.log(MAX_DISTANCE / max_exact))
                         * np.float32(NUM_BUCKETS - max_exact)).astype(np.int32)
    return np.where(dist < max_exact, dist, np.minimum(large, NUM_BUCKETS - 1)).astype(np.int32)


def _prompt_bucket_index():
    a = np.arange(Q_BLK)[:, None]
    c = np.arange(2 * Q_BLK)[None, :]
    step = Q_BLK + a - c
    valid = (step >= 0) & (step <= Q_BLK)
    out = []
    for _, d in BRANCHES:
        out.append(np.where(valid, _t5_bucket_np(np.clip(step, 0, Q_BLK) * d), -1))
    return np.stack(out).astype(np.int32)


def _sample_bucket_index(n_new, n_past):
    t = np.arange(n_new)[:, None]
    out_c, out_n = [], []
    for w, d in BRANCHES:
        dist_c = n_past + t - np.arange(n_past)[None, :]
        dist_n = t - np.arange(n_new)[None, :]
        for dist, out in ((dist_c, out_c), (dist_n, out_n)):
            valid = (dist >= 0) & (dist <= w) & (dist % d == 0)
            out.append(np.where(valid, _t5_bucket_np(np.clip(dist, 0, w)), -1))
    return np.stack(out_c).astype(np.int32), np.stack(out_n).astype(np.int32)


def _bias_body(rb_ref, idx_ref, out_ref):
    idx = idx_ref[0]
    for h in range(N_HEADS):
        acc = jnp.where(idx < 0, NEG_INF, 0.0).astype(F32)
        for b in range(NUM_BUCKETS):
            acc = acc + jnp.where(idx == b, rb_ref[b * N_HEADS + h], 0.0)
        out_ref[0, h] = acc


def _bias_expand(rel_bias, idx_np, name):
    nbr, r, c = idx_np.shape
    return pl.pallas_call(
        _bias_body,
        grid=(nbr,),
        in_specs=[pl.BlockSpec(memory_space=pltpu.SMEM),
                  pl.BlockSpec((1, r, c), lambda i: (i, 0, 0))],
        out_specs=pl.BlockSpec((1, N_HEADS, r, c), lambda i: (i, 0, 0, 0)),
        out_shape=jax.ShapeDtypeStruct((nbr, N_HEADS, r, c), F32),
        name=name,
    )(rel_bias.reshape(-1), jnp.asarray(idx_np))


def _bias_tables(rel_bias, n_new, n_past):
    bias_p = _bias_expand(rel_bias, _prompt_bucket_index(), "bias_prompt")
    bias_p = bias_p.reshape(len(BRANCHES), N_PAIRS, 2 * Q_BLK, 2 * Q_BLK)
    ic, inw = _sample_bucket_index(n_new, n_past)
    return bias_p, _bias_expand(rel_bias, ic, "bias_cache"), _bias_expand(rel_bias, inw, "bias_new")


def _ada_body(c_ref, w_ref, b_ref, o_ref):
    c = c_ref[...]
    s = c * (1.0 / (1.0 + jnp.exp(-c)))
    o_ref[...] = jnp.dot(s.astype(BF16), w_ref[...].astype(BF16), preferred_element_type=F32) + b_ref[...]


def _ada(c_all, w_ada, b_ada):
    n, dm = c_all.shape
    n_out = w_ada.shape[1]
    tn = dm
    return pl.pallas_call(
        _ada_body,
        grid=(n_out // tn,),
        in_specs=[pl.BlockSpec((n, dm), lambda j: (0, 0)),
                  pl.BlockSpec((dm, tn), lambda j: (0, j)),
                  pl.BlockSpec((1, tn), lambda j: (0, j))],
        out_specs=pl.BlockSpec((n, tn), lambda j: (0, j)),
        out_shape=jax.ShapeDtypeStruct((n, n_out), F32),
        compiler_params=_cparams(("parallel",), 32),
        name="ada",
    )(c_all, w_ada, b_ada.reshape(1, n_out))


_SHIFT1, _SCALE1, _GATE1, _SHIFT2, _SCALE2, _GATE2 = range(6)


def _inproj_body(*refs, nb, tm, carry):
    if carry:
        (x_ref, sh_ref, sc_ref, w_ref, cw_ref, ng_ref, past_ref,
         q_ref, k_ref, v_ref, conv_ref, cst_ref, kt_ref, vt_ref, zz_ref) = refs
    else:
        (x_ref, sh_ref, sc_ref, w_ref, cw_ref, ng_ref, past_ref,
         q_ref, k_ref, v_ref, conv_ref, cst_ref, zz_ref) = refs
    dm = x_ref.shape[-1]
    rows = nb * tm
    u = x_ref[...] * (1.0 + sc_ref[...]) + sh_ref[...]
    u = u.reshape(rows, dm).astype(BF16)

    def proj(lo, width):
        return jnp.dot(u, w_ref[:, lo:lo + width], preferred_element_type=F32)

    q_ref[...] = proj(0, D_ATTN).reshape(nb, tm, D_ATTN)
    k = proj(D_ATTN, D_ATTN)
    v = proj(2 * D_ATTN, D_ATTN)
    k_ref[...] = k.reshape(nb, tm, D_ATTN)
    v_ref[...] = v.reshape(nb, tm, D_ATTN)
    if carry:
        kt_ref[0] = k.T
        vt_ref[0] = v.T
    gb = proj(3 * D_ATTN, D_CONV)
    gc = proj(3 * D_ATTN + D_CONV, D_CONV)
    hh = proj(3 * D_ATTN + 2 * D_CONV, D_CONV)
    z = (gc * hh).reshape(nb, tm, D_CONV)

    if carry:
        s = pl.program_id(1)

        @pl.when(s == 0)
        def _():
            zz_ref[:, 0:V7X_SUBLANES, :] = jnp.zeros((nb, V7X_SUBLANES, D_CONV), F32)

        @pl.when(s > 0)
        def _():
            zz_ref[:, 0:V7X_SUBLANES, :] = zz_ref[:, tm:tm + V7X_SUBLANES, :]
    else:
        zz_ref[:, V7X_SUBLANES - 2:V7X_SUBLANES, :] = past_ref[...]
    zz_ref[:, V7X_SUBLANES:, :] = z

    cw = cw_ref[...]
    yc = (cw[0:1, :] * zz_ref[:, V7X_SUBLANES - 2:V7X_SUBLANES - 2 + tm, :]
          + cw[1:2, :] * zz_ref[:, V7X_SUBLANES - 1:V7X_SUBLANES - 1 + tm, :]
          + cw[2:3, :] * z)
    g = gb.reshape(nb, tm, D_CONV) * yc
    ms = jnp.mean(g * g, axis=-1, keepdims=True)
    conv_ref[...] = g * lax.rsqrt(ms + RMS_EPS) * ng_ref[...]
    cst_ref[...] = zz_ref[:, tm + V7X_SUBLANES - 2:tm + V7X_SUBLANES, :]


def _inproj(x, ada3, w_in_bf, conv_w, norm_conv_g, past, *, nb, tm):
    bsz, seq, dm = x.shape
    carry = past is None
    if carry:
        past = jnp.zeros((bsz, CONV_WIDTH - 1, D_CONV), F32)
    grid = (bsz // nb, seq // tm)
    d_in = w_in_bf.shape[1]
    body = functools.partial(_inproj_body, nb=nb, tm=tm, carry=carry)
    tile = lambda width: pl.BlockSpec((nb, tm, width), lambda b, s: (b, s, 0))
    in_specs = [tile(dm),
                pl.BlockSpec((nb, 1, dm), lambda b, s: (b, 0, _SHIFT1)),
                pl.BlockSpec((nb, 1, dm), lambda b, s: (b, 0, _SCALE1)),
                pl.BlockSpec((dm, d_in), lambda b, s: (0, 0)),
                pl.BlockSpec((CONV_WIDTH, D_CONV), lambda b, s: (0, 0)),
                pl.BlockSpec((1, D_CONV), lambda b, s: (0, 0)),
                pl.BlockSpec((nb, CONV_WIDTH - 1, D_CONV), lambda b, s: (b, 0, 0))]
    args = [x, ada3, ada3, w_in_bf, conv_w, norm_conv_g.reshape(1, D_CONV), past]
    out_specs = [tile(D_ATTN), tile(D_ATTN), tile(D_ATTN), tile(D_CONV),
                 pl.BlockSpec((nb, CONV_WIDTH - 1, D_CONV), lambda b, s: (b, 0, 0))]
    out_shape = ([jax.ShapeDtypeStruct((bsz, seq, D_ATTN), F32)] * 3
                 + [jax.ShapeDtypeStruct((bsz, seq, D_CONV), F32),
                    jax.ShapeDtypeStruct((bsz, CONV_WIDTH - 1, D_CONV), F32)])
    if carry:
        assert nb == 1
        out_specs += [pl.BlockSpec((1, D_ATTN, tm), lambda b, s: (b, 0, s))] * 2
        out_shape += [jax.ShapeDtypeStruct((bsz, D_ATTN, seq), F32)] * 2
    return pl.pallas_call(
        body,
        grid=grid,
        in_specs=in_specs,
        out_specs=out_specs,
        out_shape=out_shape,
        scratch_shapes=[pltpu.VMEM((nb, tm + V7X_SUBLANES, D_CONV), F32)],
        compiler_params=_cparams(("parallel", "arbitrary"), 56),
        name="inproj_prompt" if carry else "inproj_sample",
    )(*args)


def _attn_prompt_body(q_ref, k_ref, v_ref, bias_ref, o_ref, ob_ref, lb_ref, *, seq):
    scale = HEAD_DIM ** -0.5
    lane = lax.broadcasted_iota(I32, (Q_BLK, PAIR), 1)
    even = lane < HEAD_DIM
    ones = jnp.ones((Q_BLK, PAIR), BF16)

    def rows(ref, start, d):
        if d == 1:
            return ref[pl.ds(start, Q_BLK), :]
        return ref[pl.ds(start, Q_BLK, stride=d), :]

    def store(ref, br, start, d, val):
        if d == 1:
            ref[br, pl.ds(start, Q_BLK), :] = val
        else:
            ref[br, pl.ds(start, Q_BLK, stride=d), :] = val

    for br, (_, d) in enumerate(BRANCHES):
        nblk = seq // (d * Q_BLK)

        def block(it, carry, br=br, d=d, nblk=nblk):
            bias = bias_ref[br, 0]
            r = it // nblk
            i = it - r * nblk
            cur0 = r + d * Q_BLK * i
            qf = rows(q_ref, cur0, d) * scale
            qs = jnp.concatenate([jnp.where(even, qf, 0.0), jnp.where(even, 0.0, qf)], axis=0).astype(BF16)
            kc = rows(k_ref, cur0, d).astype(BF16)
            vc = rows(v_ref, cur0, d).astype(BF16)
            if nblk == 1:
                s = lax.dot_general(qs, kc, (((1,), (1,)), ((), ())), preferred_element_type=F32)
                s = s + bias[:, Q_BLK:]
                vaug = jnp.concatenate([vc, ones], axis=1)
            else:
                prev0 = r + d * Q_BLK * jnp.maximum(i - 1, 0)
                kp = rows(k_ref, prev0, d).astype(BF16)
                vp = rows(v_ref, prev0, d).astype(BF16)
                kk = jnp.concatenate([kp, kc], axis=0)
                s = lax.dot_general(qs, kk, (((1,), (1,)), ((), ())), preferred_element_type=F32)
                col = lax.broadcasted_iota(I32, (1, 2 * Q_BLK), 1)
                first = jnp.where((col < Q_BLK) & (i == 0), NEG_INF, 0.0)
                s = s + bias + first
                vaug = jnp.concatenate([jnp.concatenate([vp, vc], axis=0),
                                        jnp.concatenate([ones, ones], axis=0)], axis=1)
            m = jnp.max(s, axis=-1, keepdims=True)
            p = jnp.exp(s - m).astype(BF16)
            out = jnp.dot(p, vaug, preferred_element_type=F32)
            num, den = out[:, :PAIR], out[:, PAIR:]
            o = num / den
            lse = m + jnp.log(den)
            store(ob_ref, br, cur0, d, jnp.where(even, o[:Q_BLK], o[Q_BLK:]))
            store(lb_ref, br, cur0, d, jnp.where(even, lse[:Q_BLK], lse[Q_BLK:]))
            return carry

        lax.fori_loop(0, d * nblk, block, 0, unroll=ATTN_UNROLL)

    mt = 2 * Q_BLK

    def merge(i, carry):
        sl = pl.ds(pl.multiple_of(i * mt, mt), mt)
        l0, l1, l2 = lb_ref[0, sl, :], lb_ref[1, sl, :], lb_ref[2, sl, :]
        mx = jnp.maximum(jnp.maximum(l0, l1), l2)
        w0, w1, w2 = jnp.exp(l0 - mx), jnp.exp(l1 - mx), jnp.exp(l2 - mx)
        acc = w0 * ob_ref[0, sl, :] + w1 * ob_ref[1, sl, :] + w2 * ob_ref[2, sl, :]
        o_ref[sl, :] = acc / (w0 + w1 + w2)
        return carry

    lax.fori_loop(0, seq // mt, merge, 0)


def _attn_prompt(q, k, v, bias_p):
    bsz, seq, _ = q.shape
    assert seq % (BRANCHES[-1][1] * Q_BLK) == 0
    body = functools.partial(_attn_prompt_body, seq=seq)
    col = pl.BlockSpec((None, seq, PAIR), lambda b, j: (b, 0, j))
    return pl.pallas_call(
        body,
        grid=(bsz, N_PAIRS),
        in_specs=[col, col, col,
                  pl.BlockSpec((len(BRANCHES), 1, 2 * Q_BLK, 2 * Q_BLK), lambda b, j: (0, j, 0, 0))],
        out_specs=col,
        out_shape=jax.ShapeDtypeStruct((bsz, seq, D_ATTN), F32),
        scratch_shapes=[pltpu.VMEM((len(BRANCHES), seq, PAIR), F32),
                        pltpu.VMEM((len(BRANCHES), seq, PAIR), F32)],
        compiler_params=_cparams(("parallel", "parallel"), 48),
        name="attn_prompt",
    )(q, k, v, bias_p)


def _attn_sample_body(q_ref, kn_ref, vn_ref, ckt_ref, cvt_ref, bc_ref, bn_ref, o_ref):
    scale = HEAD_DIM ** -0.5
    nt = (((1,), (1,)), ((), ()))
    nbr = len(BRANCHES)
    for h in range(N_HEADS):
        rows = slice(h * HEAD_DIM, (h + 1) * HEAD_DIM)
        qh = (q_ref[h] * scale).astype(BF16)
        s_c = jnp.dot(qh, ckt_ref[rows, :].astype(BF16), preferred_element_type=F32)
        s_n = lax.dot_general(qh, kn_ref[h].astype(BF16), nt, preferred_element_type=F32)
        ps, lses, dens = [], [], []
        for br in range(nbr):
            lc = s_c + bc_ref[br, h]
            ln = s_n + bn_ref[br, h]
            m = jnp.maximum(jnp.max(lc, axis=-1, keepdims=True), jnp.max(ln, axis=-1, keepdims=True))
            pc = jnp.exp(lc - m)
            pn = jnp.exp(ln - m)
            den = jnp.sum(pc, axis=-1, keepdims=True) + jnp.sum(pn, axis=-1, keepdims=True)
            ps.append((pc, pn))
            dens.append(den)
            lses.append(m + jnp.log(den))
        mx = jnp.maximum(jnp.maximum(lses[0], lses[1]), lses[2])
        ws = [jnp.exp(l - mx) for l in lses]
        wsum = ws[0] + ws[1] + ws[2]
        coefs = [ws[br] / (wsum * dens[br]) for br in range(nbr)]
        p_c = coefs[0] * ps[0][0] + coefs[1] * ps[1][0] + coefs[2] * ps[2][0]
        p_n = coefs[0] * ps[0][1] + coefs[1] * ps[1][1] + coefs[2] * ps[2][1]
        o = lax.dot_general(p_c.astype(BF16), cvt_ref[rows, :].astype(BF16), nt, preferred_element_type=F32)
        o_ref[h] = o + jnp.dot(p_n.astype(BF16), vn_ref[h].astype(BF16), preferred_element_type=F32)


def _attn_sample(q, k_new, v_new, cache_kt, cache_vt, bias_c, bias_n):
    bsz, _, n_new, _ = q.shape
    n_past = cache_kt.shape[2]
    new = pl.BlockSpec((None, N_HEADS, n_new, HEAD_DIM), lambda b: (b, 0, 0, 0))
    cache = pl.BlockSpec((None, D_ATTN, n_past), lambda b: (b, 0, 0))
    return pl.pallas_call(
        _attn_sample_body,
        grid=(bsz,),
        in_specs=[new, new, new, cache, cache,
                  pl.BlockSpec(bias_c.shape, lambda b: (0, 0, 0, 0)), pl.BlockSpec(bias_n.shape, lambda b: (0, 0, 0, 0))],
        out_specs=new,
        out_shape=jax.ShapeDtypeStruct((bsz, N_HEADS, n_new, HEAD_DIM), F32),
        compiler_params=_cparams(("parallel",), 48),
        name="attn_sample",
    )(q, k_new, v_new, cache_kt, cache_vt, bias_c, bias_n)


def _layer_norm(y, g, b):
    mu = jnp.mean(y, axis=-1, keepdims=True)
    c = y - mu
    var = jnp.mean(c * c, axis=-1, keepdims=True)
    return c * lax.rsqrt(var + LN_EPS) * g + b


def _to_row_tiles(ref, val):
    rows, dm = val.shape
    assert dm == V7X_SUBLANES * V7X_LANES
    for c in range(V7X_SUBLANES):
        ref[pl.ds(c, rows, stride=V7X_SUBLANES), :] = val[:, c * V7X_LANES:(c + 1) * V7X_LANES]


def _from_row_tiles(ref, rows, lead=()):
    chunks = [ref[lead + (pl.ds(c, rows, stride=V7X_SUBLANES), slice(None))] for c in range(V7X_SUBLANES)]
    return jnp.concatenate(chunks, axis=1)


def _outproj_body(*refs, nb, tm, alpha, aliased, n_main):
    if aliased:
        refs = refs[2:]
    (a_ref, c_ref, x_ref, g1_ref, sh2_ref, sc2_ref, wa_ref, wc_ref, ng_ref, lg_ref, lb_ref, x1_ref, u2_ref) = refs
    rows = nb * tm

    @pl.when(pl.program_id(0) < n_main)
    def _():
        a = a_ref[...]
        ms = jnp.mean(a * a, axis=-1, keepdims=True)
        an = (a * lax.rsqrt(ms + RMS_EPS) * ng_ref[...]).reshape(rows, D_ATTN).astype(BF16)
        cn = c_ref[...].reshape(rows, D_CONV).astype(BF16)
        mix = (jnp.dot(an, wa_ref[...], preferred_element_type=F32)
               + jnp.dot(cn, wc_ref[...], preferred_element_type=F32))
        dm = mix.shape[-1]
        y = alpha * x_ref[...] + (1.0 + g1_ref[...]) * mix.reshape(nb, tm, dm)
        x1 = _layer_norm(y, lg_ref[...], lb_ref[...])
        x1_ref[...] = x1.reshape(rows, dm)
        u2_ref[...] = (x1 * (1.0 + sc2_ref[...]) + sh2_ref[...]).reshape(rows, dm)

    @pl.when(pl.program_id(0) >= n_main)
    def _():
        x1_ref[...] = jnp.zeros_like(x1_ref)
        u2_ref[...] = jnp.zeros_like(u2_ref)


def _outproj(attn, conv, x, ada3, w_o_bf, norm_attn_g, ln1_g, ln1_b, *, nb, tm, alpha, n_total, row0, prev=None):
    bsz, seq, dm = x.shape
    rows = nb * tm
    assert row0 % rows == 0
    st = seq // tm
    blk0 = row0 // rows
    n_main = (bsz // nb) * st
    aliased = prev is not None
    n_steps = n_main if aliased else pl.cdiv(n_total, rows)

    def bs_of(i):
        j = jnp.minimum(i, n_main - 1)
        return j // st, j % st

    tile = lambda width: pl.BlockSpec((nb, tm, width), lambda i: bs_of(i) + (0,))
    ada = lambda chunk: pl.BlockSpec((nb, 1, dm), lambda i: (bs_of(i)[0], 0, chunk))
    const = lambda shape: pl.BlockSpec(shape, lambda i: (0,) * len(shape))
    out = pl.BlockSpec((rows, dm), lambda i: (blk0 + i, 0))
    body = functools.partial(_outproj_body, nb=nb, tm=tm, alpha=alpha, aliased=aliased, n_main=n_main)
    in_specs = [tile(D_ATTN), tile(D_CONV), tile(dm), ada(_GATE1), ada(_SHIFT2), ada(_SCALE2),
                const((D_ATTN, dm)), const((D_CONV, dm)), const((1, D_ATTN)), const((1, dm)), const((1, dm))]
    args = [attn, conv, x, ada3, ada3, ada3, w_o_bf[:D_ATTN], w_o_bf[D_ATTN:],
            norm_attn_g.reshape(1, D_ATTN), ln1_g.reshape(1, dm), ln1_b.reshape(1, dm)]
    kwargs = {}
    if aliased:
        in_specs = [pl.BlockSpec(memory_space=pl.ANY)] * 2 + in_specs
        args = list(prev) + args
        kwargs["input_output_aliases"] = {0: 0, 1: 1}
    return pl.pallas_call(
        body,
        grid=(n_steps,),
        in_specs=in_specs,
        out_specs=[out, out],
        out_shape=[jax.ShapeDtypeStruct((n_total, dm), F32)] * 2,
        compiler_params=_cparams(("parallel",), 48),
        name="outproj_sample" if aliased else "outproj_prompt",
        **kwargs,
    )(*args)


def _router_body(u_ref, w_ref, b_ref, ri_ref, rr_ref, rg_ref, cnt_ref, run_ref, *, tm):
    i = pl.program_id(0)

    @pl.when(i == 0)
    def _():
        run_ref[...] = jnp.zeros_like(run_ref)

    u = u_ref[...]
    u_hi = u.astype(BF16)
    u_lo = (u - u_hi.astype(F32)).astype(BF16)
    both = jnp.dot(u_hi, w_ref[...], preferred_element_type=F32)
    cross = jnp.dot(u_lo, w_ref[:, :V7X_LANES], preferred_element_type=F32)
    logits = both[:, :V7X_LANES] + both[:, V7X_LANES:] + cross + b_ref[...]
    lane_i = lax.broadcasted_iota(I32, (tm, V7X_LANES), 1)
    lane = lane_i.astype(F32)
    vals = logits
    tops, idxs = [], []
    for _ in range(TOP_K):
        mk = jnp.max(vals, axis=-1, keepdims=True)
        ik = jnp.min(jnp.where(vals == mk, lane, float(V7X_LANES)), axis=-1, keepdims=True)
        tops.append(mk)
        idxs.append(ik)
        vals = jnp.where(lane == ik, -jnp.inf, vals)
    es = [jnp.exp(t - tops[0]) for t in tops]
    den = es[0] + es[1] + es[2] + es[3]
    sel = jnp.zeros((tm, V7X_LANES), F32)
    for ik in idxs:
        sel = sel + jnp.where(lane == ik, 1.0, 0.0)
    ri = lax.broadcasted_iota(I32, (tm, tm), 0)
    ci = lax.broadcasted_iota(I32, (tm, tm), 1)
    lower = jnp.where(ci < ri, 1.0, 0.0).astype(BF16)
    before = jnp.dot(lower, sel.astype(BF16), preferred_element_type=F32) + run_ref[...]
    out_i = jnp.full((tm, V7X_LANES), -1, I32)
    out_r = jnp.zeros((tm, V7X_LANES), I32)
    out_g = jnp.zeros((tm, V7X_LANES), F32)
    for kk in range(TOP_K):
        rank = jnp.sum(jnp.where(lane == idxs[kk], before, 0.0), axis=-1, keepdims=True)
        out_i = jnp.where(lane_i == kk, idxs[kk].astype(I32), out_i)
        out_r = jnp.where(lane_i == kk, rank.astype(I32), out_r)
        out_g = jnp.where(lane_i == kk, es[kk] / den, out_g)
    ri_ref[...] = out_i
    rr_ref[...] = out_r
    rg_ref[...] = out_g
    run_ref[...] = run_ref[...] + jnp.sum(sel, axis=0, keepdims=True)
    cnt_ref[...] = run_ref[...]


def _router(u2, router_w, router_b, *, tm):
    n, dm = u2.shape
    w_pad = jnp.pad(router_w, ((0, 0), (0, V7X_LANES - N_EXPERTS)))
    w_hi = w_pad.astype(BF16)
    w_lo = (w_pad - w_hi.astype(F32)).astype(BF16)
    b_pad = jnp.pad(router_b, (0, V7X_LANES - N_EXPERTS), constant_values=NEG_INF).reshape(1, V7X_LANES)
    tok = pl.BlockSpec((tm, V7X_LANES), lambda i: (i, 0))
    return pl.pallas_call(
        functools.partial(_router_body, tm=tm),
        grid=(n // tm,),
        in_specs=[pl.BlockSpec((tm, dm), lambda i: (i, 0)),
                  pl.BlockSpec((dm, 2 * V7X_LANES), lambda i: (0, 0)),
                  pl.BlockSpec((1, V7X_LANES), lambda i: (0, 0))],
        out_specs=[tok, tok, tok, pl.BlockSpec((1, V7X_LANES), lambda i: (0, 0))],
        out_shape=[jax.ShapeDtypeStruct((n, V7X_LANES), I32), jax.ShapeDtypeStruct((n, V7X_LANES), I32),
                   jax.ShapeDtypeStruct((n, V7X_LANES), F32), jax.ShapeDtypeStruct((1, V7X_LANES), F32)],
        scratch_shapes=[pltpu.VMEM((1, V7X_LANES), F32)],
        compiler_params=_cparams(("arbitrary",), 32),
        name="router",
    )(u2, jnp.concatenate([w_hi, w_lo], axis=1), b_pad)


def _dest_body(ps_ref, ri_ref, rr_ref, d_ref):
    idx = ri_ref[...]
    acc = rr_ref[...]
    for e in range(N_EXPERTS):
        acc = acc + jnp.where(idx == e, ps_ref[e], 0)
    d_ref[...] = acc


def _dest_rows(expert, rank, pad_starts):
    return pl.pallas_call(
        _dest_body,
        in_specs=[pl.BlockSpec(memory_space=pltpu.SMEM), pl.BlockSpec(memory_space=pltpu.VMEM),
                  pl.BlockSpec(memory_space=pltpu.VMEM)],
        out_specs=pl.BlockSpec(memory_space=pltpu.VMEM),
        out_shape=jax.ShapeDtypeStruct(expert.shape, I32),
        name="dest_rows",
    )(pad_starts, expert, rank)


ISSUE_TOKENS = 2


def _row_tile(ref, row):
    return ref.at[pl.ds(pl.multiple_of(row * V7X_SUBLANES, V7X_SUBLANES), V7X_SUBLANES), :]


def _issue_rows(dest_ref, n_tok, start_copy):
    def trip(it, carry):
        base = it * ISSUE_TOKENS
        rows = [dest_ref[0, 0, (base + u) * TOP_K + kk] for u in range(ISSUE_TOKENS) for kk in range(TOP_K)]
        for u in range(ISSUE_TOKENS):
            for kk in range(TOP_K):
                start_copy(base + u, kk, rows[u * TOP_K + kk])
        return carry

    lax.fori_loop(0, n_tok // ISSUE_TOKENS, trip, 0)


DISPATCH_SLOTS = 3


FILL_UNROLL = 8


def _dispatch_body(fr_ref, dest_ref, u_ref, xs_out, stage_ref, zero_ref, sems, fill_sem, *, tm, n_fill):
    i = pl.program_id(0)
    last = pl.num_programs(0) - 1
    slot = i % DISPATCH_SLOTS

    def wait_step(s):
        for _ in range(TOP_K):
            pltpu.make_async_copy(stage_ref.at[s], xs_out.at[pl.ds(0, tm * V7X_SUBLANES), :], sems.at[s]).wait()

    @pl.when(i == 0)
    def _():
        zero_ref[...] = jnp.zeros_like(zero_ref)

        def trip(it, carry):
            rows = [fr_ref[it * FILL_UNROLL + u] for u in range(FILL_UNROLL)]
            for u, row in enumerate(rows):
                pltpu.make_async_copy(zero_ref, _row_tile(xs_out, row), fill_sem).start(priority=u % DMA_QUEUES)
            return carry

        lax.fori_loop(0, n_fill // FILL_UNROLL, trip, 0)

    @pl.when(i >= DISPATCH_SLOTS - 1)
    def _():
        wait_step((i + 1) % DISPATCH_SLOTS)

    _to_row_tiles(stage_ref.at[slot], u_ref[...])

    def start_copy(r, kk, row):
        pltpu.make_async_copy(_row_tile(stage_ref.at[slot], r), _row_tile(xs_out, row),
                              sems.at[slot]).start(priority=kk % DMA_QUEUES)

    _issue_rows(dest_ref, tm, start_copy)

    @pl.when(i == last)
    def _():
        for back in range(DISPATCH_SLOTS - 1):
            @pl.when(i >= back)
            def _():
                wait_step((i - back) % DISPATCH_SLOTS)
        pltpu.make_async_copy(xs_out.at[pl.ds(0, n_fill * V7X_SUBLANES), :],
                              xs_out.at[pl.ds(0, n_fill * V7X_SUBLANES), :], fill_sem).wait()


def _dispatch(u2, dest_sm, fill_rows, n_rows, *, tm):
    n, dm = u2.shape
    n_fill = fill_rows.shape[0]
    assert n_fill == n_rows - n * TOP_K and n_fill % FILL_UNROLL == 0
    grid_spec = pltpu.PrefetchScalarGridSpec(
        num_scalar_prefetch=1,
        grid=(n // tm,),
        in_specs=[pl.BlockSpec((1, 1, tm * TOP_K), lambda i, fr: (i, 0, 0), memory_space=pltpu.SMEM),
                  pl.BlockSpec((tm, dm), lambda i, fr: (i, 0))],
        out_specs=pl.BlockSpec(memory_space=pl.ANY),
        scratch_shapes=[pltpu.VMEM((DISPATCH_SLOTS, tm * V7X_SUBLANES, V7X_LANES), F32),
                        pltpu.VMEM((V7X_SUBLANES, V7X_LANES), F32),
                        pltpu.SemaphoreType.DMA((DISPATCH_SLOTS,)), pltpu.SemaphoreType.DMA(())],
    )
    return pl.pallas_call(
        functools.partial(_dispatch_body, tm=tm, n_fill=n_fill),
        grid_spec=grid_spec,
        out_shape=jax.ShapeDtypeStruct((n_rows * V7X_SUBLANES, V7X_LANES), F32),
        compiler_params=_cparams(("arbitrary",), 32),
        name="dispatch",
    )(fill_rows, dest_sm, u2)


def _gmm_body(be_ref, bv_ref, bf_ref, bs_ref, bn_ref, xs_ref, wu_hbm, bu_ref, wd_hbm, bd_ref, ys_ref,
              wu_f32, wd_f32, wu_bf, wd_bf, sems, *, bm):
    j = pl.program_id(0)
    e = be_ref[j]
    s = bs_ref[j]
    d_ff = wd_hbm.shape[1]
    chunk = 64

    def fetch(ex, slot):
        return (pltpu.make_async_copy(wu_hbm.at[ex], wu_f32.at[slot], sems.at[0, slot]),
                pltpu.make_async_copy(wd_hbm.at[ex], wd_f32.at[slot], sems.at[1, slot]))

    @pl.when(bf_ref[j] != 0)
    def _():
        @pl.when(j == 0)
        def _():
            for c in fetch(e, s):
                c.start()

        for c in fetch(e, s):
            c.wait()

        @pl.when(bn_ref[j] >= 0)
        def _():
            for c in fetch(bn_ref[j], 1 - s):
                c.start()

        def cast(c, carry):
            sl = pl.ds(pl.multiple_of(c * chunk, chunk), chunk)
            wu_bf[sl, :] = wu_f32[s, sl, :].astype(BF16)
            wd_bf[sl, :] = wd_f32[s, sl, :].astype(BF16)
            return carry

        lax.fori_loop(0, wu_hbm.shape[1] // chunk, cast, 0)

    @pl.when(bv_ref[j] != 0)
    def _():
        for g in range(bm // EXPERT_ROWS):
            tiles = pl.ds(g * EXPERT_ROWS * V7X_SUBLANES, EXPERT_ROWS * V7X_SUBLANES)
            x = _from_row_tiles(xs_ref.at[tiles, :], EXPERT_ROWS).astype(BF16)
            glu = jnp.dot(x, wu_bf[:, :d_ff], preferred_element_type=F32) + bu_ref[0, :, :d_ff]
            lin = jnp.dot(x, wu_bf[:, d_ff:], preferred_element_type=F32) + bu_ref[0, :, d_ff:]
            glu = jnp.minimum(glu, SWIGLU_LIMIT)
            lin = jnp.clip(lin, -SWIGLU_LIMIT, SWIGLU_LIMIT)
            act = glu * (1.0 / (1.0 + jnp.exp(-SWIGLU_ALPHA * glu))) * (lin + 1.0)
            _to_row_tiles(ys_ref.at[tiles, :],
                          jnp.dot(act.astype(BF16), wd_bf[...], preferred_element_type=F32) + bd_ref[0])

    @pl.when(bv_ref[j] == 0)
    def _():
        ys_ref[...] = jnp.zeros_like(ys_ref)


def _gmm(xs, blk_e, blk_valid, blk_first, blk_slot, blk_next, w_up, b_up, w_down, b_down, *, bm):
    n_rows = xs.shape[0] // V7X_SUBLANES
    n_e, dm, d_up = w_up.shape
    d_ff = w_down.shape[1]
    assert d_ff == dm
    row_tiles = pl.BlockSpec((bm * V7X_SUBLANES, V7X_LANES), lambda j, *_: (j, 0))
    grid_spec = pltpu.PrefetchScalarGridSpec(
        num_scalar_prefetch=5,
        grid=(n_rows // bm,),
        in_specs=[row_tiles,
                  pl.BlockSpec(memory_space=pl.ANY),
                  pl.BlockSpec((1, 1, d_up), lambda j, be, *_: (be[j], 0, 0)),
                  pl.BlockSpec(memory_space=pl.ANY),
                  pl.BlockSpec((1, 1, dm), lambda j, be, *_: (be[j], 0, 0))],
        out_specs=row_tiles,
        scratch_shapes=[pltpu.VMEM((2, dm, d_up), F32), pltpu.VMEM((2, d_ff, dm), F32),
                        pltpu.VMEM((dm, d_up), BF16), pltpu.VMEM((d_ff, dm), BF16),
                        pltpu.SemaphoreType.DMA((2, 2))],
    )
    return pl.pallas_call(
        functools.partial(_gmm_body, bm=bm),
        grid_spec=grid_spec,
        out_shape=jax.ShapeDtypeStruct(xs.shape, F32),
        compiler_params=_cparams(("arbitrary",), 56),
        name="expert_mlp",
    )(blk_e, blk_valid, blk_first, blk_slot, blk_next, xs, w_up, b_up.reshape(n_e, 1, d_up), w_down,
      b_down.reshape(n_e, 1, dm))


def _combine_body(rt_ref, rtn_ref, ys_hbm, rg_ref, x1_ref, g2_ref, lg_ref, lb_ref, y_ref, rows_ref, sems,
                  *, nb, tm, alpha, n_steps):
    n_tok = nb * tm
    i = pl.program_id(0)
    slot = i % 2

    def issue(dest_ref, s):
        def start_copy(r, kk, row):
            pltpu.make_async_copy(_row_tile(ys_hbm, row), _row_tile(rows_ref.at[s, kk], r),
                                  sems.at[s]).start(priority=kk % DMA_QUEUES)

        _issue_rows(dest_ref, n_tok, start_copy)

    @pl.when(i == 0)
    def _():
        issue(rt_ref, 0)

    @pl.when(i + 1 < n_steps)
    def _():
        issue(rtn_ref, 1 - slot)

    for kk in range(TOP_K):
        pltpu.make_async_copy(ys_hbm.at[pl.ds(0, n_tok * V7X_SUBLANES), :], rows_ref.at[slot, kk],
                              sems.at[slot]).wait()

    gates = rg_ref[...]
    ffn = gates[:, 0:1] * _from_row_tiles(rows_ref, n_tok, (slot, 0))
    for kk in range(1, TOP_K):
        ffn = ffn + gates[:, kk:kk + 1] * _from_row_tiles(rows_ref, n_tok, (slot, kk))
    dm = ffn.shape[-1]
    y = alpha * x1_ref[...].reshape(nb, tm, dm) + (1.0 + g2_ref[...]) * ffn.reshape(nb, tm, dm)
    y_ref[...] = _layer_norm(y, lg_ref[...], lb_ref[...])


def _combine(ys, dest_sm, route_g, x1_all, ada3, ln2_g, ln2_b, *, bsz, seq, nb, tm, alpha, row0):
    dm = x1_all.shape[1]
    n_tok = nb * tm
    assert row0 % n_tok == 0 and dest_sm.shape[2] == n_tok * TOP_K and n_tok % ISSUE_TOKENS == 0
    blk0 = row0 // n_tok
    st = seq // tm
    n_steps = (bsz // nb) * st
    dest = lambda off: pl.BlockSpec((1, 1, n_tok * TOP_K),
                                    lambda i: (blk0 + jnp.minimum(i + off, n_steps - 1), 0, 0),
                                    memory_space=pltpu.SMEM)
    return pl.pallas_call(
        functools.partial(_combine_body, nb=nb, tm=tm, alpha=alpha, n_steps=n_steps),
        grid=(n_steps,),
        in_specs=[dest(0), dest(1),
                  pl.BlockSpec(memory_space=pl.ANY),
                  pl.BlockSpec((n_tok, V7X_LANES), lambda i: (blk0 + i, 0)),
                  pl.BlockSpec((n_tok, dm), lambda i: (blk0 + i, 0)),
                  pl.BlockSpec((nb, 1, dm), lambda i: (i // st, 0, _GATE2)),
                  pl.BlockSpec((1, dm), lambda i: (0, 0)),
                  pl.BlockSpec((1, dm), lambda i: (0, 0))],
        out_specs=pl.BlockSpec((nb, tm, dm), lambda i: (i // st, i % st, 0)),
        scratch_shapes=[pltpu.VMEM((2, TOP_K, n_tok * V7X_SUBLANES, V7X_LANES), F32),
                        pltpu.SemaphoreType.DMA((2,))],
        out_shape=jax.ShapeDtypeStruct((bsz, seq, dm), F32),
        compiler_params=_cparams(("arbitrary",), 48),
        name="combine_prompt" if row0 == 0 else "combine_sample",
    )(dest_sm, dest_sm, ys, route_g, x1_all, ada3, ln2_g.reshape(1, dm), ln2_b.reshape(1, dm))


def _layer(xp, xs, cache_k, cache_v, conv_past, cp, cs, w_ada, b_ada, w_in, conv_w, norm_attn_g, norm_conv_g,
           w_o, ln1_g, ln1_b, bias_tabs, router_w, router_b, w_up, b_up, w_down, b_down, ln2_g, ln2_b, alpha):
    bp, sp, dm = xp.shape
    bs, ts, _ = xs.shape
    n_p, n_s = bp * sp, bs * ts
    n_tok = n_p + n_s
    assert n_p % TOK_TILE == 0 and n_s == TOK_TILE and sp % ROW_TILE == 0 and sp % OUT_TILE == 0
    bias_p, bias_c, bias_n = bias_tabs

    ada = _ada(jnp.concatenate([cp, cs], axis=0), w_ada, b_ada)
    ada_p = ada[:bp].reshape(bp, 1, -1)
    ada_s = ada[bp:].reshape(bs, 1, -1)
    w_in_bf = w_in.astype(BF16)
    w_o_bf = w_o.astype(BF16)

    qp, kp, vp, convp, cstp, kpt, vpt = _inproj(xp, ada_p, w_in_bf, conv_w, norm_conv_g, None, nb=1, tm=ROW_TILE)
    attn_p = _attn_prompt(qp, kp, vp, bias_p)
    x1_all, u2_all = _outproj(attn_p, convp, xp, ada_p, w_o_bf, norm_attn_g, ln1_g, ln1_b,
                              nb=1, tm=OUT_TILE, alpha=alpha, n_total=n_tok, row0=0)
    qs, ks, vs, convs, csts = _inproj(xs, ada_s, w_in_bf, conv_w, norm_conv_g, conv_past, nb=bs, tm=ts)
    heads = lambda a: jnp.transpose(a.reshape(bs, ts, N_HEADS, HEAD_DIM), (0, 2, 1, 3))
    n_past = cache_k.shape[1]
    feat_major = lambda c: jnp.transpose(c.reshape(bs, n_past, D_ATTN), (0, 2, 1))
    attn_s = _attn_sample(heads(qs), heads(ks), heads(vs), feat_major(cache_k), feat_major(cache_v), bias_c, bias_n)
    attn_s = jnp.transpose(attn_s, (0, 2, 1, 3)).reshape(bs, ts, D_ATTN)
    x1_all, u2_all = _outproj(attn_s, convs, xs, ada_s, w_o_bf, norm_attn_g, ln1_g, ln1_b,
                              nb=bs, tm=ts, alpha=alpha, n_total=n_tok, row0=n_p, prev=(x1_all, u2_all))

    route_i, route_r, route_g, counts = _router(u2_all, router_w, router_b, tm=TOK_TILE)
    counts = counts[0, :N_EXPERTS].astype(I32)
    padded = ((counts + EXPERT_BLK - 1) // EXPERT_BLK) * EXPERT_BLK
    pad_ends = jnp.cumsum(padded)
    pad_starts = (pad_ends - padded).astype(I32)
    n_blocks = n_tok * TOP_K // EXPERT_BLK + N_EXPERTS
    n_rows = n_blocks * EXPERT_BLK
    blk_row = jnp.arange(n_blocks, dtype=I32) * EXPERT_BLK
    blk_valid = (blk_row < pad_ends[-1]).astype(I32)
    blk_e = jnp.sum((blk_row[:, None] >= pad_ends[None, :]).astype(I32), axis=1)
    last_e = jnp.sum((pad_ends[-1] - 1 >= pad_ends).astype(I32))
    blk_e = jnp.where(blk_valid != 0, blk_e, last_e).astype(I32)
    ar = jnp.arange(N_EXPERTS, dtype=I32)
    has_rows = padded > 0
    slot_of_e = (jnp.cumsum(has_rows.astype(I32)) - 1) % 2
    next_of_e = jnp.min(jnp.where((ar[None, :] > ar[:, None]) & has_rows[None, :], ar[None, :], N_EXPERTS), axis=1)
    next_of_e = jnp.where(next_of_e < N_EXPERTS, next_of_e, -1)
    is_e = blk_e[:, None] == ar[None, :]
    pick = lambda per_expert: jnp.sum(jnp.where(is_e, per_expert[None, :], 0), axis=1).astype(I32)
    blk_first = (blk_valid * (blk_row == pick(pad_starts)).astype(I32)).astype(I32)
    blk_slot = pick(slot_of_e)
    blk_next = pick(next_of_e)
    dense = lambda a: a[:, :TOP_K].reshape(n_tok * TOP_K // V7X_LANES, V7X_LANES)
    dest = _dest_rows(dense(route_i), dense(route_r), pad_starts)
    dest_sm = dest.reshape(n_tok // TOK_TILE, 1, TOK_TILE * TOP_K)

    run_start = jnp.concatenate([pad_starts + counts, pad_ends[-1:]])
    run_len = jnp.concatenate([padded - counts, n_rows - pad_ends[-1:]])
    run_k0 = jnp.cumsum(run_len) - run_len
    k = jnp.arange(n_rows - n_tok * TOP_K, dtype=I32)[:, None]
    in_run = (k >= run_k0[None, :]) & (k < (run_k0 + run_len)[None, :])
    fill_rows = jnp.sum(jnp.where(in_run, run_start[None, :] + k - run_k0[None, :], 0), axis=1).astype(I32)
    x_sorted = _dispatch(u2_all, dest_sm, fill_rows, n_rows, tm=TOK_TILE)
    y_sorted = _gmm(x_sorted, blk_e, blk_valid, blk_first, blk_slot, blk_next, w_up, b_up, w_down, b_down,
                    bm=EXPERT_BLK)
    yp = _combine(y_sorted, dest_sm, route_g, x1_all, ada_p, ln2_g, ln2_b,
                  bsz=bp, seq=sp, nb=1, tm=TOK_TILE, alpha=alpha, row0=0)
    ys_out = _combine(y_sorted, dest_sm, route_g, x1_all, ada_s, ln2_g, ln2_b,
                      bsz=bs, seq=ts, nb=bs, tm=ts, alpha=alpha, row0=n_p)
    return yp, ys_out, kpt, vpt, cstp, ks, vs, csts


def kernel(x_prompt, x_sample, cache_k, cache_v, state_conv, c_prompt, c_sample, w_ada, b_ada, w_in, conv_w,
           norm_attn_g, norm_conv_g, w_o, ln1_g, ln1_b, rel_bias, router_w, router_b, w_up, b_up, w_down, b_down,
           ln2_g, ln2_b):
    depth = w_ada.shape[0]
    alpha = (2 * depth) ** 0.25
    xp, xs = x_prompt, x_sample
    bp, sp, _ = xp.shape
    bs, ts, _ = xs.shape
    n_keep = min(BRANCHES[-1][0], sp)
    bias_tabs = _bias_tables(rel_bias, ts, cache_k.shape[2])
    outs = [[] for _ in range(6)]
    for l in range(depth):
        xp, xs, kp, vp, cstp, ks, vs, csts = _layer(
            xp, xs, cache_k[l], cache_v[l], state_conv[l], c_prompt, c_sample, w_ada[l], b_ada[l], w_in[l],
            conv_w[l], norm_attn_g[l], norm_conv_g[l], w_o[l], ln1_g[l], ln1_b[l], bias_tabs, router_w[l],
            router_b[l], w_up[l], b_up[l], w_down[l], b_down[l], ln2_g[l], ln2_b[l], alpha)
        kp = jnp.transpose(kp, (0, 2, 1)).reshape(bp, sp, N_HEADS, HEAD_DIM)[:, -n_keep:]
        vp = jnp.transpose(vp, (0, 2, 1)).reshape(bp, sp, N_HEADS, HEAD_DIM)[:, -n_keep:]
        for lst, val in zip(outs, (kp, vp, cstp, ks.reshape(bs, ts, N_HEADS, HEAD_DIM),
                                   vs.reshape(bs, ts, N_HEADS, HEAD_DIM), csts)):
            lst.append(val)
    return (xp, xs) + tuple(jnp.stack(o) for o in outs)
```

```python
import functools
import math

import numpy as np
import jax
import jax.numpy as jnp
from jax import lax
from jax.experimental import pallas as pl
from jax.experimental.pallas import tpu as pltpu

F32 = jnp.float32
BF16 = jnp.bfloat16
I32 = jnp.int32

HEAD_DIM = 64
N_HEADS = 12
D_ATTN = N_HEADS * HEAD_DIM
D_CONV = 256
CONV_WIDTH = 3
BRANCHES = ((128, 1), (512, 4), (2048, 16))
NUM_BUCKETS = 32
MAX_DISTANCE = 2048
N_EXPERTS = 32
TOP_K = 4
SWIGLU_LIMIT = 7.0
SWIGLU_ALPHA = 1.702
LN_EPS = 1e-5
RMS_EPS = 1e-6
NEG_INF = -1e30
LOG2E = math.log2(math.e)

V7X_LANES = 128
V7X_SUBLANES = 8
V7X_VMEM_BYTES = 64 * 1024 * 1024
DMA_QUEUES = 2

Q_BLK = 128
PAIR = 2 * HEAD_DIM
N_PAIRS = N_HEADS // 2
TOK_TILE = 256
ROW_TILE = 512
OUT_TILE = 512
EXPERT_BLK = 256
EXPERT_ROWS = 256
ATTN_UNROLL = 16


def _cparams(sem, vmem_mb):
    return pltpu.CompilerParams(dimension_semantics=sem, vmem_limit_bytes=vmem_mb * 1024 * 1024)


def _t5_bucket_np(dist):
    dist = np.asarray(dist, np.int64)
    max_exact = NUM_BUCKETS // 2
    df = np.maximum(dist, max_exact).astype(np.float32)
    large = max_exact + (np.log(df / np.float32(max_exact)) / np.float32(math.log(MAX_DISTANCE / max_exact))
                         * np.float32(NUM_BUCKETS - max_exact)).astype(np.int32)
    return np.where(dist < max_exact, dist, np.minimum(large, NUM_BUCKETS - 1)).astype(np.int32)


def _prompt_bucket_index():
    a = np.arange(Q_BLK)[:, None]
    c = np.arange(2 * Q_BLK)[None, :]
    step = Q_BLK + a - c
    valid = (step >= 0) & (step <= Q_BLK)
    out = []
    for _, d in BRANCHES:
        out.append(np.where(valid, _t5_bucket_np(np.clip(step, 0, Q_BLK) * d), -1))
    return np.stack(out).astype(np.int32)


def _sample_bucket_index(n_new, n_past):
    t = np.arange(n_new)[:, None]
    out_c, out_n = [], []
    for w, d in BRANCHES:
        dist_c = n_past + t - np.arange(n_past)[None, :]
        dist_n = t - np.arange(n_new)[None, :]
        for dist, out in ((dist_c, out_c), (dist_n, out_n)):
            valid = (dist >= 0) & (dist <= w) & (dist % d == 0)
            out.append(np.where(valid, _t5_bucket_np(np.clip(dist, 0, w)), -1))
    return np.stack(out_c).astype(np.int32), np.stack(out_n).astype(np.int32)


def _bias_body(rb_ref, idx_ref, out_ref, *, mult):
    idx = idx_ref[0]
    for h in range(N_HEADS):
        acc = jnp.where(idx < 0, NEG_INF, 0.0).astype(F32)
        for b in range(NUM_BUCKETS):
            acc = acc + jnp.where(idx == b, rb_ref[b * N_HEADS + h] * mult, 0.0)
        out_ref[0, h] = acc


def _bias_expand(rel_bias, idx_np, name, mult=1.0):
    nbr, r, c = idx_np.shape
    return pl.pallas_call(
        functools.partial(_bias_body, mult=mult),
        grid=(nbr,),
        in_specs=[pl.BlockSpec(memory_space=pltpu.SMEM),
                  pl.BlockSpec((1, r, c), lambda i: (i, 0, 0))],
        out_specs=pl.BlockSpec((1, N_HEADS, r, c), lambda i: (i, 0, 0, 0)),
        out_shape=jax.ShapeDtypeStruct((nbr, N_HEADS, r, c), F32),
        name=name,
    )(rel_bias.reshape(-1), jnp.asarray(idx_np))


def _bias_tables(rel_bias, n_new, n_past):
    idx_p = _prompt_bucket_index()
    idx_first = idx_p.copy()
    idx_first[:, :, :Q_BLK] = -1
    idx_both = np.stack([idx_p, idx_first], axis=1).reshape(2 * len(BRANCHES), Q_BLK, 2 * Q_BLK)
    bias_p = _bias_expand(rel_bias, idx_both, "bias_prompt", mult=LOG2E)
    bias_p = bias_p.reshape(len(BRANCHES), 2, N_PAIRS, 2 * Q_BLK, 2 * Q_BLK)
    ic, inw = _sample_bucket_index(n_new, n_past)
    return bias_p, _bias_expand(rel_bias, ic, "bias_cache"), _bias_expand(rel_bias, inw, "bias_new")


def _ada_body(c_ref, w_ref, b_ref, o_ref):
    c = c_ref[...]
    s = c * (1.0 / (1.0 + jnp.exp(-c)))
    o_ref[...] = jnp.dot(s.astype(BF16), w_ref[...].astype(BF16), preferred_element_type=F32) + b_ref[...]


def _ada(c_all, w_ada, b_ada):
    n, dm = c_all.shape
    n_out = w_ada.shape[1]
    tn = dm
    return pl.pallas_call(
        _ada_body,
        grid=(n_out // tn,),
        in_specs=[pl.BlockSpec((n, dm), lambda j: (0, 0)),
                  pl.BlockSpec((dm, tn), lambda j: (0, j)),
                  pl.BlockSpec((1, tn), lambda j: (0, j))],
        out_specs=pl.BlockSpec((n, tn), lambda j: (0, j)),
        out_shape=jax.ShapeDtypeStruct((n, n_out), F32),
        compiler_params=_cparams(("parallel",), 32),
        name="ada",
    )(c_all, w_ada, b_ada.reshape(1, n_out))


_SHIFT1, _SCALE1, _GATE1, _SHIFT2, _SCALE2, _GATE2 = range(6)


def _inproj_body(*refs, nb, tm, carry):
    if carry:
        (x_ref, sh_ref, sc_ref, w_ref, cw_ref, ng_ref, past_ref,
         q_ref, k_ref, v_ref, conv_ref, cst_ref, kt_ref, vt_ref, zz_ref) = refs
    else:
        (x_ref, sh_ref, sc_ref, w_ref, cw_ref, ng_ref, past_ref,
         q_ref, k_ref, v_ref, conv_ref, cst_ref, zz_ref) = refs
    dm = x_ref.shape[-1]
    rows = nb * tm
    u = x_ref[...] * (1.0 + sc_ref[...]) + sh_ref[...]
    u = u.reshape(rows, dm).astype(BF16)

    def proj(lo, width):
        return jnp.dot(u, w_ref[:, lo:lo + width], preferred_element_type=F32)

    q_ref[...] = proj(0, D_ATTN).reshape(nb, tm, D_ATTN)
    k = proj(D_ATTN, D_ATTN)
    v = proj(2 * D_ATTN, D_ATTN)
    k_ref[...] = k.reshape(nb, tm, D_ATTN)
    v_ref[...] = v.reshape(nb, tm, D_ATTN)
    if carry:
        kt_ref[0] = k.T
        vt_ref[0] = v.T
    gb = proj(3 * D_ATTN, D_CONV)
    gc = proj(3 * D_ATTN + D_CONV, D_CONV)
    hh = proj(3 * D_ATTN + 2 * D_CONV, D_CONV)
    z = (gc * hh).reshape(nb, tm, D_CONV)

    if carry:
        s = pl.program_id(1)

        @pl.when(s == 0)
        def _():
            zz_ref[:, 0:V7X_SUBLANES, :] = jnp.zeros((nb, V7X_SUBLANES, D_CONV), F32)

        @pl.when(s > 0)
        def _():
            zz_ref[:, 0:V7X_SUBLANES, :] = zz_ref[:, tm:tm + V7X_SUBLANES, :]
    else:
        zz_ref[:, V7X_SUBLANES - 2:V7X_SUBLANES, :] = past_ref[...]
    zz_ref[:, V7X_SUBLANES:, :] = z

    cw = cw_ref[...]
    yc = (cw[0:1, :] * zz_ref[:, V7X_SUBLANES - 2:V7X_SUBLANES - 2 + tm, :]
          + cw[1:2, :] * zz_ref[:, V7X_SUBLANES - 1:V7X_SUBLANES - 1 + tm, :]
          + cw[2:3, :] * z)
    g = gb.reshape(nb, tm, D_CONV) * yc
    ms = jnp.mean(g * g, axis=-1, keepdims=True)
    conv_ref[...] = g * lax.rsqrt(ms + RMS_EPS) * ng_ref[...]
    cst_ref[...] = zz_ref[:, tm + V7X_SUBLANES - 2:tm + V7X_SUBLANES, :]


def _inproj(x, ada3, w_in_bf, conv_w, norm_conv_g, past, *, nb, tm):
    bsz, seq, dm = x.shape
    carry = past is None
    if carry:
        past = jnp.zeros((bsz, CONV_WIDTH - 1, D_CONV), F32)
    grid = (bsz // nb, seq // tm)
    d_in = w_in_bf.shape[1]
    body = functools.partial(_inproj_body, nb=nb, tm=tm, carry=carry)
    tile = lambda width: pl.BlockSpec((nb, tm, width), lambda b, s: (b, s, 0))
    in_specs = [tile(dm),
                pl.BlockSpec((nb, 1, dm), lambda b, s: (b, 0, _SHIFT1)),
                pl.BlockSpec((nb, 1, dm), lambda b, s: (b, 0, _SCALE1)),
                pl.BlockSpec((dm, d_in), lambda b, s: (0, 0)),
                pl.BlockSpec((CONV_WIDTH, D_CONV), lambda b, s: (0, 0)),
                pl.BlockSpec((1, D_CONV), lambda b, s: (0, 0)),
                pl.BlockSpec((nb, CONV_WIDTH - 1, D_CONV), lambda b, s: (b, 0, 0))]
    args = [x, ada3, ada3, w_in_bf, conv_w, norm_conv_g.reshape(1, D_CONV), past]
    out_specs = [tile(D_ATTN), tile(D_ATTN), tile(D_ATTN), tile(D_CONV),
                 pl.BlockSpec((nb, CONV_WIDTH - 1, D_CONV), lambda b, s: (b, 0, 0))]
    out_shape = ([jax.ShapeDtypeStruct((bsz, seq, D_ATTN), F32)] * 3
                 + [jax.ShapeDtypeStruct((bsz, seq, D_CONV), F32),
                    jax.ShapeDtypeStruct((bsz, CONV_WIDTH - 1, D_CONV), F32)])
    if carry:
        assert nb == 1
        out_specs += [pl.BlockSpec((1, D_ATTN, tm), lambda b, s: (b, 0, s))] * 2
        out_shape += [jax.ShapeDtypeStruct((bsz, D_ATTN, seq), F32)] * 2
    return pl.pallas_call(
        body,
        grid=grid,
        in_specs=in_specs,
        out_specs=out_specs,
        out_shape=out_shape,
        scratch_shapes=[pltpu.VMEM((nb, tm + V7X_SUBLANES, D_CONV), F32)],
        compiler_params=_cparams(("parallel", "arbitrary"), 56),
        name="inproj_prompt" if carry else "inproj_sample",
    )(*args)


def _attn_prompt_body(q_ref, k_ref, v_ref, bias_ref, o_ref, ob_ref, lb_ref, *, seq):
    scale = HEAD_DIM ** -0.5 * LOG2E
    lane = lax.broadcasted_iota(I32, (Q_BLK, PAIR), 1)
    even = lane < HEAD_DIM
    ones = jnp.ones((Q_BLK, PAIR), BF16)

    def rows(ref, start, d):
        if d == 1:
            return ref[pl.ds(start, Q_BLK), :]
        return ref[pl.ds(start, Q_BLK, stride=d), :]

    def store(ref, br, start, d, val):
        if d == 1:
            ref[br, pl.ds(start, Q_BLK), :] = val
        else:
            ref[br, pl.ds(start, Q_BLK, stride=d), :] = val

    for br, (_, d) in enumerate(BRANCHES):
        nblk = seq // (d * Q_BLK)

        def block(it, carry, br=br, d=d, nblk=nblk):
            r = it // nblk
            i = it - r * nblk
            cur0 = r + d * Q_BLK * i
            qf = rows(q_ref, cur0, d) * scale
            qs = jnp.concatenate([jnp.where(even, qf, 0.0), jnp.where(even, 0.0, qf)], axis=0).astype(BF16)
            kc = rows(k_ref, cur0, d).astype(BF16)
            vc = rows(v_ref, cur0, d).astype(BF16)
            if nblk == 1:
                s = lax.dot_general(qs, kc, (((1,), (1,)), ((), ())), preferred_element_type=F32)
                s = s + bias_ref[br, 0, 0, :, Q_BLK:]
                vaug = jnp.concatenate([vc, ones], axis=1)
            else:
                prev0 = r + d * Q_BLK * jnp.maximum(i - 1, 0)
                kp = rows(k_ref, prev0, d).astype(BF16)
                vp = rows(v_ref, prev0, d).astype(BF16)
                kk = jnp.concatenate([kp, kc], axis=0)
                s = lax.dot_general(qs, kk, (((1,), (1,)), ((), ())), preferred_element_type=F32)
                s = s + bias_ref[br, jnp.where(i == 0, 1, 0), 0]
                vaug = jnp.concatenate([jnp.concatenate([vp, vc], axis=0),
                                        jnp.concatenate([ones, ones], axis=0)], axis=1)
            m = jnp.max(s, axis=-1, keepdims=True)
            p = jnp.exp2(s - m).astype(BF16)
            out = jnp.dot(p, vaug, preferred_element_type=F32)
            num, den = out[:, :PAIR], out[:, PAIR:]
            o = num / den
            lse = m + jnp.log2(den)
            store(ob_ref, br, cur0, d, jnp.where(even, o[:Q_BLK], o[Q_BLK:]))
            store(lb_ref, br, cur0, d, jnp.where(even, lse[:Q_BLK], lse[Q_BLK:]))
            return carry

        lax.fori_loop(0, d * nblk, block, 0, unroll=ATTN_UNROLL)

    mt = 2 * Q_BLK

    def merge(i, carry):
        sl = pl.ds(pl.multiple_of(i * mt, mt), mt)
        l0, l1, l2 = lb_ref[0, sl, :], lb_ref[1, sl, :], lb_ref[2, sl, :]
        mx = jnp.maximum(jnp.maximum(l0, l1), l2)
        w0, w1, w2 = jnp.exp2(l0 - mx), jnp.exp2(l1 - mx), jnp.exp2(l2 - mx)
        acc = w0 * ob_ref[0, sl, :] + w1 * ob_ref[1, sl, :] + w2 * ob_ref[2, sl, :]
        o_ref[sl, :] = acc / (w0 + w1 + w2)
        return carry

    lax.fori_loop(0, seq // mt, merge, 0)


def _attn_prompt(q, k, v, bias_p):
    bsz, seq, _ = q.shape
    assert seq % (BRANCHES[-1][1] * Q_BLK) == 0
    body = functools.partial(_attn_prompt_body, seq=seq)
    col = pl.BlockSpec((None, seq, PAIR), lambda b, j: (b, 0, j))
    return pl.pallas_call(
        body,
        grid=(bsz, N_PAIRS),
        in_specs=[col, col, col,
                  pl.BlockSpec((len(BRANCHES), 2, 1, 2 * Q_BLK, 2 * Q_BLK), lambda b, j: (0, 0, j, 0, 0))],
        out_specs=col,
        out_shape=jax.ShapeDtypeStruct((bsz, seq, D_ATTN), F32),
        scratch_shapes=[pltpu.VMEM((len(BRANCHES), seq, PAIR), F32),
                        pltpu.VMEM((len(BRANCHES), seq, PAIR), F32)],
        compiler_params=_cparams(("parallel", "parallel"), 48),
        name="attn_prompt",
    )(q, k, v, bias_p)


def _attn_sample_body(q_ref, kn_ref, vn_ref, ckt_ref, cvt_ref, bc_ref, bn_ref, o_ref):
    scale = HEAD_DIM ** -0.5
    nt = (((1,), (1,)), ((), ()))
    nbr = len(BRANCHES)
    for h in range(N_HEADS):
        rows = slice(h * HEAD_DIM, (h + 1) * HEAD_DIM)
        qh = (q_ref[h] * scale).astype(BF16)
        s_c = jnp.dot(qh, ckt_ref[rows, :].astype(BF16), preferred_element_type=F32)
        s_n = lax.dot_general(qh, kn_ref[h].astype(BF16), nt, preferred_element_type=F32)
        ps, lses, dens = [], [], []
        for br in range(nbr):
            lc = s_c + bc_ref[br, h]
            ln = s_n + bn_ref[br, h]
            m = jnp.maximum(jnp.max(lc, axis=-1, keepdims=True), jnp.max(ln, axis=-1, keepdims=True))
            pc = jnp.exp(lc - m)
            pn = jnp.exp(ln - m)
            den = jnp.sum(pc, axis=-1, keepdims=True) + jnp.sum(pn, axis=-1, keepdims=True)
            ps.append((pc, pn))
            dens.append(den)
            lses.append(m + jnp.log(den))
        mx = jnp.maximum(jnp.maximum(lses[0], lses[1]), lses[2])
        ws = [jnp.exp(l - mx) for l in lses]
        wsum = ws[0] + ws[1] + ws[2]
        coefs = [ws[br] / (wsum * dens[br]) for br in range(nbr)]
        p_c = coefs[0] * ps[0][0] + coefs[1] * ps[1][0] + coefs[2] * ps[2][0]
        p_n = coefs[0] * ps[0][1] + coefs[1] * ps[1][1] + coefs[2] * ps[2][1]
        o = lax.dot_general(p_c.astype(BF16), cvt_ref[rows, :].astype(BF16), nt, preferred_element_type=F32)
        o_ref[h] = o + jnp.dot(p_n.astype(BF16), vn_ref[h].astype(BF16), preferred_element_type=F32)


def _attn_sample(q, k_new, v_new, cache_kt, cache_vt, bias_c, bias_n):
    bsz, _, n_new, _ = q.shape
    n_past = cache_kt.shape[2]
    new = pl.BlockSpec((None, N_HEADS, n_new, HEAD_DIM), lambda b: (b, 0, 0, 0))
    cache = pl.BlockSpec((None, D_ATTN, n_past), lambda b: (b, 0, 0))
    return pl.pallas_call(
        _attn_sample_body,
        grid=(bsz,),
        in_specs=[new, new, new, cache, cache,
                  pl.BlockSpec(bias_c.shape, lambda b: (0, 0, 0, 0)), pl.BlockSpec(bias_n.shape, lambda b: (0, 0, 0, 0))],
        out_specs=new,
        out_shape=jax.ShapeDtypeStruct((bsz, N_HEADS, n_new, HEAD_DIM), F32),
        compiler_params=_cparams(("parallel",), 48),
        name="attn_sample",
    )(q, k_new, v_new, cache_kt, cache_vt, bias_c, bias_n)


def _layer_norm(y, g, b):
    mu = jnp.mean(y, axis=-1, keepdims=True)
    c = y - mu
    var = jnp.mean(c * c, axis=-1, keepdims=True)
    return c * lax.rsqrt(var + LN_EPS) * g + b


def _to_row_tiles(ref, val):
    rows, dm = val.shape
    assert dm == V7X_SUBLANES * V7X_LANES
    for c in range(V7X_SUBLANES):
        ref[pl.ds(c, rows, stride=V7X_SUBLANES), :] = val[:, c * V7X_LANES:(c + 1) * V7X_LANES]


def _from_row_tiles(ref, rows, lead=()):
    chunks = [ref[lead + (pl.ds(c, rows, stride=V7X_SUBLANES), slice(None))] for c in range(V7X_SUBLANES)]
    return jnp.concatenate(chunks, axis=1)


def _outproj_body(*refs, nb, tm, alpha, aliased, n_main):
    if aliased:
        refs = refs[2:]
    (a_ref, c_ref, x_ref, g1_ref, sh2_ref, sc2_ref, wa_ref, wc_ref, ng_ref, lg_ref, lb_ref, x1_ref, u2_ref) = refs
    rows = nb * tm

    @pl.when(pl.program_id(0) < n_main)
    def _():
        a = a_ref[...]
        ms = jnp.mean(a * a, axis=-1, keepdims=True)
        an = (a * lax.rsqrt(ms + RMS_EPS) * ng_ref[...]).reshape(rows, D_ATTN).astype(BF16)
        cn = c_ref[...].reshape(rows, D_CONV).astype(BF16)
        mix = (jnp.dot(an, wa_ref[...], preferred_element_type=F32)
               + jnp.dot(cn, wc_ref[...], preferred_element_type=F32))
        dm = mix.shape[-1]
        y = alpha * x_ref[...] + (1.0 + g1_ref[...]) * mix.reshape(nb, tm, dm)
        x1 = _layer_norm(y, lg_ref[...], lb_ref[...])
        x1_ref[...] = x1.reshape(rows, dm)
        u2_ref[...] = (x1 * (1.0 + sc2_ref[...]) + sh2_ref[...]).reshape(rows, dm)

    @pl.when(pl.program_id(0) >= n_main)
    def _():
        x1_ref[...] = jnp.zeros_like(x1_ref)
        u2_ref[...] = jnp.zeros_like(u2_ref)


def _outproj(attn, conv, x, ada3, w_o_bf, norm_attn_g, ln1_g, ln1_b, *, nb, tm, alpha, n_total, row0, prev=None):
    bsz, seq, dm = x.shape
    rows = nb * tm
    assert row0 % rows == 0
    st = seq // tm
    blk0 = row0 // rows
    n_main = (bsz // nb) * st
    aliased = prev is not None
    n_steps = n_main if aliased else pl.cdiv(n_total, rows)

    def bs_of(i):
        j = jnp.minimum(i, n_main - 1)
        return j // st, j % st

    tile = lambda width: pl.BlockSpec((nb, tm, width), lambda i: bs_of(i) + (0,))
    ada = lambda chunk: pl.BlockSpec((nb, 1, dm), lambda i: (bs_of(i)[0], 0, chunk))
    const = lambda shape: pl.BlockSpec(shape, lambda i: (0,) * len(shape))
    out = pl.BlockSpec((rows, dm), lambda i: (blk0 + i, 0))
    body = functools.partial(_outproj_body, nb=nb, tm=tm, alpha=alpha, aliased=aliased, n_main=n_main)
    in_specs = [tile(D_ATTN), tile(D_CONV), tile(dm), ada(_GATE1), ada(_SHIFT2), ada(_SCALE2),
                const((D_ATTN, dm)), const((D_CONV, dm)), const((1, D_ATTN)), const((1, dm)), const((1, dm))]
    args = [attn, conv, x, ada3, ada3, ada3, w_o_bf[:D_ATTN], w_o_bf[D_ATTN:],
            norm_attn_g.reshape(1, D_ATTN), ln1_g.reshape(1, dm), ln1_b.reshape(1, dm)]
    kwargs = {}
    if aliased:
        in_specs = [pl.BlockSpec(memory_space=pl.ANY)] * 2 + in_specs
        args = list(prev) + args
        kwargs["input_output_aliases"] = {0: 0, 1: 1}
    return pl.pallas_call(
        body,
        grid=(n_steps,),
        in_specs=in_specs,
        out_specs=[out, out],
        out_shape=[jax.ShapeDtypeStruct((n_total, dm), F32)] * 2,
        compiler_params=_cparams(("parallel",), 48),
        name="outproj_sample" if aliased else "outproj_prompt",
        **kwargs,
    )(*args)


def _router_body(u_ref, w_ref, b_ref, ri_ref, rr_ref, rg_ref, cnt_ref, run_ref, *, tm):
    i = pl.program_id(0)

    @pl.when(i == 0)
    def _():
        run_ref[...] = jnp.zeros_like(run_ref)

    u = u_ref[...]
    u_hi = u.astype(BF16)
    u_lo = (u - u_hi.astype(F32)).astype(BF16)
    both = jnp.dot(u_hi, w_ref[...], preferred_element_type=F32)
    cross = jnp.dot(u_lo, w_ref[:, :V7X_LANES], preferred_element_type=F32)
    logits = both[:, :V7X_LANES] + both[:, V7X_LANES:] + cross + b_ref[...]
    lane_i = lax.broadcasted_iota(I32, (tm, V7X_LANES), 1)
    lane = lane_i.astype(F32)
    vals = logits
    tops, idxs = [], []
    for _ in range(TOP_K):
        mk = jnp.max(vals, axis=-1, keepdims=True)
        ik = jnp.min(jnp.where(vals == mk, lane, float(V7X_LANES)), axis=-1, keepdims=True)
        tops.append(mk)
        idxs.append(ik)
        vals = jnp.where(lane == ik, -jnp.inf, vals)
    es = [jnp.exp(t - tops[0]) for t in tops]
    den = es[0] + es[1] + es[2] + es[3]
    sel = jnp.zeros((tm, V7X_LANES), F32)
    for ik in idxs:
        sel = sel + jnp.where(lane == ik, 1.0, 0.0)
    ri = lax.broadcasted_iota(I32, (tm, tm), 0)
    ci = lax.broadcasted_iota(I32, (tm, tm), 1)
    lower = jnp.where(ci < ri, 1.0, 0.0).astype(BF16)
    before = jnp.dot(lower, sel.astype(BF16), preferred_element_type=F32) + run_ref[...]
    out_i = jnp.full((tm, V7X_LANES), -1, I32)
    out_r = jnp.zeros((tm, V7X_LANES), I32)
    out_g = jnp.zeros((tm, V7X_LANES), F32)
    for kk in range(TOP_K):
        rank = jnp.sum(jnp.where(lane == idxs[kk], before, 0.0), axis=-1, keepdims=True)
        out_i = jnp.where(lane_i == kk, idxs[kk].astype(I32), out_i)
        out_r = jnp.where(lane_i == kk, rank.astype(I32), out_r)
        out_g = jnp.where(lane_i == kk, es[kk] / den, out_g)
    ri_ref[...] = out_i
    rr_ref[...] = out_r
    rg_ref[...] = out_g
    run_ref[...] = run_ref[...] + jnp.sum(sel, axis=0, keepdims=True)
    cnt_ref[...] = run_ref[...]


def _router(u2, router_w, router_b, *, tm):
    n, dm = u2.shape
    w_pad = jnp.pad(router_w, ((0, 0), (0, V7X_LANES - N_EXPERTS)))
    w_hi = w_pad.astype(BF16)
    w_lo = (w_pad - w_hi.astype(F32)).astype(BF16)
    b_pad = jnp.pad(router_b, (0, V7X_LANES - N_EXPERTS), constant_values=NEG_INF).reshape(1, V7X_LANES)
    tok = pl.BlockSpec((tm, V7X_LANES), lambda i: (i, 0))
    return pl.pallas_call(
        functools.partial(_router_body, tm=tm),
        grid=(n // tm,),
        in_specs=[pl.BlockSpec((tm, dm), lambda i: (i, 0)),
                  pl.BlockSpec((dm, 2 * V7X_LANES), lambda i: (0, 0)),
                  pl.BlockSpec((1, V7X_LANES), lambda i: (0, 0))],
        out_specs=[tok, tok, tok, pl.BlockSpec((1, V7X_LANES), lambda i: (0, 0))],
        out_shape=[jax.ShapeDtypeStruct((n, V7X_LANES), I32), jax.ShapeDtypeStruct((n, V7X_LANES), I32),
                   jax.ShapeDtypeStruct((n, V7X_LANES), F32), jax.ShapeDtypeStruct((1, V7X_LANES), F32)],
        scratch_shapes=[pltpu.VMEM((1, V7X_LANES), F32)],
        compiler_params=_cparams(("arbitrary",), 32),
        name="router",
    )(u2, jnp.concatenate([w_hi, w_lo], axis=1), b_pad)


def _dest_body(ps_ref, ri_ref, rr_ref, d_ref):
    idx = ri_ref[...]
    acc = rr_ref[...]
    for e in range(N_EXPERTS):
        acc = acc + jnp.where(idx == e, ps_ref[e], 0)
    d_ref[...] = acc


def _dest_rows(expert, rank, pad_starts):
    return pl.pallas_call(
        _dest_body,
        in_specs=[pl.BlockSpec(memory_space=pltpu.SMEM), pl.BlockSpec(memory_space=pltpu.VMEM),
                  pl.BlockSpec(memory_space=pltpu.VMEM)],
        out_specs=pl.BlockSpec(memory_space=pltpu.VMEM),
        out_shape=jax.ShapeDtypeStruct(expert.shape, I32),
        name="dest_rows",
    )(pad_starts, expert, rank)


ISSUE_TOKENS = 2


def _row_tile(ref, row):
    return ref.at[pl.ds(pl.multiple_of(row * V7X_SUBLANES, V7X_SUBLANES), V7X_SUBLANES), :]


def _issue_rows(dest_ref, n_tok, start_copy):
    def trip(it, carry):
        base = it * ISSUE_TOKENS
        rows = [dest_ref[0, 0, (base + u) * TOP_K + kk] for u in range(ISSUE_TOKENS) for kk in range(TOP_K)]
        for u in range(ISSUE_TOKENS):
            for kk in range(TOP_K):
                start_copy(base + u, kk, rows[u * TOP_K + kk])
        return carry

    lax.fori_loop(0, n_tok // ISSUE_TOKENS, trip, 0)


DISPATCH_SLOTS = 3


FILL_UNROLL = 8


def _dispatch_body(fr_ref, dest_ref, u_ref, xs_out, stage_ref, zero_ref, sems, fill_sem, *, tm, n_fill):
    i = pl.program_id(0)
    last = pl.num_programs(0) - 1
    slot = i % DISPATCH_SLOTS

    def wait_step(s):
        for _ in range(TOP_K):
            pltpu.make_async_copy(stage_ref.at[s], xs_out.at[pl.ds(0, tm * V7X_SUBLANES), :], sems.at[s]).wait()

    @pl.when(i == 0)
    def _():
        zero_ref[...] = jnp.zeros_like(zero_ref)

        def trip(it, carry):
            rows = [fr_ref[it * FILL_UNROLL + u] for u in range(FILL_UNROLL)]
            for u, row in enumerate(rows):
                pltpu.make_async_copy(zero_ref, _row_tile(xs_out, row), fill_sem).start(priority=u % DMA_QUEUES)
            return carry

        lax.fori_loop(0, n_fill // FILL_UNROLL, trip, 0)

    @pl.when(i >= DISPATCH_SLOTS - 1)
    def _():
        wait_step((i + 1) % DISPATCH_SLOTS)

    _to_row_tiles(stage_ref.at[slot], u_ref[...])

    def start_copy(r, kk, row):
        pltpu.make_async_copy(_row_tile(stage_ref.at[slot], r), _row_tile(xs_out, row),
                              sems.at[slot]).start(priority=kk % DMA_QUEUES)

    _issue_rows(dest_ref, tm, start_copy)

    @pl.when(i == last)
    def _():
        for back in range(DISPATCH_SLOTS - 1):
            @pl.when(i >= back)
            def _():
                wait_step((i - back) % DISPATCH_SLOTS)
        pltpu.make_async_copy(xs_out.at[pl.ds(0, n_fill * V7X_SUBLANES), :],
                              xs_out.at[pl.ds(0, n_fill * V7X_SUBLANES), :], fill_sem).wait()


def _dispatch(u2, dest_sm, fill_rows, n_rows, *, tm):
    n, dm = u2.shape
    n_fill = fill_rows.shape[0]
    assert n_fill == n_rows - n * TOP_K and n_fill % FILL_UNROLL == 0
    grid_spec = pltpu.PrefetchScalarGridSpec(
        num_scalar_prefetch=1,
        grid=(n // tm,),
        in_specs=[pl.BlockSpec((1, 1, tm * TOP_K), lambda i, fr: (i, 0, 0), memory_space=pltpu.SMEM),
                  pl.BlockSpec((tm, dm), lambda i, fr: (i, 0))],
        out_specs=pl.BlockSpec(memory_space=pl.ANY),
        scratch_shapes=[pltpu.VMEM((DISPATCH_SLOTS, tm * V7X_SUBLANES, V7X_LANES), F32),
                        pltpu.VMEM((V7X_SUBLANES, V7X_LANES), F32),
                        pltpu.SemaphoreType.DMA((DISPATCH_SLOTS,)), pltpu.SemaphoreType.DMA(())],
    )
    return pl.pallas_call(
        functools.partial(_dispatch_body, tm=tm, n_fill=n_fill),
        grid_spec=grid_spec,
        out_shape=jax.ShapeDtypeStruct((n_rows * V7X_SUBLANES, V7X_LANES), F32),
        compiler_params=_cparams(("arbitrary",), 32),
        name="dispatch",
    )(fill_rows, dest_sm, u2)


def _gmm_body(be_ref, bv_ref, bf_ref, bs_ref, bn_ref, xs_ref, wu_hbm, bu_ref, wd_hbm, bd_ref, ys_ref,
              wu_f32, wd_f32, wu_bf, wd_bf, sems, *, bm):
    j = pl.program_id(0)
    e = be_ref[j]
    s = bs_ref[j]
    d_ff = wd_hbm.shape[1]
    chunk = 64

    def fetch(ex, slot):
        return (pltpu.make_async_copy(wu_hbm.at[ex], wu_f32.at[slot], sems.at[0, slot]),
                pltpu.make_async_copy(wd_hbm.at[ex], wd_f32.at[slot], sems.at[1, slot]))

    @pl.when(bf_ref[j] != 0)
    def _():
        @pl.when(j == 0)
        def _():
            for c in fetch(e, s):
                c.start()

        for c in fetch(e, s):
            c.wait()

        @pl.when(bn_ref[j] >= 0)
        def _():
            for c in fetch(bn_ref[j], 1 - s):
                c.start()

        def cast(c, carry):
            sl = pl.ds(pl.multiple_of(c * chunk, chunk), chunk)
            wu_bf[sl, :] = wu_f32[s, sl, :].astype(BF16)
            wd_bf[sl, :] = wd_f32[s, sl, :].astype(BF16)
            return carry

        lax.fori_loop(0, wu_hbm.shape[1] // chunk, cast, 0)

    @pl.when(bv_ref[j] != 0)
    def _():
        for g in range(bm // EXPERT_ROWS):
            tiles = pl.ds(g * EXPERT_ROWS * V7X_SUBLANES, EXPERT_ROWS * V7X_SUBLANES)
            x = _from_row_tiles(xs_ref.at[tiles, :], EXPERT_ROWS).astype(BF16)
            glu = jnp.dot(x, wu_bf[:, :d_ff], preferred_element_type=F32) + bu_ref[0, :, :d_ff]
            lin = jnp.dot(x, wu_bf[:, d_ff:], preferred_element_type=F32) + bu_ref[0, :, d_ff:]
            glu = jnp.minimum(glu, SWIGLU_LIMIT)
            lin = jnp.clip(lin, -SWIGLU_LIMIT, SWIGLU_LIMIT)
            act = glu * (1.0 / (1.0 + jnp.exp(-SWIGLU_ALPHA * glu))) * (lin + 1.0)
            _to_row_tiles(ys_ref.at[tiles, :],
                          jnp.dot(act.astype(BF16), wd_bf[...], preferred_element_type=F32) + bd_ref[0])

    @pl.when(bv_ref[j] == 0)
    def _():
        ys_ref[...] = jnp.zeros_like(ys_ref)


def _gmm(xs, blk_e, blk_valid, blk_first, blk_slot, blk_next, w_up, b_up, w_down, b_down, *, bm):
    n_rows = xs.shape[0] // V7X_SUBLANES
    n_e, dm, d_up = w_up.shape
    d_ff = w_down.shape[1]
    assert d_ff == dm
    row_tiles = pl.BlockSpec((bm * V7X_SUBLANES, V7X_LANES), lambda j, *_: (j, 0))
    grid_spec = pltpu.PrefetchScalarGridSpec(
        num_scalar_prefetch=5,
        grid=(n_rows // bm,),
        in_specs=[row_tiles,
                  pl.BlockSpec(memory_space=pl.ANY),
                  pl.BlockSpec((1, 1, d_up), lambda j, be, *_: (be[j], 0, 0)),
                  pl.BlockSpec(memory_space=pl.ANY),
                  pl.BlockSpec((1, 1, dm), lambda j, be, *_: (be[j], 0, 0))],
        out_specs=row_tiles,
        scratch_shapes=[pltpu.VMEM((2, dm, d_up), F32), pltpu.VMEM((2, d_ff, dm), F32),
                        pltpu.VMEM((dm, d_up), BF16), pltpu.VMEM((d_ff, dm), BF16),
                        pltpu.SemaphoreType.DMA((2, 2))],
    )
    return pl.pallas_call(
        functools.partial(_gmm_body, bm=bm),
        grid_spec=grid_spec,
        out_shape=jax.ShapeDtypeStruct(xs.shape, F32),
        compiler_params=_cparams(("arbitrary",), 56),
        name="expert_mlp",
    )(blk_e, blk_valid, blk_first, blk_slot, blk_next, xs, w_up, b_up.reshape(n_e, 1, d_up), w_down,
      b_down.reshape(n_e, 1, dm))


def _combine_body(rt_ref, rtn_ref, ys_hbm, rg_ref, x1_ref, g2_ref, lg_ref, lb_ref, y_ref, rows_ref, sems,
                  *, nb, tm, alpha, n_steps):
    n_tok = nb * tm
    i = pl.program_id(0)
    slot = i % 2

    def issue(dest_ref, s):
        def start_copy(r, kk, row):
            pltpu.make_async_copy(_row_tile(ys_hbm, row), _row_tile(rows_ref.at[s, kk], r),
                                  sems.at[s]).start(priority=kk % DMA_QUEUES)

        _issue_rows(dest_ref, n_tok, start_copy)

    @pl.when(i == 0)
    def _():
        issue(rt_ref, 0)

    @pl.when(i + 1 < n_steps)
    def _():
        issue(rtn_ref, 1 - slot)

    for kk in range(TOP_K):
        pltpu.make_async_copy(ys_hbm.at[pl.ds(0, n_tok * V7X_SUBLANES), :], rows_ref.at[slot, kk],
                              sems.at[slot]).wait()

    gates = rg_ref[...]
    ffn = gates[:, 0:1] * _from_row_tiles(rows_ref, n_tok, (slot, 0))
    for kk in range(1, TOP_K):
        ffn = ffn + gates[:, kk:kk + 1] * _from_row_tiles(rows_ref, n_tok, (slot, kk))
    dm = ffn.shape[-1]
    y = alpha * x1_ref[...].reshape(nb, tm, dm) + (1.0 + g2_ref[...]) * ffn.reshape(nb, tm, dm)
    y_ref[...] = _layer_norm(y, lg_ref[...], lb_ref[...])


def _combine(ys, dest_sm, route_g, x1_all, ada3, ln2_g, ln2_b, *, bsz, seq, nb, tm, alpha, row0):
    dm = x1_all.shape[1]
    n_tok = nb * tm
    assert row0 % n_tok == 0 and dest_sm.shape[2] == n_tok * TOP_K and n_tok % ISSUE_TOKENS == 0
    blk0 = row0 // n_tok
    st = seq // tm
    n_steps = (bsz // nb) * st
    dest = lambda off: pl.BlockSpec((1, 1, n_tok * TOP_K),
                                    lambda i: (blk0 + jnp.minimum(i + off, n_steps - 1), 0, 0),
                                    memory_space=pltpu.SMEM)
    return pl.pallas_call(
        functools.partial(_combine_body, nb=nb, tm=tm, alpha=alpha, n_steps=n_steps),
        grid=(n_steps,),
        in_specs=[dest(0), dest(1),
                  pl.BlockSpec(memory_space=pl.ANY),
                  pl.BlockSpec((n_tok, V7X_LANES), lambda i: (blk0 + i, 0)),
                  pl.BlockSpec((n_tok, dm), lambda i: (blk0 + i, 0)),
                  pl.BlockSpec((nb, 1, dm), lambda i: (i // st, 0, _GATE2)),
                  pl.BlockSpec((1, dm), lambda i: (0, 0)),
                  pl.BlockSpec((1, dm), lambda i: (0, 0))],
        out_specs=pl.BlockSpec((nb, tm, dm), lambda i: (i // st, i % st, 0)),
        scratch_shapes=[pltpu.VMEM((2, TOP_K, n_tok * V7X_SUBLANES, V7X_LANES), F32),
                        pltpu.SemaphoreType.DMA((2,))],
        out_shape=jax.ShapeDtypeStruct((bsz, seq, dm), F32),
        compiler_params=_cparams(("arbitrary",), 48),
        name="combine_prompt" if row0 == 0 else "combine_sample",
    )(dest_sm, dest_sm, ys, route_g, x1_all, ada3, ln2_g.reshape(1, dm), ln2_b.reshape(1, dm))


def _layer(xp, xs, cache_k, cache_v, conv_past, cp, cs, w_ada, b_ada, w_in, conv_w, norm_attn_g, norm_conv_g,
           w_o, ln1_g, ln1_b, bias_tabs, router_w, router_b, w_up, b_up, w_down, b_down, ln2_g, ln2_b, alpha):
    bp, sp, dm = xp.shape
    bs, ts, _ = xs.shape
    n_p, n_s = bp * sp, bs * ts
    n_tok = n_p + n_s
    assert n_p % TOK_TILE == 0 and n_s == TOK_TILE and sp % ROW_TILE == 0 and sp % OUT_TILE == 0
    bias_p, bias_c, bias_n = bias_tabs

    ada = _ada(jnp.concatenate([cp, cs], axis=0), w_ada, b_ada)
    ada_p = ada[:bp].reshape(bp, 1, -1)
    ada_s = ada[bp:].reshape(bs, 1, -1)
    w_in_bf = w_in.astype(BF16)
    w_o_bf = w_o.astype(BF16)

    qp, kp, vp, convp, cstp, kpt, vpt = _inproj(xp, ada_p, w_in_bf, conv_w, norm_conv_g, None, nb=1, tm=ROW_TILE)
    attn_p = _attn_prompt(qp, kp, vp, bias_p)
    x1_all, u2_all = _outproj(attn_p, convp, xp, ada_p, w_o_bf, norm_attn_g, ln1_g, ln1_b,
                              nb=1, tm=OUT_TILE, alpha=alpha, n_total=n_tok, row0=0)
    qs, ks, vs, convs, csts = _inproj(xs, ada_s, w_in_bf, conv_w, norm_conv_g, conv_past, nb=bs, tm=ts)
    heads = lambda a: jnp.transpose(a.reshape(bs, ts, N_HEADS, HEAD_DIM), (0, 2, 1, 3))
    n_past = cache_k.shape[1]
    feat_major = lambda c: jnp.transpose(c.reshape(bs, n_past, D_ATTN), (0, 2, 1))
    attn_s = _attn_sample(heads(qs), heads(ks), heads(vs), feat_major(cache_k), feat_major(cache_v), bias_c, bias_n)
    attn_s = jnp.transpose(attn_s, (0, 2, 1, 3)).reshape(bs, ts, D_ATTN)
    x1_all, u2_all = _outproj(attn_s, convs, xs, ada_s, w_o_bf, norm_attn_g, ln1_g, ln1_b,
                              nb=bs, tm=ts, alpha=alpha, n_total=n_tok, row0=n_p, prev=(x1_all, u2_all))

    route_i, route_r, route_g, counts = _router(u2_all, router_w, router_b, tm=TOK_TILE)
    counts = counts[0, :N_EXPERTS].astype(I32)
    padded = ((counts + EXPERT_BLK - 1) // EXPERT_BLK) * EXPERT_BLK
    pad_ends = jnp.cumsum(padded)
    pad_starts = (pad_ends - padded).astype(I32)
    n_blocks = n_tok * TOP_K // EXPERT_BLK + N_EXPERTS
    n_rows = n_blocks * EXPERT_BLK
    blk_row = jnp.arange(n_blocks, dtype=I32) * EXPERT_BLK
    blk_valid = (blk_row < pad_ends[-1]).astype(I32)
    blk_e = jnp.sum((blk_row[:, None] >= pad_ends[None, :]).astype(I32), axis=1)
    last_e = jnp.sum((pad_ends[-1] - 1 >= pad_ends).astype(I32))
    blk_e = jnp.where(blk_valid != 0, blk_e, last_e).astype(I32)
    ar = jnp.arange(N_EXPERTS, dtype=I32)
    has_rows = padded > 0
    slot_of_e = (jnp.cumsum(has_rows.astype(I32)) - 1) % 2
    next_of_e = jnp.min(jnp.where((ar[None, :] > ar[:, None]) & has_rows[None, :], ar[None, :], N_EXPERTS), axis=1)
    next_of_e = jnp.where(next_of_e < N_EXPERTS, next_of_e, -1)
    is_e = blk_e[:, None] == ar[None, :]
    pick = lambda per_expert: jnp.sum(jnp.where(is_e, per_expert[None, :], 0), axis=1).astype(I32)
    blk_first = (blk_valid * (blk_row == pick(pad_starts)).astype(I32)).astype(I32)
    blk_slot = pick(slot_of_e)
    blk_next = pick(next_of_e)
    dense = lambda a: a[:, :TOP_K].reshape(n_tok * TOP_K // V7X_LANES, V7X_LANES)
    dest = _dest_rows(dense(route_i), dense(route_r), pad_starts)
    dest_sm = dest.reshape(n_tok // TOK_TILE, 1, TOK_TILE * TOP_K)

    run_start = jnp.concatenate([pad_starts + counts, pad_ends[-1:]])
    run_len = jnp.concatenate([padded - counts, n_rows - pad_ends[-1:]])
    run_k0 = jnp.cumsum(run_len) - run_len
    k = jnp.arange(n_rows - n_tok * TOP_K, dtype=I32)[:, None]
    in_run = (k >= run_k0[None, :]) & (k < (run_k0 + run_len)[None, :])
    fill_rows = jnp.sum(jnp.where(in_run, run_start[None, :] + k - run_k0[None, :], 0), axis=1).astype(I32)
    x_sorted = _dispatch(u2_all, dest_sm, fill_rows, n_rows, tm=TOK_TILE)
    y_sorted = _gmm(x_sorted, blk_e, blk_valid, blk_first, blk_slot, blk_next, w_up, b_up, w_down, b_down,
                    bm=EXPERT_BLK)
    yp = _combine(y_sorted, dest_sm, route_g, x1_all, ada_p, ln2_g, ln2_b,
                  bsz=bp, seq=sp, nb=1, tm=TOK_TILE, alpha=alpha, row0=0)
    ys_out = _combine(y_sorted, dest_sm, route_g, x1_all, ada_s, ln2_g, ln2_b,
                      bsz=bs, seq=ts, nb=bs, tm=ts, alpha=alpha, row0=n_p)
    return yp, ys_out, kpt, vpt, cstp, ks, vs, csts


def kernel(x_prompt, x_sample, cache_k, cache_v, state_conv, c_prompt, c_sample, w_ada, b_ada, w_in, conv_w,
           norm_attn_g, norm_conv_g, w_o, ln1_g, ln1_b, rel_bias, router_w, router_b, w_up, b_up, w_down, b_down,
           ln2_g, ln2_b):
    depth = w_ada.shape[0]
    alpha = (2 * depth) ** 0.25
    xp, xs = x_prompt, x_sample
    bp, sp, _ = xp.shape
    bs, ts, _ = xs.shape
    n_keep = min(BRANCHES[-1][0], sp)
    bias_tabs = _bias_tables(rel_bias, ts, cache_k.shape[2])
    outs = [[] for _ in range(6)]
    for l in range(depth):
        xp, xs, kp, vp, cstp, ks, vs, csts = _layer(
            xp, xs, cache_k[l], cache_v[l], state_conv[l], c_prompt, c_sample, w_ada[l], b_ada[l], w_in[l],
            conv_w[l], norm_attn_g[l], norm_conv_g[l], w_o[l], ln1_g[l], ln1_b[l], bias_tabs, router_w[l],
            router_b[l], w_up[l], b_up[l], w_down[l], b_down[l], ln2_g[l], ln2_b[l], alpha)
        kp = jnp.transpose(kp, (0, 2, 1)).reshape(bp, sp, N_HEADS, HEAD_DIM)[:, -n_keep:]
        vp = jnp.transpose(vp, (0, 2, 1)).reshape(bp, sp, N_HEADS, HEAD_DIM)[:, -n_keep:]
        for lst, val in zip(outs, (kp, vp, cstp, ks.reshape(bs, ts, N_HEADS, HEAD_DIM),
                                   vs.reshape(bs, ts, N_HEADS, HEAD_DIM), csts)):
            lst.append(val)
    return (xp, xs) + tuple(jnp.stack(o) for o in outs)
```

```python
import functools
import math

import numpy as np
import jax
import jax.numpy as jnp
from jax import lax
from jax.experimental import pallas as pl
from jax.experimental.pallas import tpu as pltpu

F32 = jnp.float32
BF16 = jnp.bfloat16
I32 = jnp.int32

HEAD_DIM = 64
N_HEADS = 12
D_ATTN = N_HEADS * HEAD_DIM
D_CONV = 256
CONV_WIDTH = 3
BRANCHES = ((128, 1), (512, 4), (2048, 16))
NUM_BUCKETS = 32
MAX_DISTANCE = 2048
N_EXPERTS = 32
TOP_K = 4
SWIGLU_LIMIT = 7.0
SWIGLU_ALPHA = 1.702
LN_EPS = 1e-5
RMS_EPS = 1e-6
NEG_INF = -1e30
LOG2E = math.log2(math.e)

V7X_LANES = 128
V7X_SUBLANES = 8
V7X_VMEM_BYTES = 64 * 1024 * 1024
DMA_QUEUES = 2

Q_BLK = 128
PAIR = 2 * HEAD_DIM
N_PAIRS = N_HEADS // 2
TOK_TILE = 256
ROW_TILE = 512
OUT_TILE = 512
EXPERT_BLK = 256
EXPERT_ROWS = 256
ATTN_UNROLL = 16


def _cparams(sem, vmem_mb):
    return pltpu.CompilerParams(dimension_semantics=sem, vmem_limit_bytes=vmem_mb * 1024 * 1024)


def _t5_bucket_np(dist):
    dist = np.asarray(dist, np.int64)
    max_exact = NUM_BUCKETS // 2
    df = np.maximum(dist, max_exact).astype(np.float32)
    large = max_exact + (np.log(df / np.float32(max_exact)) / np.float32(math.log(MAX_DISTANCE / max_exact))
                         * np.float32(NUM_BUCKETS - max_exact)).astype(np.int32)
    return np.where(dist < max_exact, dist, np.minimum(large, NUM_BUCKETS - 1)).astype(np.int32)


def _prompt_bucket_index():
    a = np.arange(Q_BLK)[:, None]
    c = np.arange(2 * Q_BLK)[None, :]
    step = Q_BLK + a - c
    valid = (step >= 0) & (step <= Q_BLK)
    out = []
    for _, d in BRANCHES:
        out.append(np.where(valid, _t5_bucket_np(np.clip(step, 0, Q_BLK) * d), -1))
    return np.stack(out).astype(np.int32)


def _sample_bucket_index(n_new, n_past):
    t = np.arange(n_new)[:, None]
    out_c, out_n = [], []
    for w, d in BRANCHES:
        dist_c = n_past + t - np.arange(n_past)[None, :]
        dist_n = t - np.arange(n_new)[None, :]
        for dist, out in ((dist_c, out_c), (dist_n, out_n)):
            valid = (dist >= 0) & (dist <= w) & (dist % d == 0)
            out.append(np.where(valid, _t5_bucket_np(np.clip(dist, 0, w)), -1))
    return np.stack(out_c).astype(np.int32), np.stack(out_n).astype(np.int32)


def _bias_body(rb_ref, idx_ref, out_ref, *, mult, hide_cols):
    idx = idx_ref[0]
    col = lax.broadcasted_iota(I32, idx.shape, 1)
    for h in range(N_HEADS):
        acc = jnp.where(idx < 0, NEG_INF, 0.0).astype(F32)
        for b in range(NUM_BUCKETS):
            acc = acc + jnp.where(idx == b, rb_ref[b * N_HEADS + h] * mult, 0.0)
        out_ref[0, 0, h] = acc
        if hide_cols:
            out_ref[0, 1, h] = jnp.where(col < hide_cols, NEG_INF, acc)


def _bias_expand(rel_bias, idx_np, name, mult=1.0, hide_cols=0):
    nbr, r, c = idx_np.shape
    nvar = 2 if hide_cols else 1
    return pl.pallas_call(
        functools.partial(_bias_body, mult=mult, hide_cols=hide_cols),
        grid=(nbr,),
        in_specs=[pl.BlockSpec(memory_space=pltpu.SMEM),
                  pl.BlockSpec((1, r, c), lambda i: (i, 0, 0))],
        out_specs=pl.BlockSpec((1, nvar, N_HEADS, r, c), lambda i: (i, 0, 0, 0, 0)),
        out_shape=jax.ShapeDtypeStruct((nbr, nvar, N_HEADS, r, c), F32),
        name=name,
    )(rel_bias.reshape(-1), jnp.asarray(idx_np))


def _bias_tables(rel_bias, n_new, n_past):
    bias_p = _bias_expand(rel_bias, _prompt_bucket_index(), "bias_prompt", mult=LOG2E, hide_cols=Q_BLK)
    bias_p = bias_p.reshape(len(BRANCHES), 2, N_PAIRS, 2 * Q_BLK, 2 * Q_BLK)
    ic, inw = _sample_bucket_index(n_new, n_past)
    plain = lambda idx, name: _bias_expand(rel_bias, idx, name).reshape((idx.shape[0], N_HEADS) + idx.shape[1:])
    return bias_p, plain(ic, "bias_cache"), plain(inw, "bias_new")


def _ada_body(c_ref, w_ref, b_ref, o_ref):
    c = c_ref[...]
    s = c * (1.0 / (1.0 + jnp.exp(-c)))
    o_ref[...] = jnp.dot(s.astype(BF16), w_ref[...].astype(BF16), preferred_element_type=F32) + b_ref[...]


def _ada(c_all, w_ada, b_ada):
    n, dm = c_all.shape
    n_out = w_ada.shape[1]
    tn = dm
    return pl.pallas_call(
        _ada_body,
        grid=(n_out // tn,),
        in_specs=[pl.BlockSpec((n, dm), lambda j: (0, 0)),
                  pl.BlockSpec((dm, tn), lambda j: (0, j)),
                  pl.BlockSpec((1, tn), lambda j: (0, j))],
        out_specs=pl.BlockSpec((n, tn), lambda j: (0, j)),
        out_shape=jax.ShapeDtypeStruct((n, n_out), F32),
        compiler_params=_cparams(("parallel",), 32),
        name="ada",
    )(c_all, w_ada, b_ada.reshape(1, n_out))


_SHIFT1, _SCALE1, _GATE1, _SHIFT2, _SCALE2, _GATE2 = range(6)


def _inproj_body(*refs, nb, tm, carry):
    if carry:
        (x_ref, sh_ref, sc_ref, w_ref, cw_ref, ng_ref, past_ref,
         q_ref, k_ref, v_ref, conv_ref, cst_ref, kt_ref, vt_ref, zz_ref) = refs
    else:
        (x_ref, sh_ref, sc_ref, w_ref, cw_ref, ng_ref, past_ref,
         q_ref, k_ref, v_ref, conv_ref, cst_ref, zz_ref) = refs
    dm = x_ref.shape[-1]
    rows = nb * tm
    u = x_ref[...] * (1.0 + sc_ref[...]) + sh_ref[...]
    u = u.reshape(rows, dm).astype(BF16)

    def proj(lo, width):
        return jnp.dot(u, w_ref[:, lo:lo + width], preferred_element_type=F32)

    q_ref[...] = proj(0, D_ATTN).reshape(nb, tm, D_ATTN)
    k = proj(D_ATTN, D_ATTN)
    v = proj(2 * D_ATTN, D_ATTN)
    k_ref[...] = k.reshape(nb, tm, D_ATTN)
    v_ref[...] = v.reshape(nb, tm, D_ATTN)
    if carry:
        kt_ref[0] = k.T
        vt_ref[0] = v.T
    gb = proj(3 * D_ATTN, D_CONV)
    gc = proj(3 * D_ATTN + D_CONV, D_CONV)
    hh = proj(3 * D_ATTN + 2 * D_CONV, D_CONV)
    z = (gc * hh).reshape(nb, tm, D_CONV)

    if carry:
        s = pl.program_id(1)

        @pl.when(s == 0)
        def _():
            zz_ref[:, 0:V7X_SUBLANES, :] = jnp.zeros((nb, V7X_SUBLANES, D_CONV), F32)

        @pl.when(s > 0)
        def _():
            zz_ref[:, 0:V7X_SUBLANES, :] = zz_ref[:, tm:tm + V7X_SUBLANES, :]
    else:
        zz_ref[:, V7X_SUBLANES - 2:V7X_SUBLANES, :] = past_ref[...]
    zz_ref[:, V7X_SUBLANES:, :] = z

    cw = cw_ref[...]
    yc = (cw[0:1, :] * zz_ref[:, V7X_SUBLANES - 2:V7X_SUBLANES - 2 + tm, :]
          + cw[1:2, :] * zz_ref[:, V7X_SUBLANES - 1:V7X_SUBLANES - 1 + tm, :]
          + cw[2:3, :] * z)
    g = gb.reshape(nb, tm, D_CONV) * yc
    ms = jnp.mean(g * g, axis=-1, keepdims=True)
    conv_ref[...] = g * lax.rsqrt(ms + RMS_EPS) * ng_ref[...]
    cst_ref[...] = zz_ref[:, tm + V7X_SUBLANES - 2:tm + V7X_SUBLANES, :]


def _inproj(x, ada3, w_in_bf, conv_w, norm_conv_g, past, *, nb, tm):
    bsz, seq, dm = x.shape
    carry = past is None
    if carry:
        past = jnp.zeros((bsz, CONV_WIDTH - 1, D_CONV), F32)
    grid = (bsz // nb, seq // tm)
    d_in = w_in_bf.shape[1]
    body = functools.partial(_inproj_body, nb=nb, tm=tm, carry=carry)
    tile = lambda width: pl.BlockSpec((nb, tm, width), lambda b, s: (b, s, 0))
    in_specs = [tile(dm),
                pl.BlockSpec((nb, 1, dm), lambda b, s: (b, 0, _SHIFT1)),
                pl.BlockSpec((nb, 1, dm), lambda b, s: (b, 0, _SCALE1)),
                pl.BlockSpec((dm, d_in), lambda b, s: (0, 0)),
                pl.BlockSpec((CONV_WIDTH, D_CONV), lambda b, s: (0, 0)),
                pl.BlockSpec((1, D_CONV), lambda b, s: (0, 0)),
                pl.BlockSpec((nb, CONV_WIDTH - 1, D_CONV), lambda b, s: (b, 0, 0))]
    args = [x, ada3, ada3, w_in_bf, conv_w, norm_conv_g.reshape(1, D_CONV), past]
    out_specs = [tile(D_ATTN), tile(D_ATTN), tile(D_ATTN), tile(D_CONV),
                 pl.BlockSpec((nb, CONV_WIDTH - 1, D_CONV), lambda b, s: (b, 0, 0))]
    out_shape = ([jax.ShapeDtypeStruct((bsz, seq, D_ATTN), F32)] * 3
                 + [jax.ShapeDtypeStruct((bsz, seq, D_CONV), F32),
                    jax.ShapeDtypeStruct((bsz, CONV_WIDTH - 1, D_CONV), F32)])
    if carry:
        assert nb == 1
        out_specs += [pl.BlockSpec((1, D_ATTN, tm), lambda b, s: (b, 0, s))] * 2
        out_shape += [jax.ShapeDtypeStruct((bsz, D_ATTN, seq), F32)] * 2
    return pl.pallas_call(
        body,
        grid=grid,
        in_specs=in_specs,
        out_specs=out_specs,
        out_shape=out_shape,
        scratch_shapes=[pltpu.VMEM((nb, tm + V7X_SUBLANES, D_CONV), F32)],
        compiler_params=_cparams(("parallel", "arbitrary"), 56),
        name="inproj_prompt" if carry else "inproj_sample",
    )(*args)


def _attn_prompt_body(q_ref, k_ref, v_ref, bias_ref, o_ref, ob_ref, lb_ref, *, seq):
    scale = HEAD_DIM ** -0.5 * LOG2E
    lane = lax.broadcasted_iota(I32, (Q_BLK, PAIR), 1)
    even = lane < HEAD_DIM
    ones = jnp.ones((Q_BLK, PAIR), BF16)

    def rows(ref, start, d):
        if d == 1:
            return ref[pl.ds(start, Q_BLK), :]
        return ref[pl.ds(start, Q_BLK, stride=d), :]

    def store(ref, br, start, d, val):
        if d == 1:
            ref[br, pl.ds(start, Q_BLK), :] = val
        else:
            ref[br, pl.ds(start, Q_BLK, stride=d), :] = val

    for br, (_, d) in enumerate(BRANCHES):
        nblk = seq // (d * Q_BLK)

        def block(it, carry, br=br, d=d, nblk=nblk):
            r = it // nblk
            i = it - r * nblk
            cur0 = r + d * Q_BLK * i
            qf = rows(q_ref, cur0, d) * scale
            qs = jnp.concatenate([jnp.where(even, qf, 0.0), jnp.where(even, 0.0, qf)], axis=0).astype(BF16)
            kc = rows(k_ref, cur0, d).astype(BF16)
            vc = rows(v_ref, cur0, d).astype(BF16)
            if nblk == 1:
                s = lax.dot_general(qs, kc, (((1,), (1,)), ((), ())), preferred_element_type=F32)
                s = s + bias_ref[br, 0, 0, :, Q_BLK:]
                vaug = jnp.concatenate([vc, ones], axis=1)
            else:
                prev0 = r + d * Q_BLK * jnp.maximum(i - 1, 0)
                kp = rows(k_ref, prev0, d).astype(BF16)
                vp = rows(v_ref, prev0, d).astype(BF16)
                kk = jnp.concatenate([kp, kc], axis=0)
                s = lax.dot_general(qs, kk, (((1,), (1,)), ((), ())), preferred_element_type=F32)
                s = s + bias_ref[br, jnp.where(i == 0, 1, 0), 0]
                vaug = jnp.concatenate([jnp.concatenate([vp, vc], axis=0),
                                        jnp.concatenate([ones, ones], axis=0)], axis=1)
            m = jnp.max(s, axis=-1, keepdims=True)
            p = jnp.exp2(s - m).astype(BF16)
            out = jnp.dot(p, vaug, preferred_element_type=F32)
            num, den = out[:, :PAIR], out[:, PAIR:]
            o = num / den
            lse = m + jnp.log2(den)
            store(ob_ref, br, cur0, d, jnp.where(even, o[:Q_BLK], o[Q_BLK:]))
            store(lb_ref, br, cur0, d, jnp.where(even, lse[:Q_BLK], lse[Q_BLK:]))
            return carry

        lax.fori_loop(0, d * nblk, block, 0, unroll=ATTN_UNROLL)

    mt = 2 * Q_BLK

    def merge(i, carry):
        sl = pl.ds(pl.multiple_of(i * mt, mt), mt)
        l0, l1, l2 = lb_ref[0, sl, :], lb_ref[1, sl, :], lb_ref[2, sl, :]
        mx = jnp.maximum(jnp.maximum(l0, l1), l2)
        w0, w1, w2 = jnp.exp2(l0 - mx), jnp.exp2(l1 - mx), jnp.exp2(l2 - mx)
        acc = w0 * ob_ref[0, sl, :] + w1 * ob_ref[1, sl, :] + w2 * ob_ref[2, sl, :]
        o_ref[sl, :] = acc / (w0 + w1 + w2)
        return carry

    lax.fori_loop(0, seq // mt, merge, 0)


def _attn_prompt(q, k, v, bias_p):
    bsz, seq, _ = q.shape
    assert seq % (BRANCHES[-1][1] * Q_BLK) == 0
    body = functools.partial(_attn_prompt_body, seq=seq)
    col = pl.BlockSpec((None, seq, PAIR), lambda b, j: (b, 0, j))
    return pl.pallas_call(
        body,
        grid=(bsz, N_PAIRS),
        in_specs=[col, col, col,
                  pl.BlockSpec((len(BRANCHES), 2, 1, 2 * Q_BLK, 2 * Q_BLK), lambda b, j: (0, 0, j, 0, 0))],
        out_specs=col,
        out_shape=jax.ShapeDtypeStruct((bsz, seq, D_ATTN), F32),
        scratch_shapes=[pltpu.VMEM((len(BRANCHES), seq, PAIR), F32),
                        pltpu.VMEM((len(BRANCHES), seq, PAIR), F32)],
        compiler_params=_cparams(("parallel", "parallel"), 48),
        name="attn_prompt",
    )(q, k, v, bias_p)


def _attn_sample_body(q_ref, kn_ref, vn_ref, ckt_ref, cvt_ref, bc_ref, bn_ref, o_ref):
    scale = HEAD_DIM ** -0.5
    nt = (((1,), (1,)), ((), ()))
    nbr = len(BRANCHES)
    for h in range(N_HEADS):
        rows = slice(h * HEAD_DIM, (h + 1) * HEAD_DIM)
        qh = (q_ref[h] * scale).astype(BF16)
        s_c = jnp.dot(qh, ckt_ref[rows, :].astype(BF16), preferred_element_type=F32)
        s_n = lax.dot_general(qh, kn_ref[h].astype(BF16), nt, preferred_element_type=F32)
        ps, lses, dens = [], [], []
        for br in range(nbr):
            lc = s_c + bc_ref[br, h]
            ln = s_n + bn_ref[br, h]
            m = jnp.maximum(jnp.max(lc, axis=-1, keepdims=True), jnp.max(ln, axis=-1, keepdims=True))
            pc = jnp.exp(lc - m)
            pn = jnp.exp(ln - m)
            den = jnp.sum(pc, axis=-1, keepdims=True) + jnp.sum(pn, axis=-1, keepdims=True)
            ps.append((pc, pn))
            dens.append(den)
            lses.append(m + jnp.log(den))
        mx = jnp.maximum(jnp.maximum(lses[0], lses[1]), lses[2])
        ws = [jnp.exp(l - mx) for l in lses]
        wsum = ws[0] + ws[1] + ws[2]
        coefs = [ws[br] / (wsum * dens[br]) for br in range(nbr)]
        p_c = coefs[0] * ps[0][0] + coefs[1] * ps[1][0] + coefs[2] * ps[2][0]
        p_n = coefs[0] * ps[0][1] + coefs[1] * ps[1][1] + coefs[2] * ps[2][1]
        o = lax.dot_general(p_c.astype(BF16), cvt_ref[rows, :].astype(BF16), nt, preferred_element_type=F32)
        o_ref[h] = o + jnp.dot(p_n.astype(BF16), vn_ref[h].astype(BF16), preferred_element_type=F32)


def _attn_sample(q, k_new, v_new, cache_kt, cache_vt, bias_c, bias_n):
    bsz, _, n_new, _ = q.shape
    n_past = cache_kt.shape[2]
    new = pl.BlockSpec((None, N_HEADS, n_new, HEAD_DIM), lambda b: (b, 0, 0, 0))
    cache = pl.BlockSpec((None, D_ATTN, n_past), lambda b: (b, 0, 0))
    return pl.pallas_call(
        _attn_sample_body,
        grid=(bsz,),
        in_specs=[new, new, new, cache, cache,
                  pl.BlockSpec(bias_c.shape, lambda b: (0, 0, 0, 0)), pl.BlockSpec(bias_n.shape, lambda b: (0, 0, 0, 0))],
        out_specs=new,
        out_shape=jax.ShapeDtypeStruct((bsz, N_HEADS, n_new, HEAD_DIM), F32),
        compiler_params=_cparams(("parallel",), 48),
        name="attn_sample",
    )(q, k_new, v_new, cache_kt, cache_vt, bias_c, bias_n)


def _layer_norm(y, g, b):
    mu = jnp.mean(y, axis=-1, keepdims=True)
    c = y - mu
    var = jnp.mean(c * c, axis=-1, keepdims=True)
    return c * lax.rsqrt(var + LN_EPS) * g + b


def _to_row_tiles(ref, val):
    rows, dm = val.shape
    assert dm == V7X_SUBLANES * V7X_LANES
    for c in range(V7X_SUBLANES):
        ref[pl.ds(c, rows, stride=V7X_SUBLANES), :] = val[:, c * V7X_LANES:(c + 1) * V7X_LANES]


def _from_row_tiles(ref, rows, lead=()):
    chunks = [ref[lead + (pl.ds(c, rows, stride=V7X_SUBLANES), slice(None))] for c in range(V7X_SUBLANES)]
    return jnp.concatenate(chunks, axis=1)


def _outproj_body(*refs, nb, tm, alpha, aliased, n_main):
    if aliased:
        refs = refs[2:]
    (a_ref, c_ref, x_ref, g1_ref, sh2_ref, sc2_ref, wa_ref, wc_ref, ng_ref, lg_ref, lb_ref, x1_ref, u2_ref) = refs
    rows = nb * tm

    @pl.when(pl.program_id(0) < n_main)
    def _():
        a = a_ref[...]
        ms = jnp.mean(a * a, axis=-1, keepdims=True)
        an = (a * lax.rsqrt(ms + RMS_EPS) * ng_ref[...]).reshape(rows, D_ATTN).astype(BF16)
        cn = c_ref[...].reshape(rows, D_CONV).astype(BF16)
        mix = (jnp.dot(an, wa_ref[...], preferred_element_type=F32)
               + jnp.dot(cn, wc_ref[...], preferred_element_type=F32))
        dm = mix.shape[-1]
        y = alpha * x_ref[...] + (1.0 + g1_ref[...]) * mix.reshape(nb, tm, dm)
        x1 = _layer_norm(y, lg_ref[...], lb_ref[...])
        x1_ref[...] = x1.reshape(rows, dm)
        u2_ref[...] = (x1 * (1.0 + sc2_ref[...]) + sh2_ref[...]).reshape(rows, dm)

    @pl.when(pl.program_id(0) >= n_main)
    def _():
        x1_ref[...] = jnp.zeros_like(x1_ref)
        u2_ref[...] = jnp.zeros_like(u2_ref)


def _outproj(attn, conv, x, ada3, w_o_bf, norm_attn_g, ln1_g, ln1_b, *, nb, tm, alpha, n_total, row0, prev=None):
    bsz, seq, dm = x.shape
    rows = nb * tm
    assert row0 % rows == 0
    st = seq // tm
    blk0 = row0 // rows
    n_main = (bsz // nb) * st
    aliased = prev is not None
    n_steps = n_main if aliased else pl.cdiv(n_total, rows)

    def bs_of(i):
        j = jnp.minimum(i, n_main - 1)
        return j // st, j % st

    tile = lambda width: pl.BlockSpec((nb, tm, width), lambda i: bs_of(i) + (0,))
    ada = lambda chunk: pl.BlockSpec((nb, 1, dm), lambda i: (bs_of(i)[0], 0, chunk))
    const = lambda shape: pl.BlockSpec(shape, lambda i: (0,) * len(shape))
    out = pl.BlockSpec((rows, dm), lambda i: (blk0 + i, 0))
    body = functools.partial(_outproj_body, nb=nb, tm=tm, alpha=alpha, aliased=aliased, n_main=n_main)
    in_specs = [tile(D_ATTN), tile(D_CONV), tile(dm), ada(_GATE1), ada(_SHIFT2), ada(_SCALE2),
                const((D_ATTN, dm)), const((D_CONV, dm)), const((1, D_ATTN)), const((1, dm)), const((1, dm))]
    args = [attn, conv, x, ada3, ada3, ada3, w_o_bf[:D_ATTN], w_o_bf[D_ATTN:],
            norm_attn_g.reshape(1, D_ATTN), ln1_g.reshape(1, dm), ln1_b.reshape(1, dm)]
    kwargs = {}
    if aliased:
        in_specs = [pl.BlockSpec(memory_space=pl.ANY)] * 2 + in_specs
        args = list(prev) + args
        kwargs["input_output_aliases"] = {0: 0, 1: 1}
    return pl.pallas_call(
        body,
        grid=(n_steps,),
        in_specs=in_specs,
        out_specs=[out, out],
        out_shape=[jax.ShapeDtypeStruct((n_total, dm), F32)] * 2,
        compiler_params=_cparams(("parallel",), 48),
        name="outproj_sample" if aliased else "outproj_prompt",
        **kwargs,
    )(*args)


def _router_body(u_ref, w_ref, b_ref, ri_ref, rr_ref, rg_ref, cnt_ref, run_ref, *, tm):
    i = pl.program_id(0)

    @pl.when(i == 0)
    def _():
        run_ref[...] = jnp.zeros_like(run_ref)

    u = u_ref[...]
    u_hi = u.astype(BF16)
    u_lo = (u - u_hi.astype(F32)).astype(BF16)
    both = jnp.dot(u_hi, w_ref[...], preferred_element_type=F32)
    cross = jnp.dot(u_lo, w_ref[:, :V7X_LANES], preferred_element_type=F32)
    logits = both[:, :V7X_LANES] + both[:, V7X_LANES:] + cross + b_ref[...]
    lane_i = lax.broadcasted_iota(I32, (tm, V7X_LANES), 1)
    lane = lane_i.astype(F32)
    vals = logits
    tops, idxs = [], []
    for _ in range(TOP_K):
        mk = jnp.max(vals, axis=-1, keepdims=True)
        ik = jnp.min(jnp.where(vals == mk, lane, float(V7X_LANES)), axis=-1, keepdims=True)
        tops.append(mk)
        idxs.append(ik)
        vals = jnp.where(lane == ik, -jnp.inf, vals)
    es = [jnp.exp(t - tops[0]) for t in tops]
    den = es[0] + es[1] + es[2] + es[3]
    sel = jnp.zeros((tm, V7X_LANES), F32)
    for ik in idxs:
        sel = sel + jnp.where(lane == ik, 1.0, 0.0)
    ri = lax.broadcasted_iota(I32, (tm, tm), 0)
    ci = lax.broadcasted_iota(I32, (tm, tm), 1)
    lower = jnp.where(ci < ri, 1.0, 0.0).astype(BF16)
    before = jnp.dot(lower, sel.astype(BF16), preferred_element_type=F32) + run_ref[...]
    out_i = jnp.full((tm, V7X_LANES), -1, I32)
    out_r = jnp.zeros((tm, V7X_LANES), I32)
    out_g = jnp.zeros((tm, V7X_LANES), F32)
    for kk in range(TOP_K):
        rank = jnp.sum(jnp.where(lane == idxs[kk], before, 0.0), axis=-1, keepdims=True)
        out_i = jnp.where(lane_i == kk, idxs[kk].astype(I32), out_i)
        out_r = jnp.where(lane_i == kk, rank.astype(I32), out_r)
        out_g = jnp.where(lane_i == kk, es[kk] / den, out_g)
    ri_ref[...] = out_i
    rr_ref[...] = out_r
    rg_ref[...] = out_g
    run_ref[...] = run_ref[...] + jnp.sum(sel, axis=0, keepdims=True)
    cnt_ref[...] = run_ref[...]


def _router(u2, router_w, router_b, *, tm):
    n, dm = u2.shape
    w_pad = jnp.pad(router_w, ((0, 0), (0, V7X_LANES - N_EXPERTS)))
    w_hi = w_pad.astype(BF16)
    w_lo = (w_pad - w_hi.astype(F32)).astype(BF16)
    b_pad = jnp.pad(router_b, (0, V7X_LANES - N_EXPERTS), constant_values=NEG_INF).reshape(1, V7X_LANES)
    tok = pl.BlockSpec((tm, V7X_LANES), lambda i: (i, 0))
    return pl.pallas_call(
        functools.partial(_router_body, tm=tm),
        grid=(n // tm,),
        in_specs=[pl.BlockSpec((tm, dm), lambda i: (i, 0)),
                  pl.BlockSpec((dm, 2 * V7X_LANES), lambda i: (0, 0)),
                  pl.BlockSpec((1, V7X_LANES), lambda i: (0, 0))],
        out_specs=[tok, tok, tok, pl.BlockSpec((1, V7X_LANES), lambda i: (0, 0))],
        out_shape=[jax.ShapeDtypeStruct((n, V7X_LANES), I32), jax.ShapeDtypeStruct((n, V7X_LANES), I32),
                   jax.ShapeDtypeStruct((n, V7X_LANES), F32), jax.ShapeDtypeStruct((1, V7X_LANES), F32)],
        scratch_shapes=[pltpu.VMEM((1, V7X_LANES), F32)],
        compiler_params=_cparams(("arbitrary",), 32),
        name="router",
    )(u2, jnp.concatenate([w_hi, w_lo], axis=1), b_pad)


def _dest_body(ps_ref, ri_ref, rr_ref, d_ref):
    idx = ri_ref[...]
    acc = rr_ref[...]
    for e in range(N_EXPERTS):
        acc = acc + jnp.where(idx == e, ps_ref[e], 0)
    d_ref[...] = acc


def _dest_rows(expert, rank, pad_starts):
    return pl.pallas_call(
        _dest_body,
        in_specs=[pl.BlockSpec(memory_space=pltpu.SMEM), pl.BlockSpec(memory_space=pltpu.VMEM),
                  pl.BlockSpec(memory_space=pltpu.VMEM)],
        out_specs=pl.BlockSpec(memory_space=pltpu.VMEM),
        out_shape=jax.ShapeDtypeStruct(expert.shape, I32),
        name="dest_rows",
    )(pad_starts, expert, rank)


ISSUE_TOKENS = 2


def _row_tile(ref, row):
    return ref.at[pl.ds(pl.multiple_of(row * V7X_SUBLANES, V7X_SUBLANES), V7X_SUBLANES), :]


def _issue_rows(dest_ref, n_tok, start_copy):
    def trip(it, carry):
        base = it * ISSUE_TOKENS
        rows = [dest_ref[0, 0, (base + u) * TOP_K + kk] for u in range(ISSUE_TOKENS) for kk in range(TOP_K)]
        for u in range(ISSUE_TOKENS):
            for kk in range(TOP_K):
                start_copy(base + u, kk, rows[u * TOP_K + kk])
        return carry

    lax.fori_loop(0, n_tok // ISSUE_TOKENS, trip, 0)


DISPATCH_SLOTS = 3


FILL_UNROLL = 8


def _dispatch_body(fr_ref, dest_ref, u_ref, xs_out, stage_ref, zero_ref, sems, fill_sem, *, tm, n_fill):
    i = pl.program_id(0)
    last = pl.num_programs(0) - 1
    slot = i % DISPATCH_SLOTS

    def wait_step(s):
        for _ in range(TOP_K):
            pltpu.make_async_copy(stage_ref.at[s], xs_out.at[pl.ds(0, tm * V7X_SUBLANES), :], sems.at[s]).wait()

    @pl.when(i == 0)
    def _():
        zero_ref[...] = jnp.zeros_like(zero_ref)

        def trip(it, carry):
            rows = [fr_ref[it * FILL_UNROLL + u] for u in range(FILL_UNROLL)]
            for u, row in enumerate(rows):
                pltpu.make_async_copy(zero_ref, _row_tile(xs_out, row), fill_sem).start(priority=u % DMA_QUEUES)
            return carry

        lax.fori_loop(0, n_fill // FILL_UNROLL, trip, 0)

    @pl.when(i >= DISPATCH_SLOTS - 1)
    def _():
        wait_step((i + 1) % DISPATCH_SLOTS)

    _to_row_tiles(stage_ref.at[slot], u_ref[...])

    def start_copy(r, kk, row):
        pltpu.make_async_copy(_row_tile(stage_ref.at[slot], r), _row_tile(xs_out, row),
                              sems.at[slot]).start(priority=kk % DMA_QUEUES)

    _issue_rows(dest_ref, tm, start_copy)

    @pl.when(i == last)
    def _():
        for back in range(DISPATCH_SLOTS - 1):
            @pl.when(i >= back)
            def _():
                wait_step((i - back) % DISPATCH_SLOTS)
        pltpu.make_async_copy(xs_out.at[pl.ds(0, n_fill * V7X_SUBLANES), :],
                              xs_out.at[pl.ds(0, n_fill * V7X_SUBLANES), :], fill_sem).wait()


def _dispatch(u2, dest_sm, fill_rows, n_rows, *, tm):
    n, dm = u2.shape
    n_fill = fill_rows.shape[0]
    assert n_fill == n_rows - n * TOP_K and n_fill % FILL_UNROLL == 0
    grid_spec = pltpu.PrefetchScalarGridSpec(
        num_scalar_prefetch=1,
        grid=(n // tm,),
        in_specs=[pl.BlockSpec((1, 1, tm * TOP_K), lambda i, fr: (i, 0, 0), memory_space=pltpu.SMEM),
                  pl.BlockSpec((tm, dm), lambda i, fr: (i, 0))],
        out_specs=pl.BlockSpec(memory_space=pl.ANY),
        scratch_shapes=[pltpu.VMEM((DISPATCH_SLOTS, tm * V7X_SUBLANES, V7X_LANES), F32),
                        pltpu.VMEM((V7X_SUBLANES, V7X_LANES), F32),
                        pltpu.SemaphoreType.DMA((DISPATCH_SLOTS,)), pltpu.SemaphoreType.DMA(())],
    )
    return pl.pallas_call(
        functools.partial(_dispatch_body, tm=tm, n_fill=n_fill),
        grid_spec=grid_spec,
        out_shape=jax.ShapeDtypeStruct((n_rows * V7X_SUBLANES, V7X_LANES), F32),
        compiler_params=_cparams(("arbitrary",), 32),
        name="dispatch",
    )(fill_rows, dest_sm, u2)


def _gmm_body(be_ref, bv_ref, bf_ref, bs_ref, bn_ref, xs_ref, wu_hbm, bu_ref, wd_hbm, bd_ref, ys_ref,
              wu_f32, wd_f32, wu_bf, wd_bf, sems, *, bm):
    j = pl.program_id(0)
    e = be_ref[j]
    s = bs_ref[j]
    d_ff = wd_hbm.shape[1]
    chunk = 64

    def fetch(ex, slot):
        return (pltpu.make_async_copy(wu_hbm.at[ex], wu_f32.at[slot], sems.at[0, slot]),
                pltpu.make_async_copy(wd_hbm.at[ex], wd_f32.at[slot], sems.at[1, slot]))

    @pl.when(bf_ref[j] != 0)
    def _():
        @pl.when(j == 0)
        def _():
            for c in fetch(e, s):
                c.start()

        for c in fetch(e, s):
            c.wait()

        @pl.when(bn_ref[j] >= 0)
        def _():
            for c in fetch(bn_ref[j], 1 - s):
                c.start()

        def cast(c, carry):
            sl = pl.ds(pl.multiple_of(c * chunk, chunk), chunk)
            wu_bf[sl, :] = wu_f32[s, sl, :].astype(BF16)
            wd_bf[sl, :] = wd_f32[s, sl, :].astype(BF16)
            return carry

        lax.fori_loop(0, wu_hbm.shape[1] // chunk, cast, 0)

    @pl.when(bv_ref[j] != 0)
    def _():
        for g in range(bm // EXPERT_ROWS):
            tiles = pl.ds(g * EXPERT_ROWS * V7X_SUBLANES, EXPERT_ROWS * V7X_SUBLANES)
            x = _from_row_tiles(xs_ref.at[tiles, :], EXPERT_ROWS).astype(BF16)
            glu = jnp.dot(x, wu_bf[:, :d_ff], preferred_element_type=F32) + bu_ref[0, :, :d_ff]
            lin = jnp.dot(x, wu_bf[:, d_ff:], preferred_element_type=F32) + bu_ref[0, :, d_ff:]
            glu = jnp.minimum(glu, SWIGLU_LIMIT)
            lin = jnp.clip(lin, -SWIGLU_LIMIT, SWIGLU_LIMIT)
            act = glu * (1.0 / (1.0 + jnp.exp(-SWIGLU_ALPHA * glu))) * (lin + 1.0)
            _to_row_tiles(ys_ref.at[tiles, :],
                          jnp.dot(act.astype(BF16), wd_bf[...], preferred_element_type=F32) + bd_ref[0])

    @pl.when(bv_ref[j] == 0)
    def _():
        ys_ref[...] = jnp.zeros_like(ys_ref)


def _gmm(xs, blk_e, blk_valid, blk_first, blk_slot, blk_next, w_up, b_up, w_down, b_down, *, bm):
    n_rows = xs.shape[0] // V7X_SUBLANES
    n_e, dm, d_up = w_up.shape
    d_ff = w_down.shape[1]
    assert d_ff == dm
    row_tiles = pl.BlockSpec((bm * V7X_SUBLANES, V7X_LANES), lambda j, *_: (j, 0))
    grid_spec = pltpu.PrefetchScalarGridSpec(
        num_scalar_prefetch=5,
        grid=(n_rows // bm,),
        in_specs=[row_tiles,
                  pl.BlockSpec(memory_space=pl.ANY),
                  pl.BlockSpec((1, 1, d_up), lambda j, be, *_: (be[j], 0, 0)),
                  pl.BlockSpec(memory_space=pl.ANY),
                  pl.BlockSpec((1, 1, dm), lambda j, be, *_: (be[j], 0, 0))],
        out_specs=row_tiles,
        scratch_shapes=[pltpu.VMEM((2, dm, d_up), F32), pltpu.VMEM((2, d_ff, dm), F32),
                        pltpu.VMEM((dm, d_up), BF16), pltpu.VMEM((d_ff, dm), BF16),
                        pltpu.SemaphoreType.DMA((2, 2))],
    )
    return pl.pallas_call(
        functools.partial(_gmm_body, bm=bm),
        grid_spec=grid_spec,
        out_shape=jax.ShapeDtypeStruct(xs.shape, F32),
        compiler_params=_cparams(("arbitrary",), 56),
        name="expert_mlp",
    )(blk_e, blk_valid, blk_first, blk_slot, blk_next, xs, w_up, b_up.reshape(n_e, 1, d_up), w_down,
      b_down.reshape(n_e, 1, dm))


def _combine_body(rt_ref, rtn_ref, ys_hbm, rg_ref, x1_ref, g2_ref, lg_ref, lb_ref, y_ref, rows_ref, sems,
                  *, nb, tm, alpha, n_steps):
    n_tok = nb * tm
    i = pl.program_id(0)
    slot = i % 2

    def issue(dest_ref, s):
        def start_copy(r, kk, row):
            pltpu.make_async_copy(_row_tile(ys_hbm, row), _row_tile(rows_ref.at[s, kk], r),
                                  sems.at[s]).start(priority=kk % DMA_QUEUES)

        _issue_rows(dest_ref, n_tok, start_copy)

    @pl.when(i == 0)
    def _():
        issue(rt_ref, 0)

    @pl.when(i + 1 < n_steps)
    def _():
        issue(rtn_ref, 1 - slot)

    for kk in range(TOP_K):
        pltpu.make_async_copy(ys_hbm.at[pl.ds(0, n_tok * V7X_SUBLANES), :], rows_ref.at[slot, kk],
                              sems.at[slot]).wait()

    gates = rg_ref[...]
    ffn = gates[:, 0:1] * _from_row_tiles(rows_ref, n_tok, (slot, 0))
    for kk in range(1, TOP_K):
        ffn = ffn + gates[:, kk:kk + 1] * _from_row_tiles(rows_ref, n_tok, (slot, kk))
    dm = ffn.shape[-1]
    y = alpha * x1_ref[...].reshape(nb, tm, dm) + (1.0 + g2_ref[...]) * ffn.reshape(nb, tm, dm)
    y_ref[...] = _layer_norm(y, lg_ref[...], lb_ref[...])


def _combine(ys, dest_sm, route_g, x1_all, ada3, ln2_g, ln2_b, *, bsz, seq, nb, tm, alpha, row0):
    dm = x1_all.shape[1]
    n_tok = nb * tm
    assert row0 % n_tok == 0 and dest_sm.shape[2] == n_tok * TOP_K and n_tok % ISSUE_TOKENS == 0
    blk0 = row0 // n_tok
    st = seq // tm
    n_steps = (bsz // nb) * st
    dest = lambda off: pl.BlockSpec((1, 1, n_tok * TOP_K),
                                    lambda i: (blk0 + jnp.minimum(i + off, n_steps - 1), 0, 0),
                                    memory_space=pltpu.SMEM)
    return pl.pallas_call(
        functools.partial(_combine_body, nb=nb, tm=tm, alpha=alpha, n_steps=n_steps),
        grid=(n_steps,),
        in_specs=[dest(0), dest(1),
                  pl.BlockSpec(memory_space=pl.ANY),
                  pl.BlockSpec((n_tok, V7X_LANES), lambda i: (blk0 + i, 0)),
                  pl.BlockSpec((n_tok, dm), lambda i: (blk0 + i, 0)),
                  pl.BlockSpec((nb, 1, dm), lambda i: (i // st, 0, _GATE2)),
                  pl.BlockSpec((1, dm), lambda i: (0, 0)),
                  pl.BlockSpec((1, dm), lambda i: (0, 0))],
        out_specs=pl.BlockSpec((nb, tm, dm), lambda i: (i // st, i % st, 0)),
        scratch_shapes=[pltpu.VMEM((2, TOP_K, n_tok * V7X_SUBLANES, V7X_LANES), F32),
                        pltpu.SemaphoreType.DMA((2,))],
        out_shape=jax.ShapeDtypeStruct((bsz, seq, dm), F32),
        compiler_params=_cparams(("arbitrary",), 48),
        name="combine_prompt" if row0 == 0 else "combine_sample",
    )(dest_sm, dest_sm, ys, route_g, x1_all, ada3, ln2_g.reshape(1, dm), ln2_b.reshape(1, dm))


def _layer(xp, xs, cache_k, cache_v, conv_past, cp, cs, w_ada, b_ada, w_in, conv_w, norm_attn_g, norm_conv_g,
           w_o, ln1_g, ln1_b, bias_tabs, router_w, router_b, w_up, b_up, w_down, b_down, ln2_g, ln2_b, alpha):
    bp, sp, dm = xp.shape
    bs, ts, _ = xs.shape
    n_p, n_s = bp * sp, bs * ts
    n_tok = n_p + n_s
    assert n_p % TOK_TILE == 0 and n_s == TOK_TILE and sp % ROW_TILE == 0 and sp % OUT_TILE == 0
    bias_p, bias_c, bias_n = bias_tabs

    ada = _ada(jnp.concatenate([cp, cs], axis=0), w_ada, b_ada)
    ada_p = ada[:bp].reshape(bp, 1, -1)
    ada_s = ada[bp:].reshape(bs, 1, -1)
    w_in_bf = w_in.astype(BF16)
    w_o_bf = w_o.astype(BF16)

    qp, kp, vp, convp, cstp, kpt, vpt = _inproj(xp, ada_p, w_in_bf, conv_w, norm_conv_g, None, nb=1, tm=ROW_TILE)
    attn_p = _attn_prompt(qp, kp, vp, bias_p)
    x1_all, u2_all = _outproj(attn_p, convp, xp, ada_p, w_o_bf, norm_attn_g, ln1_g, ln1_b,
                              nb=1, tm=OUT_TILE, alpha=alpha, n_total=n_tok, row0=0)
    qs, ks, vs, convs, csts = _inproj(xs, ada_s, w_in_bf, conv_w, norm_conv_g, conv_past, nb=bs, tm=ts)
    heads = lambda a: jnp.transpose(a.reshape(bs, ts, N_HEADS, HEAD_DIM), (0, 2, 1, 3))
    n_past = cache_k.shape[1]
    feat_major = lambda c: jnp.transpose(c.reshape(bs, n_past, D_ATTN), (0, 2, 1))
    attn_s = _attn_sample(heads(qs), heads(ks), heads(vs), feat_major(cache_k), feat_major(cache_v), bias_c, bias_n)
    attn_s = jnp.transpose(attn_s, (0, 2, 1, 3)).reshape(bs, ts, D_ATTN)
    x1_all, u2_all = _outproj(attn_s, convs, xs, ada_s, w_o_bf, norm_attn_g, ln1_g, ln1_b,
                              nb=bs, tm=ts, alpha=alpha, n_total=n_tok, row0=n_p, prev=(x1_all, u2_all))

    route_i, route_r, route_g, counts = _router(u2_all, router_w, router_b, tm=TOK_TILE)
    counts = counts[0, :N_EXPERTS].astype(I32)
    padded = ((counts + EXPERT_BLK - 1) // EXPERT_BLK) * EXPERT_BLK
    pad_ends = jnp.cumsum(padded)
    pad_starts = (pad_ends - padded).astype(I32)
    n_blocks = n_tok * TOP_K // EXPERT_BLK + N_EXPERTS
    n_rows = n_blocks * EXPERT_BLK
    blk_row = jnp.arange(n_blocks, dtype=I32) * EXPERT_BLK
    blk_valid = (blk_row < pad_ends[-1]).astype(I32)
    blk_e = jnp.sum((blk_row[:, None] >= pad_ends[None, :]).astype(I32), axis=1)
    last_e = jnp.sum((pad_ends[-1] - 1 >= pad_ends).astype(I32))
    blk_e = jnp.where(blk_valid != 0, blk_e, last_e).astype(I32)
    ar = jnp.arange(N_EXPERTS, dtype=I32)
    has_rows = padded > 0
    slot_of_e = (jnp.cumsum(has_rows.astype(I32)) - 1) % 2
    next_of_e = jnp.min(jnp.where((ar[None, :] > ar[:, None]) & has_rows[None, :], ar[None, :], N_EXPERTS), axis=1)
    next_of_e = jnp.where(next_of_e < N_EXPERTS, next_of_e, -1)
    is_e = blk_e[:, None] == ar[None, :]
    pick = lambda per_expert: jnp.sum(jnp.where(is_e, per_expert[None, :], 0), axis=1).astype(I32)
    blk_first = (blk_valid * (blk_row == pick(pad_starts)).astype(I32)).astype(I32)
    blk_slot = pick(slot_of_e)
    blk_next = pick(next_of_e)
    dense = lambda a: a[:, :TOP_K].reshape(n_tok * TOP_K // V7X_LANES, V7X_LANES)
    dest = _dest_rows(dense(route_i), dense(route_r), pad_starts)
    dest_sm = dest.reshape(n_tok // TOK_TILE, 1, TOK_TILE * TOP_K)

    run_start = jnp.concatenate([pad_starts + counts, pad_ends[-1:]])
    run_len = jnp.concatenate([padded - counts, n_rows - pad_ends[-1:]])
    run_k0 = jnp.cumsum(run_len) - run_len
    k = jnp.arange(n_rows - n_tok * TOP_K, dtype=I32)[:, None]
    in_run = (k >= run_k0[None, :]) & (k < (run_k0 + run_len)[None, :])
    fill_rows = jnp.sum(jnp.where(in_run, run_start[None, :] + k - run_k0[None, :], 0), axis=1).astype(I32)
    x_sorted = _dispatch(u2_all, dest_sm, fill_rows, n_rows, tm=TOK_TILE)
    y_sorted = _gmm(x_sorted, blk_e, blk_valid, blk_first, blk_slot, blk_next, w_up, b_up, w_down, b_down,
                    bm=EXPERT_BLK)
    yp = _combine(y_sorted, dest_sm, route_g, x1_all, ada_p, ln2_g, ln2_b,
                  bsz=bp, seq=sp, nb=1, tm=TOK_TILE, alpha=alpha, row0=0)
    ys_out = _combine(y_sorted, dest_sm, route_g, x1_all, ada_s, ln2_g, ln2_b,
                      bsz=bs, seq=ts, nb=bs, tm=ts, alpha=alpha, row0=n_p)
    return yp, ys_out, kpt, vpt, cstp, ks, vs, csts


def kernel(x_prompt, x_sample, cache_k, cache_v, state_conv, c_prompt, c_sample, w_ada, b_ada, w_in, conv_w,
           norm_attn_g, norm_conv_g, w_o, ln1_g, ln1_b, rel_bias, router_w, router_b, w_up, b_up, w_down, b_down,
           ln2_g, ln2_b):
    depth = w_ada.shape[0]
    alpha = (2 * depth) ** 0.25
    xp, xs = x_prompt, x_sample
    bp, sp, _ = xp.shape
    bs, ts, _ = xs.shape
    n_keep = min(BRANCHES[-1][0], sp)
    bias_tabs = _bias_tables(rel_bias, ts, cache_k.shape[2])
    outs = [[] for _ in range(6)]
    for l in range(depth):
        xp, xs, kp, vp, cstp, ks, vs, csts = _layer(
            xp, xs, cache_k[l], cache_v[l], state_conv[l], c_prompt, c_sample, w_ada[l], b_ada[l], w_in[l],
            conv_w[l], norm_attn_g[l], norm_conv_g[l], w_o[l], ln1_g[l], ln1_b[l], bias_tabs, router_w[l],
            router_b[l], w_up[l], b_up[l], w_down[l], b_down[l], ln2_g[l], ln2_b[l], alpha)
        kp = jnp.transpose(kp, (0, 2, 1)).reshape(bp, sp, N_HEADS, HEAD_DIM)[:, -n_keep:]
        vp = jnp.transpose(vp, (0, 2, 1)).reshape(bp, sp, N_HEADS, HEAD_DIM)[:, -n_keep:]
        for lst, val in zip(outs, (kp, vp, cstp, ks.reshape(bs, ts, N_HEADS, HEAD_DIM),
                                   vs.reshape(bs, ts, N_HEADS, HEAD_DIM), csts)):
            lst.append(val)
    return (xp, xs) + tuple(jnp.stack(o) for o in outs)
```

```python
import functools
import math

import numpy as np
import jax
import jax.numpy as jnp
from jax import lax
from jax.experimental import pallas as pl
from jax.experimental.pallas import tpu as pltpu

F32 = jnp.float32
BF16 = jnp.bfloat16
I32 = jnp.int32

HEAD_DIM = 64
N_HEADS = 12
D_ATTN = N_HEADS * HEAD_DIM
D_CONV = 256
CONV_WIDTH = 3
BRANCHES = ((128, 1), (512, 4), (2048, 16))
NUM_BUCKETS = 32
MAX_DISTANCE = 2048
N_EXPERTS = 32
TOP_K = 4
SWIGLU_LIMIT = 7.0
SWIGLU_ALPHA = 1.702
LN_EPS = 1e-5
RMS_EPS = 1e-6
NEG_INF = -1e30
LOG2E = math.log2(math.e)

V7X_LANES = 128
V7X_SUBLANES = 8
V7X_VMEM_BYTES = 64 * 1024 * 1024
DMA_QUEUES = 2

Q_BLK = 128
PAIR = 2 * HEAD_DIM
N_PAIRS = N_HEADS // 2
TOK_TILE = 256
ROW_TILE = 512
OUT_TILE = 512
EXPERT_BLK = 256
EXPERT_ROWS = 256
ATTN_UNROLL = 16


def _cparams(sem, vmem_mb):
    return pltpu.CompilerParams(dimension_semantics=sem, vmem_limit_bytes=vmem_mb * 1024 * 1024)


def _t5_bucket_np(dist):
    dist = np.asarray(dist, np.int64)
    max_exact = NUM_BUCKETS // 2
    df = np.maximum(dist, max_exact).astype(np.float32)
    large = max_exact + (np.log(df / np.float32(max_exact)) / np.float32(math.log(MAX_DISTANCE / max_exact))
                         * np.float32(NUM_BUCKETS - max_exact)).astype(np.int32)
    return np.where(dist < max_exact, dist, np.minimum(large, NUM_BUCKETS - 1)).astype(np.int32)


def _prompt_bucket_index():
    a = np.arange(Q_BLK)[:, None]
    c = np.arange(2 * Q_BLK)[None, :]
    step = Q_BLK + a - c
    valid = (step >= 0) & (step <= Q_BLK)
    out = []
    for _, d in BRANCHES:
        out.append(np.where(valid, _t5_bucket_np(np.clip(step, 0, Q_BLK) * d), -1))
    return np.stack(out).astype(np.int32)


def _sample_bucket_index(n_new, n_past):
    t = np.arange(n_new)[:, None]
    out_c, out_n = [], []
    for w, d in BRANCHES:
        dist_c = n_past + t - np.arange(n_past)[None, :]
        dist_n = t - np.arange(n_new)[None, :]
        for dist, out in ((dist_c, out_c), (dist_n, out_n)):
            valid = (dist >= 0) & (dist <= w) & (dist % d == 0)
            out.append(np.where(valid, _t5_bucket_np(np.clip(dist, 0, w)), -1))
    return np.stack(out_c).astype(np.int32), np.stack(out_n).astype(np.int32)


def _bias_body(rb_ref, idx_ref, out_ref, *, mult, hide_cols):
    idx = idx_ref[0]
    col = lax.broadcasted_iota(I32, idx.shape, 1)
    for h in range(N_HEADS):
        acc = jnp.where(idx < 0, NEG_INF, 0.0).astype(F32)
        for b in range(NUM_BUCKETS):
            acc = acc + jnp.where(idx == b, rb_ref[b * N_HEADS + h] * mult, 0.0)
        out_ref[0, 0, h] = acc
        if hide_cols:
            out_ref[0, 1, h] = jnp.where(col < hide_cols, NEG_INF, acc)


def _bias_expand(rel_bias, idx_np, name, mult=1.0, hide_cols=0):
    nbr, r, c = idx_np.shape
    nvar = 2 if hide_cols else 1
    return pl.pallas_call(
        functools.partial(_bias_body, mult=mult, hide_cols=hide_cols),
        grid=(nbr,),
        in_specs=[pl.BlockSpec(memory_space=pltpu.SMEM),
                  pl.BlockSpec((1, r, c), lambda i: (i, 0, 0))],
        out_specs=pl.BlockSpec((1, nvar, N_HEADS, r, c), lambda i: (i, 0, 0, 0, 0)),
        out_shape=jax.ShapeDtypeStruct((nbr, nvar, N_HEADS, r, c), F32),
        name=name,
    )(rel_bias.reshape(-1), jnp.asarray(idx_np))


def _bias_tables(rel_bias, n_new, n_past):
    bias_p = _bias_expand(rel_bias, _prompt_bucket_index(), "bias_prompt", mult=LOG2E, hide_cols=Q_BLK)
    bias_p = bias_p.reshape(len(BRANCHES), 2, N_PAIRS, 2 * Q_BLK, 2 * Q_BLK)
    ic, inw = _sample_bucket_index(n_new, n_past)
    plain = lambda idx, name: _bias_expand(rel_bias, idx, name).reshape((idx.shape[0], N_HEADS) + idx.shape[1:])
    return bias_p, plain(ic, "bias_cache"), plain(inw, "bias_new")


def _ada_body(c_ref, w_ref, b_ref, o_ref):
    c = c_ref[...]
    s = c * (1.0 / (1.0 + jnp.exp(-c)))
    o_ref[...] = jnp.dot(s.astype(BF16), w_ref[...].astype(BF16), preferred_element_type=F32) + b_ref[...]


def _ada(c_all, w_ada, b_ada):
    n, dm = c_all.shape
    n_out = w_ada.shape[1]
    tn = dm
    return pl.pallas_call(
        _ada_body,
        grid=(n_out // tn,),
        in_specs=[pl.BlockSpec((n, dm), lambda j: (0, 0)),
                  pl.BlockSpec((dm, tn), lambda j: (0, j)),
                  pl.BlockSpec((1, tn), lambda j: (0, j))],
        out_specs=pl.BlockSpec((n, tn), lambda j: (0, j)),
        out_shape=jax.ShapeDtypeStruct((n, n_out), F32),
        compiler_params=_cparams(("parallel",), 32),
        name="ada",
    )(c_all, w_ada, b_ada.reshape(1, n_out))


_SHIFT1, _SCALE1, _GATE1, _SHIFT2, _SCALE2, _GATE2 = range(6)


def _inproj_body(*refs, nb, tm, carry):
    if carry:
        (x_ref, sh_ref, sc_ref, w_ref, cw_ref, ng_ref, past_ref,
         q_ref, k_ref, v_ref, conv_ref, cst_ref, kt_ref, vt_ref, zz_ref) = refs
    else:
        (x_ref, sh_ref, sc_ref, w_ref, cw_ref, ng_ref, past_ref,
         q_ref, k_ref, v_ref, conv_ref, cst_ref, zz_ref) = refs
    dm = x_ref.shape[-1]
    rows = nb * tm
    u = x_ref[...] * (1.0 + sc_ref[...]) + sh_ref[...]
    u = u.reshape(rows, dm).astype(BF16)

    def proj(lo, width):
        return jnp.dot(u, w_ref[:, lo:lo + width], preferred_element_type=F32)

    q_ref[...] = proj(0, D_ATTN).reshape(nb, tm, D_ATTN)
    k = proj(D_ATTN, D_ATTN)
    v = proj(2 * D_ATTN, D_ATTN)
    k_ref[...] = k.reshape(nb, tm, D_ATTN)
    v_ref[...] = v.reshape(nb, tm, D_ATTN)
    if carry:
        kt_ref[0] = k.T
        vt_ref[0] = v.T
    gb = proj(3 * D_ATTN, D_CONV)
    gc = proj(3 * D_ATTN + D_CONV, D_CONV)
    hh = proj(3 * D_ATTN + 2 * D_CONV, D_CONV)
    z = (gc * hh).reshape(nb, tm, D_CONV)

    if carry:
        s = pl.program_id(1)

        @pl.when(s == 0)
        def _():
            zz_ref[:, 0:V7X_SUBLANES, :] = jnp.zeros((nb, V7X_SUBLANES, D_CONV), F32)

        @pl.when(s > 0)
        def _():
            zz_ref[:, 0:V7X_SUBLANES, :] = zz_ref[:, tm:tm + V7X_SUBLANES, :]
    else:
        zz_ref[:, V7X_SUBLANES - 2:V7X_SUBLANES, :] = past_ref[...]
    zz_ref[:, V7X_SUBLANES:, :] = z

    cw = cw_ref[...]
    yc = (cw[0:1, :] * zz_ref[:, V7X_SUBLANES - 2:V7X_SUBLANES - 2 + tm, :]
          + cw[1:2, :] * zz_ref[:, V7X_SUBLANES - 1:V7X_SUBLANES - 1 + tm, :]
          + cw[2:3, :] * z)
    g = gb.reshape(nb, tm, D_CONV) * yc
    ms = jnp.mean(g * g, axis=-1, keepdims=True)
    conv_ref[...] = g * lax.rsqrt(ms + RMS_EPS) * ng_ref[...]
    cst_ref[...] = zz_ref[:, tm + V7X_SUBLANES - 2:tm + V7X_SUBLANES, :]


def _inproj(x, ada3, w_in_bf, conv_w, norm_conv_g, past, *, nb, tm):
    bsz, seq, dm = x.shape
    carry = past is None
    if carry:
        past = jnp.zeros((bsz, CONV_WIDTH - 1, D_CONV), F32)
    grid = (bsz // nb, seq // tm)
    d_in = w_in_bf.shape[1]
    body = functools.partial(_inproj_body, nb=nb, tm=tm, carry=carry)
    tile = lambda width: pl.BlockSpec((nb, tm, width), lambda b, s: (b, s, 0))
    in_specs = [tile(dm),
                pl.BlockSpec((nb, 1, dm), lambda b, s: (b, 0, _SHIFT1)),
                pl.BlockSpec((nb, 1, dm), lambda b, s: (b, 0, _SCALE1)),
                pl.BlockSpec((dm, d_in), lambda b, s: (0, 0)),
                pl.BlockSpec((CONV_WIDTH, D_CONV), lambda b, s: (0, 0)),
                pl.BlockSpec((1, D_CONV), lambda b, s: (0, 0)),
                pl.BlockSpec((nb, CONV_WIDTH - 1, D_CONV), lambda b, s: (b, 0, 0))]
    args = [x, ada3, ada3, w_in_bf, conv_w, norm_conv_g.reshape(1, D_CONV), past]
    out_specs = [tile(D_ATTN), tile(D_ATTN), tile(D_ATTN), tile(D_CONV),
                 pl.BlockSpec((nb, CONV_WIDTH - 1, D_CONV), lambda b, s: (b, 0, 0))]
    out_shape = ([jax.ShapeDtypeStruct((bsz, seq, D_ATTN), F32)] * 3
                 + [jax.ShapeDtypeStruct((bsz, seq, D_CONV), F32),
                    jax.ShapeDtypeStruct((bsz, CONV_WIDTH - 1, D_CONV), F32)])
    if carry:
        assert nb == 1
        out_specs += [pl.BlockSpec((1, D_ATTN, tm), lambda b, s: (b, 0, s))] * 2
        out_shape += [jax.ShapeDtypeStruct((bsz, D_ATTN, seq), F32)] * 2
    return pl.pallas_call(
        body,
        grid=grid,
        in_specs=in_specs,
        out_specs=out_specs,
        out_shape=out_shape,
        scratch_shapes=[pltpu.VMEM((nb, tm + V7X_SUBLANES, D_CONV), F32)],
        compiler_params=_cparams(("parallel", "arbitrary"), 56),
        name="inproj_prompt" if carry else "inproj_sample",
    )(*args)


def _attn_prompt_body(q_ref, k_ref, v_ref, bias_ref, o_ref, ob_ref, lb_ref, *, seq):
    scale = HEAD_DIM ** -0.5 * LOG2E
    lane = lax.broadcasted_iota(I32, (Q_BLK, PAIR), 1)
    even = lane < HEAD_DIM
    ones = jnp.ones((Q_BLK, PAIR), BF16)

    def rows(ref, start, d):
        if d == 1:
            return ref[pl.ds(start, Q_BLK), :]
        return ref[pl.ds(start, Q_BLK, stride=d), :]

    def store(ref, br, start, d, val):
        if d == 1:
            ref[br, pl.ds(start, Q_BLK), :] = val
        else:
            ref[br, pl.ds(start, Q_BLK, stride=d), :] = val

    for br, (_, d) in enumerate(BRANCHES):
        nblk = seq // (d * Q_BLK)

        def block(it, carry, br=br, d=d, nblk=nblk):
            r = it // nblk
            i = it - r * nblk
            cur0 = r + d * Q_BLK * i
            qf = rows(q_ref, cur0, d) * scale
            qs = jnp.concatenate([jnp.where(even, qf, 0.0), jnp.where(even, 0.0, qf)], axis=0).astype(BF16)
            kc = rows(k_ref, cur0, d).astype(BF16)
            vc = rows(v_ref, cur0, d).astype(BF16)
            if nblk == 1:
                s = lax.dot_general(qs, kc, (((1,), (1,)), ((), ())), preferred_element_type=F32)
                s = s + bias_ref[br, 0, 0, :, Q_BLK:]
                vaug = jnp.concatenate([vc, ones], axis=1)
            else:
                prev0 = r + d * Q_BLK * jnp.maximum(i - 1, 0)
                kp = rows(k_ref, prev0, d).astype(BF16)
                vp = rows(v_ref, prev0, d).astype(BF16)
                kk = jnp.concatenate([kp, kc], axis=0)
                s = lax.dot_general(qs, kk, (((1,), (1,)), ((), ())), preferred_element_type=F32)
                s = s + bias_ref[br, jnp.where(i == 0, 1, 0), 0]
                vaug = jnp.concatenate([jnp.concatenate([vp, vc], axis=0),
                                        jnp.concatenate([ones, ones], axis=0)], axis=1)
            m = jnp.max(s, axis=-1, keepdims=True)
            p = jnp.exp2(s - m).astype(BF16)
            out = jnp.dot(p, vaug, preferred_element_type=F32)
            num, den = out[:, :PAIR], out[:, PAIR:]
            o = num / den
            lse = m + jnp.log2(den)
            store(ob_ref, br, cur0, d, jnp.where(even, o[:Q_BLK], o[Q_BLK:]))
            store(lb_ref, br, cur0, d, jnp.where(even, lse[:Q_BLK], lse[Q_BLK:]))
            return carry

        lax.fori_loop(0, d * nblk, block, 0, unroll=ATTN_UNROLL)

    mt = 2 * Q_BLK

    def merge(i, carry):
        sl = pl.ds(pl.multiple_of(i * mt, mt), mt)
        l0, l1, l2 = lb_ref[0, sl, :], lb_ref[1, sl, :], lb_ref[2, sl, :]
        mx = jnp.maximum(jnp.maximum(l0, l1), l2)
        w0, w1, w2 = jnp.exp2(l0 - mx), jnp.exp2(l1 - mx), jnp.exp2(l2 - mx)
        acc = w0 * ob_ref[0, sl, :] + w1 * ob_ref[1, sl, :] + w2 * ob_ref[2, sl, :]
        o_ref[sl, :] = acc / (w0 + w1 + w2)
        return carry

    lax.fori_loop(0, seq // mt, merge, 0)


def _attn_prompt(q, k, v, bias_p):
    bsz, seq, _ = q.shape
    assert seq % (BRANCHES[-1][1] * Q_BLK) == 0
    body = functools.partial(_attn_prompt_body, seq=seq)
    col = pl.BlockSpec((None, seq, PAIR), lambda b, j: (b, 0, j))
    return pl.pallas_call(
        body,
        grid=(bsz, N_PAIRS),
        in_specs=[col, col, col,
                  pl.BlockSpec((len(BRANCHES), 2, 1, 2 * Q_BLK, 2 * Q_BLK), lambda b, j: (0, 0, j, 0, 0))],
        out_specs=col,
        out_shape=jax.ShapeDtypeStruct((bsz, seq, D_ATTN), F32),
        scratch_shapes=[pltpu.VMEM((len(BRANCHES), seq, PAIR), F32),
                        pltpu.VMEM((len(BRANCHES), seq, PAIR), F32)],
        compiler_params=_cparams(("parallel", "parallel"), 48),
        name="attn_prompt",
    )(q, k, v, bias_p)


def _attn_sample_body(q_ref, kn_ref, vn_ref, ckt_ref, cvt_ref, bc_ref, bn_ref, o_ref):
    scale = HEAD_DIM ** -0.5
    nt = (((1,), (1,)), ((), ()))
    nbr = len(BRANCHES)
    for h in range(N_HEADS):
        rows = slice(h * HEAD_DIM, (h + 1) * HEAD_DIM)
        qh = (q_ref[h] * scale).astype(BF16)
        s_c = jnp.dot(qh, ckt_ref[rows, :].astype(BF16), preferred_element_type=F32)
        s_n = lax.dot_general(qh, kn_ref[h].astype(BF16), nt, preferred_element_type=F32)
        ps, lses, dens = [], [], []
        for br in range(nbr):
            lc = s_c + bc_ref[br, h]
            ln = s_n + bn_ref[br, h]
            m = jnp.maximum(jnp.max(lc, axis=-1, keepdims=True), jnp.max(ln, axis=-1, keepdims=True))
            pc = jnp.exp(lc - m)
            pn = jnp.exp(ln - m)
            den = jnp.sum(pc, axis=-1, keepdims=True) + jnp.sum(pn, axis=-1, keepdims=True)
            ps.append((pc, pn))
            dens.append(den)
            lses.append(m + jnp.log(den))
        mx = jnp.maximum(jnp.maximum(lses[0], lses[1]), lses[2])
        ws = [jnp.exp(l - mx) for l in lses]
        wsum = ws[0] + ws[1] + ws[2]
        coefs = [ws[br] / (wsum * dens[br]) for br in range(nbr)]
        p_c = coefs[0] * ps[0][0] + coefs[1] * ps[1][0] + coefs[2] * ps[2][0]
        p_n = coefs[0] * ps[0][1] + coefs[1] * ps[1][1] + coefs[2] * ps[2][1]
        o = lax.dot_general(p_c.astype(BF16), cvt_ref[rows, :].astype(BF16), nt, preferred_element_type=F32)
        o_ref[h] = o + jnp.dot(p_n.astype(BF16), vn_ref[h].astype(BF16), preferred_element_type=F32)


def _attn_sample(q, k_new, v_new, cache_kt, cache_vt, bias_c, bias_n):
    bsz, _, n_new, _ = q.shape
    n_past = cache_kt.shape[2]
    new = pl.BlockSpec((None, N_HEADS, n_new, HEAD_DIM), lambda b: (b, 0, 0, 0))
    cache = pl.BlockSpec((None, D_ATTN, n_past), lambda b: (b, 0, 0))
    return pl.pallas_call(
        _attn_sample_body,
        grid=(bsz,),
        in_specs=[new, new, new, cache, cache,
                  pl.BlockSpec(bias_c.shape, lambda b: (0, 0, 0, 0)), pl.BlockSpec(bias_n.shape, lambda b: (0, 0, 0, 0))],
        out_specs=new,
        out_shape=jax.ShapeDtypeStruct((bsz, N_HEADS, n_new, HEAD_DIM), F32),
        compiler_params=_cparams(("parallel",), 48),
        name="attn_sample",
    )(q, k_new, v_new, cache_kt, cache_vt, bias_c, bias_n)


def _layer_norm(y, g, b):
    mu = jnp.mean(y, axis=-1, keepdims=True)
    c = y - mu
    var = jnp.mean(c * c, axis=-1, keepdims=True)
    return c * lax.rsqrt(var + LN_EPS) * g + b


def _to_row_tiles(ref, val):
    rows, dm = val.shape
    assert dm == V7X_SUBLANES * V7X_LANES
    for c in range(V7X_SUBLANES):
        ref[pl.ds(c, rows, stride=V7X_SUBLANES), :] = val[:, c * V7X_LANES:(c + 1) * V7X_LANES]


def _from_row_tiles(ref, rows, lead=()):
    chunks = [ref[lead + (pl.ds(c, rows, stride=V7X_SUBLANES), slice(None))] for c in range(V7X_SUBLANES)]
    return jnp.concatenate(chunks, axis=1)


def _outproj_body(*refs, nb, tm, alpha, aliased, n_main):
    if aliased:
        refs = refs[2:]
    (a_ref, c_ref, x_ref, g1_ref, sh2_ref, sc2_ref, wa_ref, wc_ref, ng_ref, lg_ref, lb_ref, x1_ref, u2_ref) = refs
    rows = nb * tm

    @pl.when(pl.program_id(0) < n_main)
    def _():
        a = a_ref[...]
        ms = jnp.mean(a * a, axis=-1, keepdims=True)
        an = (a * lax.rsqrt(ms + RMS_EPS) * ng_ref[...]).reshape(rows, D_ATTN).astype(BF16)
        cn = c_ref[...].reshape(rows, D_CONV).astype(BF16)
        mix = (jnp.dot(an, wa_ref[...], preferred_element_type=F32)
               + jnp.dot(cn, wc_ref[...], preferred_element_type=F32))
        dm = mix.shape[-1]
        y = alpha * x_ref[...] + (1.0 + g1_ref[...]) * mix.reshape(nb, tm, dm)
        x1 = _layer_norm(y, lg_ref[...], lb_ref[...])
        x1_ref[...] = x1.reshape(rows, dm)
        u2_ref[...] = (x1 * (1.0 + sc2_ref[...]) + sh2_ref[...]).reshape(rows, dm)

    @pl.when(pl.program_id(0) >= n_main)
    def _():
        x1_ref[...] = jnp.zeros_like(x1_ref)
        u2_ref[...] = jnp.zeros_like(u2_ref)


def _outproj(attn, conv, x, ada3, w_o_bf, norm_attn_g, ln1_g, ln1_b, *, nb, tm, alpha, n_total, row0, prev=None):
    bsz, seq, dm = x.shape
    rows = nb * tm
    assert row0 % rows == 0
    st = seq // tm
    blk0 = row0 // rows
    n_main = (bsz // nb) * st
    aliased = prev is not None
    n_steps = n_main if aliased else pl.cdiv(n_total, rows)

    def bs_of(i):
        j = jnp.minimum(i, n_main - 1)
        return j // st, j % st

    tile = lambda width: pl.BlockSpec((nb, tm, width), lambda i: bs_of(i) + (0,))
    ada = lambda chunk: pl.BlockSpec((nb, 1, dm), lambda i: (bs_of(i)[0], 0, chunk))
    const = lambda shape: pl.BlockSpec(shape, lambda i: (0,) * len(shape))
    out = pl.BlockSpec((rows, dm), lambda i: (blk0 + i, 0))
    body = functools.partial(_outproj_body, nb=nb, tm=tm, alpha=alpha, aliased=aliased, n_main=n_main)
    in_specs = [tile(D_ATTN), tile(D_CONV), tile(dm), ada(_GATE1), ada(_SHIFT2), ada(_SCALE2),
                const((D_ATTN, dm)), const((D_CONV, dm)), const((1, D_ATTN)), const((1, dm)), const((1, dm))]
    args = [attn, conv, x, ada3, ada3, ada3, w_o_bf[:D_ATTN], w_o_bf[D_ATTN:],
            norm_attn_g.reshape(1, D_ATTN), ln1_g.reshape(1, dm), ln1_b.reshape(1, dm)]
    kwargs = {}
    if aliased:
        in_specs = [pl.BlockSpec(memory_space=pl.ANY)] * 2 + in_specs
        args = list(prev) + args
        kwargs["input_output_aliases"] = {0: 0, 1: 1}
    return pl.pallas_call(
        body,
        grid=(n_steps,),
        in_specs=in_specs,
        out_specs=[out, out],
        out_shape=[jax.ShapeDtypeStruct((n_total, dm), F32)] * 2,
        compiler_params=_cparams(("parallel",), 48),
        name="outproj_sample" if aliased else "outproj_prompt",
        **kwargs,
    )(*args)


def _router_body(u_ref, w_ref, b_ref, ri_ref, rr_ref, rg_ref, cnt_ref, run_ref, *, tm):
    i = pl.program_id(0)

    @pl.when(i == 0)
    def _():
        run_ref[...] = jnp.zeros_like(run_ref)

    u = u_ref[...]
    u_hi = u.astype(BF16)
    u_lo = (u - u_hi.astype(F32)).astype(BF16)
    both = jnp.dot(u_hi, w_ref[...], preferred_element_type=F32)
    cross = jnp.dot(u_lo, w_ref[:, :V7X_LANES], preferred_element_type=F32)
    logits = both[:, :V7X_LANES] + both[:, V7X_LANES:] + cross + b_ref[...]
    lane_i = lax.broadcasted_iota(I32, (tm, V7X_LANES), 1)
    lane = lane_i.astype(F32)
    vals = logits
    tops, idxs = [], []
    for _ in range(TOP_K):
        mk = jnp.max(vals, axis=-1, keepdims=True)
        ik = jnp.min(jnp.where(vals == mk, lane, float(V7X_LANES)), axis=-1, keepdims=True)
        tops.append(mk)
        idxs.append(ik)
        vals = jnp.where(lane == ik, -jnp.inf, vals)
    es = [jnp.exp(t - tops[0]) for t in tops]
    den = es[0] + es[1] + es[2] + es[3]
    sel = jnp.zeros((tm, V7X_LANES), F32)
    for ik in idxs:
        sel = sel + jnp.where(lane == ik, 1.0, 0.0)
    ri = lax.broadcasted_iota(I32, (tm, tm), 0)
    ci = lax.broadcasted_iota(I32, (tm, tm), 1)
    lower = jnp.where(ci < ri, 1.0, 0.0).astype(BF16)
    before = jnp.dot(lower, sel.astype(BF16), preferred_element_type=F32) + run_ref[...]
    out_i = jnp.full((tm, V7X_LANES), -1, I32)
    out_r = jnp.zeros((tm, V7X_LANES), I32)
    out_g = jnp.zeros((tm, V7X_LANES), F32)
    for kk in range(TOP_K):
        rank = jnp.sum(jnp.where(lane == idxs[kk], before, 0.0), axis=-1, keepdims=True)
        out_i = jnp.where(lane_i == kk, idxs[kk].astype(I32), out_i)
        out_r = jnp.where(lane_i == kk, rank.astype(I32), out_r)
        out_g = jnp.where(lane_i == kk, es[kk] / den, out_g)
    ri_ref[...] = out_i[:, :ROUTE_LANES]
    rr_ref[...] = out_r[:, :ROUTE_LANES]
    rg_ref[...] = out_g
    run_ref[...] = run_ref[...] + jnp.sum(sel, axis=0, keepdims=True)
    cnt_ref[...] = run_ref[...]


def _router(u2, router_w, router_b, *, tm):
    n, dm = u2.shape
    w_pad = jnp.pad(router_w, ((0, 0), (0, V7X_LANES - N_EXPERTS)))
    w_hi = w_pad.astype(BF16)
    w_lo = (w_pad - w_hi.astype(F32)).astype(BF16)
    b_pad = jnp.pad(router_b, (0, V7X_LANES - N_EXPERTS), constant_values=NEG_INF).reshape(1, V7X_LANES)
    tok = pl.BlockSpec((tm, V7X_LANES), lambda i: (i, 0))
    narrow = pl.BlockSpec((tm, ROUTE_LANES), lambda i: (i, 0))
    return pl.pallas_call(
        functools.partial(_router_body, tm=tm),
        grid=(n // tm,),
        in_specs=[pl.BlockSpec((tm, dm), lambda i: (i, 0)),
                  pl.BlockSpec((dm, 2 * V7X_LANES), lambda i: (0, 0)),
                  pl.BlockSpec((1, V7X_LANES), lambda i: (0, 0))],
        out_specs=[narrow, narrow, tok, pl.BlockSpec((1, V7X_LANES), lambda i: (0, 0))],
        out_shape=[jax.ShapeDtypeStruct((n, ROUTE_LANES), I32), jax.ShapeDtypeStruct((n, ROUTE_LANES), I32),
                   jax.ShapeDtypeStruct((n, V7X_LANES), F32), jax.ShapeDtypeStruct((1, V7X_LANES), F32)],
        scratch_shapes=[pltpu.VMEM((1, V7X_LANES), F32)],
        compiler_params=_cparams(("arbitrary",), 32),
        name="router",
    )(u2, jnp.concatenate([w_hi, w_lo], axis=1), b_pad)


def _dest_body(ps_ref, ri_ref, rr_ref, d_ref):
    idx = ri_ref[...]
    acc = rr_ref[...]
    for e in range(N_EXPERTS):
        acc = acc + jnp.where(idx == e, ps_ref[e], 0)
    d_ref[...] = acc


def _dest_rows(expert, rank, pad_starts):
    return pl.pallas_call(
        _dest_body,
        in_specs=[pl.BlockSpec(memory_space=pltpu.SMEM), pl.BlockSpec(memory_space=pltpu.VMEM),
                  pl.BlockSpec(memory_space=pltpu.VMEM)],
        out_specs=pl.BlockSpec(memory_space=pltpu.VMEM),
        out_shape=jax.ShapeDtypeStruct(expert.shape, I32),
        name="dest_rows",
    )(pad_starts, expert, rank)


ROUTE_LANES = 2 * TOP_K
ISSUE_TOKENS = 2


def _row_tile(ref, row):
    return ref.at[pl.ds(pl.multiple_of(row * V7X_SUBLANES, V7X_SUBLANES), V7X_SUBLANES), :]


def _issue_rows(dest_ref, n_tok, start_copy):
    def trip(it, carry):
        base = it * ISSUE_TOKENS
        rows = [dest_ref[0, 0, (base + u) * TOP_K + kk] for u in range(ISSUE_TOKENS) for kk in range(TOP_K)]
        for u in range(ISSUE_TOKENS):
            for kk in range(TOP_K):
                start_copy(base + u, kk, rows[u * TOP_K + kk])
        return carry

    lax.fori_loop(0, n_tok // ISSUE_TOKENS, trip, 0)


DISPATCH_SLOTS = 3


FILL_UNROLL = 8


def _dispatch_body(fr_ref, dest_ref, u_ref, xs_out, stage_ref, zero_ref, sems, fill_sem, *, tm, n_fill):
    i = pl.program_id(0)
    last = pl.num_programs(0) - 1
    slot = i % DISPATCH_SLOTS

    def wait_step(s):
        for _ in range(TOP_K):
            pltpu.make_async_copy(stage_ref.at[s], xs_out.at[pl.ds(0, tm * V7X_SUBLANES), :], sems.at[s]).wait()

    @pl.when(i == 0)
    def _():
        zero_ref[...] = jnp.zeros_like(zero_ref)

        def trip(it, carry):
            rows = [fr_ref[it * FILL_UNROLL + u] for u in range(FILL_UNROLL)]
            for u, row in enumerate(rows):
                pltpu.make_async_copy(zero_ref, _row_tile(xs_out, row), fill_sem).start(priority=u % DMA_QUEUES)
            return carry

        lax.fori_loop(0, n_fill // FILL_UNROLL, trip, 0)

    @pl.when(i >= DISPATCH_SLOTS - 1)
    def _():
        wait_step((i + 1) % DISPATCH_SLOTS)

    _to_row_tiles(stage_ref.at[slot], u_ref[...])

    def start_copy(r, kk, row):
        pltpu.make_async_copy(_row_tile(stage_ref.at[slot], r), _row_tile(xs_out, row),
                              sems.at[slot]).start(priority=kk % DMA_QUEUES)

    _issue_rows(dest_ref, tm, start_copy)

    @pl.when(i == last)
    def _():
        for back in range(DISPATCH_SLOTS - 1):
            @pl.when(i >= back)
            def _():
                wait_step((i - back) % DISPATCH_SLOTS)
        pltpu.make_async_copy(xs_out.at[pl.ds(0, n_fill * V7X_SUBLANES), :],
                              xs_out.at[pl.ds(0, n_fill * V7X_SUBLANES), :], fill_sem).wait()


def _dispatch(u2, dest_sm, fill_rows, n_rows, *, tm):
    n, dm = u2.shape
    n_fill = fill_rows.shape[0]
    assert n_fill == n_rows - n * TOP_K and n_fill % FILL_UNROLL == 0
    grid_spec = pltpu.PrefetchScalarGridSpec(
        num_scalar_prefetch=1,
        grid=(n // tm,),
        in_specs=[pl.BlockSpec((1, 1, tm * TOP_K), lambda i, fr: (i, 0, 0), memory_space=pltpu.SMEM),
                  pl.BlockSpec((tm, dm), lambda i, fr: (i, 0))],
        out_specs=pl.BlockSpec(memory_space=pl.ANY),
        scratch_shapes=[pltpu.VMEM((DISPATCH_SLOTS, tm * V7X_SUBLANES, V7X_LANES), F32),
                        pltpu.VMEM((V7X_SUBLANES, V7X_LANES), F32),
                        pltpu.SemaphoreType.DMA((DISPATCH_SLOTS,)), pltpu.SemaphoreType.DMA(())],
    )
    return pl.pallas_call(
        functools.partial(_dispatch_body, tm=tm, n_fill=n_fill),
        grid_spec=grid_spec,
        out_shape=jax.ShapeDtypeStruct((n_rows * V7X_SUBLANES, V7X_LANES), F32),
        compiler_params=_cparams(("arbitrary",), 32),
        name="dispatch",
    )(fill_rows, dest_sm, u2)


def _gmm_body(be_ref, bv_ref, bf_ref, bs_ref, bn_ref, xs_ref, wu_hbm, bu_ref, wd_hbm, bd_ref, ys_ref,
              wu_f32, wd_f32, wu_bf, wd_bf, sems, *, bm):
    j = pl.program_id(0)
    e = be_ref[j]
    s = bs_ref[j]
    d_ff = wd_hbm.shape[1]
    chunk = 64

    def fetch(ex, slot):
        return (pltpu.make_async_copy(wu_hbm.at[ex], wu_f32.at[slot], sems.at[0, slot]),
                pltpu.make_async_copy(wd_hbm.at[ex], wd_f32.at[slot], sems.at[1, slot]))

    @pl.when(bf_ref[j] != 0)
    def _():
        @pl.when(j == 0)
        def _():
            for c in fetch(e, s):
                c.start()

        for c in fetch(e, s):
            c.wait()

        @pl.when(bn_ref[j] >= 0)
        def _():
            for c in fetch(bn_ref[j], 1 - s):
                c.start()

        def cast(c, carry):
            sl = pl.ds(pl.multiple_of(c * chunk, chunk), chunk)
            wu_bf[sl, :] = wu_f32[s, sl, :].astype(BF16)
            wd_bf[sl, :] = wd_f32[s, sl, :].astype(BF16)
            return carry

        lax.fori_loop(0, wu_hbm.shape[1] // chunk, cast, 0)

    @pl.when(bv_ref[j] != 0)
    def _():
        for g in range(bm // EXPERT_ROWS):
            tiles = pl.ds(g * EXPERT_ROWS * V7X_SUBLANES, EXPERT_ROWS * V7X_SUBLANES)
            x = _from_row_tiles(xs_ref.at[tiles, :], EXPERT_ROWS).astype(BF16)
            glu = jnp.dot(x, wu_bf[:, :d_ff], preferred_element_type=F32) + bu_ref[0, :, :d_ff]
            lin = jnp.dot(x, wu_bf[:, d_ff:], preferred_element_type=F32) + bu_ref[0, :, d_ff:]
            glu = jnp.minimum(glu, SWIGLU_LIMIT)
            lin = jnp.clip(lin, -SWIGLU_LIMIT, SWIGLU_LIMIT)
            act = glu * (1.0 / (1.0 + jnp.exp(-SWIGLU_ALPHA * glu))) * (lin + 1.0)
            _to_row_tiles(ys_ref.at[tiles, :],
                          jnp.dot(act.astype(BF16), wd_bf[...], preferred_element_type=F32) + bd_ref[0])

    @pl.when(bv_ref[j] == 0)
    def _():
        ys_ref[...] = jnp.zeros_like(ys_ref)


def _gmm(xs, blk_e, blk_valid, blk_first, blk_slot, blk_next, w_up, b_up, w_down, b_down, *, bm):
    n_rows = xs.shape[0] // V7X_SUBLANES
    n_e, dm, d_up = w_up.shape
    d_ff = w_down.shape[1]
    assert d_ff == dm
    row_tiles = pl.BlockSpec((bm * V7X_SUBLANES, V7X_LANES), lambda j, *_: (j, 0))
    grid_spec = pltpu.PrefetchScalarGridSpec(
        num_scalar_prefetch=5,
        grid=(n_rows // bm,),
        in_specs=[row_tiles,
                  pl.BlockSpec(memory_space=pl.ANY),
                  pl.BlockSpec((1, 1, d_up), lambda j, be, *_: (be[j], 0, 0)),
                  pl.BlockSpec(memory_space=pl.ANY),
                  pl.BlockSpec((1, 1, dm), lambda j, be, *_: (be[j], 0, 0))],
        out_specs=row_tiles,
        scratch_shapes=[pltpu.VMEM((2, dm, d_up), F32), pltpu.VMEM((2, d_ff, dm), F32),
                        pltpu.VMEM((dm, d_up), BF16), pltpu.VMEM((d_ff, dm), BF16),
                        pltpu.SemaphoreType.DMA((2, 2))],
    )
    return pl.pallas_call(
        functools.partial(_gmm_body, bm=bm),
        grid_spec=grid_spec,
        out_shape=jax.ShapeDtypeStruct(xs.shape, F32),
        compiler_params=_cparams(("arbitrary",), 56),
        name="expert_mlp",
    )(blk_e, blk_valid, blk_first, blk_slot, blk_next, xs, w_up, b_up.reshape(n_e, 1, d_up), w_down,
      b_down.reshape(n_e, 1, dm))


def _combine_body(rt_ref, rtn_ref, ys_hbm, rg_ref, x1_ref, g2_ref, lg_ref, lb_ref, y_ref, rows_ref, sems,
                  *, nb, tm, alpha, n_steps):
    n_tok = nb * tm
    i = pl.program_id(0)
    slot = i % 2

    def issue(dest_ref, s):
        def start_copy(r, kk, row):
            pltpu.make_async_copy(_row_tile(ys_hbm, row), _row_tile(rows_ref.at[s, kk], r),
                                  sems.at[s]).start(priority=kk % DMA_QUEUES)

        _issue_rows(dest_ref, n_tok, start_copy)

    @pl.when(i == 0)
    def _():
        issue(rt_ref, 0)

    @pl.when(i + 1 < n_steps)
    def _():
        issue(rtn_ref, 1 - slot)

    for kk in range(TOP_K):
        pltpu.make_async_copy(ys_hbm.at[pl.ds(0, n_tok * V7X_SUBLANES), :], rows_ref.at[slot, kk],
                              sems.at[slot]).wait()

    gates = rg_ref[...]
    ffn = gates[:, 0:1] * _from_row_tiles(rows_ref, n_tok, (slot, 0))
    for kk in range(1, TOP_K):
        ffn = ffn + gates[:, kk:kk + 1] * _from_row_tiles(rows_ref, n_tok, (slot, kk))
    dm = ffn.shape[-1]
    y = alpha * x1_ref[...].reshape(nb, tm, dm) + (1.0 + g2_ref[...]) * ffn.reshape(nb, tm, dm)
    y_ref[...] = _layer_norm(y, lg_ref[...], lb_ref[...])


def _combine(ys, dest_sm, route_g, x1_all, ada3, ln2_g, ln2_b, *, bsz, seq, nb, tm, alpha, row0):
    dm = x1_all.shape[1]
    n_tok = nb * tm
    assert row0 % n_tok == 0 and dest_sm.shape[2] == n_tok * TOP_K and n_tok % ISSUE_TOKENS == 0
    blk0 = row0 // n_tok
    st = seq // tm
    n_steps = (bsz // nb) * st
    dest = lambda off: pl.BlockSpec((1, 1, n_tok * TOP_K),
                                    lambda i: (blk0 + jnp.minimum(i + off, n_steps - 1), 0, 0),
                                    memory_space=pltpu.SMEM)
    return pl.pallas_call(
        functools.partial(_combine_body, nb=nb, tm=tm, alpha=alpha, n_steps=n_steps),
        grid=(n_steps,),
        in_specs=[dest(0), dest(1),
                  pl.BlockSpec(memory_space=pl.ANY),
                  pl.BlockSpec((n_tok, V7X_LANES), lambda i: (blk0 + i, 0)),
                  pl.BlockSpec((n_tok, dm), lambda i: (blk0 + i, 0)),
                  pl.BlockSpec((nb, 1, dm), lambda i: (i // st, 0, _GATE2)),
                  pl.BlockSpec((1, dm), lambda i: (0, 0)),
                  pl.BlockSpec((1, dm), lambda i: (0, 0))],
        out_specs=pl.BlockSpec((nb, tm, dm), lambda i: (i // st, i % st, 0)),
        scratch_shapes=[pltpu.VMEM((2, TOP_K, n_tok * V7X_SUBLANES, V7X_LANES), F32),
                        pltpu.SemaphoreType.DMA((2,))],
        out_shape=jax.ShapeDtypeStruct((bsz, seq, dm), F32),
        compiler_params=_cparams(("arbitrary",), 48),
        name="combine_prompt" if row0 == 0 else "combine_sample",
    )(dest_sm, dest_sm, ys, route_g, x1_all, ada3, ln2_g.reshape(1, dm), ln2_b.reshape(1, dm))


def _layer(xp, xs, cache_k, cache_v, conv_past, cp, cs, w_ada, b_ada, w_in, conv_w, norm_attn_g, norm_conv_g,
           w_o, ln1_g, ln1_b, bias_tabs, router_w, router_b, w_up, b_up, w_down, b_down, ln2_g, ln2_b, alpha):
    bp, sp, dm = xp.shape
    bs, ts, _ = xs.shape
    n_p, n_s = bp * sp, bs * ts
    n_tok = n_p + n_s
    assert n_p % TOK_TILE == 0 and n_s == TOK_TILE and sp % ROW_TILE == 0 and sp % OUT_TILE == 0
    bias_p, bias_c, bias_n = bias_tabs

    ada = _ada(jnp.concatenate([cp, cs], axis=0), w_ada, b_ada)
    ada_p = ada[:bp].reshape(bp, 1, -1)
    ada_s = ada[bp:].reshape(bs, 1, -1)
    w_in_bf = w_in.astype(BF16)
    w_o_bf = w_o.astype(BF16)

    qp, kp, vp, convp, cstp, kpt, vpt = _inproj(xp, ada_p, w_in_bf, conv_w, norm_conv_g, None, nb=1, tm=ROW_TILE)
    attn_p = _attn_prompt(qp, kp, vp, bias_p)
    x1_all, u2_all = _outproj(attn_p, convp, xp, ada_p, w_o_bf, norm_attn_g, ln1_g, ln1_b,
                              nb=1, tm=OUT_TILE, alpha=alpha, n_total=n_tok, row0=0)
    qs, ks, vs, convs, csts = _inproj(xs, ada_s, w_in_bf, conv_w, norm_conv_g, conv_past, nb=bs, tm=ts)
    heads = lambda a: jnp.transpose(a.reshape(bs, ts, N_HEADS, HEAD_DIM), (0, 2, 1, 3))
    n_past = cache_k.shape[1]
    feat_major = lambda c: jnp.transpose(c.reshape(bs, n_past, D_ATTN), (0, 2, 1))
    attn_s = _attn_sample(heads(qs), heads(ks), heads(vs), feat_major(cache_k), feat_major(cache_v), bias_c, bias_n)
    attn_s = jnp.transpose(attn_s, (0, 2, 1, 3)).reshape(bs, ts, D_ATTN)
    x1_all, u2_all = _outproj(attn_s, convs, xs, ada_s, w_o_bf, norm_attn_g, ln1_g, ln1_b,
                              nb=bs, tm=ts, alpha=alpha, n_total=n_tok, row0=n_p, prev=(x1_all, u2_all))

    route_i, route_r, route_g, counts = _router(u2_all, router_w, router_b, tm=TOK_TILE)
    counts = counts[0, :N_EXPERTS].astype(I32)
    padded = ((counts + EXPERT_BLK - 1) // EXPERT_BLK) * EXPERT_BLK
    pad_ends = jnp.cumsum(padded)
    pad_starts = (pad_ends - padded).astype(I32)
    n_blocks = n_tok * TOP_K // EXPERT_BLK + N_EXPERTS
    n_rows = n_blocks * EXPERT_BLK
    blk_row = jnp.arange(n_blocks, dtype=I32) * EXPERT_BLK
    blk_valid = (blk_row < pad_ends[-1]).astype(I32)
    blk_e = jnp.sum((blk_row[:, None] >= pad_ends[None, :]).astype(I32), axis=1)
    last_e = jnp.sum((pad_ends[-1] - 1 >= pad_ends).astype(I32))
    blk_e = jnp.where(blk_valid != 0, blk_e, last_e).astype(I32)
    ar = jnp.arange(N_EXPERTS, dtype=I32)
    has_rows = padded > 0
    slot_of_e = (jnp.cumsum(has_rows.astype(I32)) - 1) % 2
    next_of_e = jnp.min(jnp.where((ar[None, :] > ar[:, None]) & has_rows[None, :], ar[None, :], N_EXPERTS), axis=1)
    next_of_e = jnp.where(next_of_e < N_EXPERTS, next_of_e, -1)
    is_e = blk_e[:, None] == ar[None, :]
    pick = lambda per_expert: jnp.sum(jnp.where(is_e, per_expert[None, :], 0), axis=1).astype(I32)
    blk_first = (blk_valid * (blk_row == pick(pad_starts)).astype(I32)).astype(I32)
    blk_slot = pick(slot_of_e)
    blk_next = pick(next_of_e)
    dense = lambda a: a[:, :TOP_K].reshape(n_tok * TOP_K // V7X_LANES, V7X_LANES)
    dest = _dest_rows(dense(route_i), dense(route_r), pad_starts)
    dest_sm = dest.reshape(n_tok // TOK_TILE, 1, TOK_TILE * TOP_K)

    run_start = jnp.concatenate([pad_starts + counts, pad_ends[-1:]])
    run_len = jnp.concatenate([padded - counts, n_rows - pad_ends[-1:]])
    run_k0 = jnp.cumsum(run_len) - run_len
    k = jnp.arange(n_rows - n_tok * TOP_K, dtype=I32)[:, None]
    in_run = (k >= run_k0[None, :]) & (k < (run_k0 + run_len)[None, :])
    fill_rows = jnp.sum(jnp.where(in_run, run_start[None, :] + k - run_k0[None, :], 0), axis=1).astype(I32)
    x_sorted = _dispatch(u2_all, dest_sm, fill_rows, n_rows, tm=TOK_TILE)
    y_sorted = _gmm(x_sorted, blk_e, blk_valid, blk_first, blk_slot, blk_next, w_up, b_up, w_down, b_down,
                    bm=EXPERT_BLK)
    yp = _combine(y_sorted, dest_sm, route_g, x1_all, ada_p, ln2_g, ln2_b,
                  bsz=bp, seq=sp, nb=1, tm=TOK_TILE, alpha=alpha, row0=0)
    ys_out = _combine(y_sorted, dest_sm, route_g, x1_all, ada_s, ln2_g, ln2_b,
                      bsz=bs, seq=ts, nb=bs, tm=ts, alpha=alpha, row0=n_p)
    return yp, ys_out, kpt, vpt, cstp, ks, vs, csts


def kernel(x_prompt, x_sample, cache_k, cache_v, state_conv, c_prompt, c_sample, w_ada, b_ada, w_in, conv_w,
           norm_attn_g, norm_conv_g, w_o, ln1_g, ln1_b, rel_bias, router_w, router_b, w_up, b_up, w_down, b_down,
           ln2_g, ln2_b):
    depth = w_ada.shape[0]
    alpha = (2 * depth) ** 0.25
    xp, xs = x_prompt, x_sample
    bp, sp, _ = xp.shape
    bs, ts, _ = xs.shape
    n_keep = min(BRANCHES[-1][0], sp)
    bias_tabs = _bias_tables(rel_bias, ts, cache_k.shape[2])
    outs = [[] for _ in range(6)]
    for l in range(depth):
        xp, xs, kp, vp, cstp, ks, vs, csts = _layer(
            xp, xs, cache_k[l], cache_v[l], state_conv[l], c_prompt, c_sample, w_ada[l], b_ada[l], w_in[l],
            conv_w[l], norm_attn_g[l], norm_conv_g[l], w_o[l], ln1_g[l], ln1_b[l], bias_tabs, router_w[l],
            router_b[l], w_up[l], b_up[l], w_down[l], b_down[l], ln2_g[l], ln2_b[l], alpha)
        kp = jnp.transpose(kp, (0, 2, 1)).reshape(bp, sp, N_HEADS, HEAD_DIM)[:, -n_keep:]
        vp = jnp.transpose(vp, (0, 2, 1)).reshape(bp, sp, N_HEADS, HEAD_DIM)[:, -n_keep:]
        for lst, val in zip(outs, (kp, vp, cstp, ks.reshape(bs, ts, N_HEADS, HEAD_DIM),
                                   vs.reshape(bs, ts, N_HEADS, HEAD_DIM), csts)):
            lst.append(val)
    return (xp, xs) + tuple(jnp.stack(o) for o in outs)
```

```python
import functools
import math

import numpy as np
import jax
import jax.numpy as jnp
from jax import lax
from jax.experimental import pallas as pl
from jax.experimental.pallas import tpu as pltpu

F32 = jnp.float32
BF16 = jnp.bfloat16
I32 = jnp.int32

HEAD_DIM = 64
N_HEADS = 12
D_ATTN = N_HEADS * HEAD_DIM
D_CONV = 256
CONV_WIDTH = 3
BRANCHES = ((128, 1), (512, 4), (2048, 16))
NUM_BUCKETS = 32
MAX_DISTANCE = 2048
N_EXPERTS = 32
TOP_K = 4
SWIGLU_LIMIT = 7.0
SWIGLU_ALPHA = 1.702
LN_EPS = 1e-5
RMS_EPS = 1e-6
NEG_INF = -1e30
LOG2E = math.log2(math.e)

V7X_LANES = 128
V7X_SUBLANES = 8
V7X_VMEM_BYTES = 64 * 1024 * 1024
DMA_QUEUES = 2

Q_BLK = 128
PAIR = 2 * HEAD_DIM
N_PAIRS = N_HEADS // 2
TOK_TILE = 256
ROW_TILE = 512
OUT_TILE = 512
EXPERT_BLK = 256
EXPERT_ROWS = 256
ATTN_UNROLL = 16


def _cparams(sem, vmem_mb):
    return pltpu.CompilerParams(dimension_semantics=sem, vmem_limit_bytes=vmem_mb * 1024 * 1024)


def _t5_bucket_np(dist):
    dist = np.asarray(dist, np.int64)
    max_exact = NUM_BUCKETS // 2
    df = np.maximum(dist, max_exact).astype(np.float32)
    large = max_exact + (np.log(df / np.float32(max_exact)) / np.float32(math.log(MAX_DISTANCE / max_exact))
                         * np.float32(NUM_BUCKETS - max_exact)).astype(np.int32)
    return np.where(dist < max_exact, dist, np.minimum(large, NUM_BUCKETS - 1)).astype(np.int32)


def _prompt_bucket_index():
    a = np.arange(Q_BLK)[:, None]
    c = np.arange(2 * Q_BLK)[None, :]
    step = Q_BLK + a - c
    valid = (step >= 0) & (step <= Q_BLK)
    out = []
    for _, d in BRANCHES:
        out.append(np.where(valid, _t5_bucket_np(np.clip(step, 0, Q_BLK) * d), -1))
    return np.stack(out).astype(np.int32)


def _sample_bucket_index(n_new, n_past):
    t = np.arange(n_new)[:, None]
    out_c, out_n = [], []
    for w, d in BRANCHES:
        dist_c = n_past + t - np.arange(n_past)[None, :]
        dist_n = t - np.arange(n_new)[None, :]
        for dist, out in ((dist_c, out_c), (dist_n, out_n)):
            valid = (dist >= 0) & (dist <= w) & (dist % d == 0)
            out.append(np.where(valid, _t5_bucket_np(np.clip(dist, 0, w)), -1))
    return np.stack(out_c).astype(np.int32), np.stack(out_n).astype(np.int32)


def _bias_body(rb_ref, idx_ref, out_ref, *, mult, hide_cols):
    idx = idx_ref[0]
    col = lax.broadcasted_iota(I32, idx.shape, 1)
    for h in range(N_HEADS):
        acc = jnp.where(idx < 0, NEG_INF, 0.0).astype(F32)
        for b in range(NUM_BUCKETS):
            acc = acc + jnp.where(idx == b, rb_ref[b * N_HEADS + h] * mult, 0.0)
        out_ref[0, 0, h] = acc
        if hide_cols:
            out_ref[0, 1, h] = jnp.where(col < hide_cols, NEG_INF, acc)


def _bias_expand(rel_bias, idx_np, name, mult=1.0, hide_cols=0):
    nbr, r, c = idx_np.shape
    nvar = 2 if hide_cols else 1
    return pl.pallas_call(
        functools.partial(_bias_body, mult=mult, hide_cols=hide_cols),
        grid=(nbr,),
        in_specs=[pl.BlockSpec(memory_space=pltpu.SMEM),
                  pl.BlockSpec((1, r, c), lambda i: (i, 0, 0))],
        out_specs=pl.BlockSpec((1, nvar, N_HEADS, r, c), lambda i: (i, 0, 0, 0, 0)),
        out_shape=jax.ShapeDtypeStruct((nbr, nvar, N_HEADS, r, c), F32),
        name=name,
    )(rel_bias.reshape(-1), jnp.asarray(idx_np))


def _bias_tables(rel_bias, n_new, n_past):
    bias_p = _bias_expand(rel_bias, _prompt_bucket_index(), "bias_prompt", mult=LOG2E, hide_cols=Q_BLK)
    bias_p = bias_p.reshape(len(BRANCHES), 2, N_PAIRS, 2 * Q_BLK, 2 * Q_BLK)
    ic, inw = _sample_bucket_index(n_new, n_past)
    plain = lambda idx, name: _bias_expand(rel_bias, idx, name).reshape((idx.shape[0], N_HEADS) + idx.shape[1:])
    return bias_p, plain(ic, "bias_cache"), plain(inw, "bias_new")


def _ada_body(c_ref, w_ref, b_ref, o_ref):
    c = c_ref[...]
    s = c * (1.0 / (1.0 + jnp.exp(-c)))
    o_ref[...] = jnp.dot(s.astype(BF16), w_ref[...].astype(BF16), preferred_element_type=F32) + b_ref[...]


def _ada(c_all, w_ada, b_ada):
    n, dm = c_all.shape
    n_out = w_ada.shape[1]
    tn = dm
    return pl.pallas_call(
        _ada_body,
        grid=(n_out // tn,),
        in_specs=[pl.BlockSpec((n, dm), lambda j: (0, 0)),
                  pl.BlockSpec((dm, tn), lambda j: (0, j)),
                  pl.BlockSpec((1, tn), lambda j: (0, j))],
        out_specs=pl.BlockSpec((n, tn), lambda j: (0, j)),
        out_shape=jax.ShapeDtypeStruct((n, n_out), F32),
        compiler_params=_cparams(("parallel",), 32),
        name="ada",
    )(c_all, w_ada, b_ada.reshape(1, n_out))


_SHIFT1, _SCALE1, _GATE1, _SHIFT2, _SCALE2, _GATE2 = range(6)


def _inproj_body(*refs, nb, tm, carry):
    if carry:
        (x_ref, sh_ref, sc_ref, w_ref, cw_ref, ng_ref, past_ref,
         q_ref, k_ref, v_ref, conv_ref, cst_ref, kt_ref, vt_ref, zz_ref) = refs
    else:
        (x_ref, sh_ref, sc_ref, w_ref, cw_ref, ng_ref, past_ref,
         q_ref, k_ref, v_ref, conv_ref, cst_ref, zz_ref) = refs
    dm = x_ref.shape[-1]
    rows = nb * tm
    u = x_ref[...] * (1.0 + sc_ref[...]) + sh_ref[...]
    u = u.reshape(rows, dm).astype(BF16)

    def proj(lo, width):
        return jnp.dot(u, w_ref[:, lo:lo + width], preferred_element_type=F32)

    q_ref[...] = proj(0, D_ATTN).reshape(nb, tm, D_ATTN)
    k = proj(D_ATTN, D_ATTN)
    v = proj(2 * D_ATTN, D_ATTN)
    k_ref[...] = k.reshape(nb, tm, D_ATTN)
    v_ref[...] = v.reshape(nb, tm, D_ATTN)
    if carry:
        kt_ref[0] = k.T
        vt_ref[0] = v.T
    gb = proj(3 * D_ATTN, D_CONV)
    gc = proj(3 * D_ATTN + D_CONV, D_CONV)
    hh = proj(3 * D_ATTN + 2 * D_CONV, D_CONV)
    z = (gc * hh).reshape(nb, tm, D_CONV)

    if carry:
        s = pl.program_id(1)

        @pl.when(s == 0)
        def _():
            zz_ref[:, 0:V7X_SUBLANES, :] = jnp.zeros((nb, V7X_SUBLANES, D_CONV), F32)

        @pl.when(s > 0)
        def _():
            zz_ref[:, 0:V7X_SUBLANES, :] = zz_ref[:, tm:tm + V7X_SUBLANES, :]
    else:
        zz_ref[:, V7X_SUBLANES - 2:V7X_SUBLANES, :] = past_ref[...]
    zz_ref[:, V7X_SUBLANES:, :] = z

    cw = cw_ref[...]
    yc = (cw[0:1, :] * zz_ref[:, V7X_SUBLANES - 2:V7X_SUBLANES - 2 + tm, :]
          + cw[1:2, :] * zz_ref[:, V7X_SUBLANES - 1:V7X_SUBLANES - 1 + tm, :]
          + cw[2:3, :] * z)
    g = gb.reshape(nb, tm, D_CONV) * yc
    ms = jnp.mean(g * g, axis=-1, keepdims=True)
    conv_ref[...] = g * lax.rsqrt(ms + RMS_EPS) * ng_ref[...]
    cst_ref[...] = zz_ref[:, tm + V7X_SUBLANES - 2:tm + V7X_SUBLANES, :]


def _inproj(x, ada3, w_in_bf, conv_w, norm_conv_g, past, *, nb, tm):
    bsz, seq, dm = x.shape
    carry = past is None
    if carry:
        past = jnp.zeros((bsz, CONV_WIDTH - 1, D_CONV), F32)
    grid = (bsz // nb, seq // tm)
    d_in = w_in_bf.shape[1]
    body = functools.partial(_inproj_body, nb=nb, tm=tm, carry=carry)
    tile = lambda width: pl.BlockSpec((nb, tm, width), lambda b, s: (b, s, 0))
    in_specs = [tile(dm),
                pl.BlockSpec((nb, 1, dm), lambda b, s: (b, 0, _SHIFT1)),
                pl.BlockSpec((nb, 1, dm), lambda b, s: (b, 0, _SCALE1)),
                pl.BlockSpec((dm, d_in), lambda b, s: (0, 0)),
                pl.BlockSpec((CONV_WIDTH, D_CONV), lambda b, s: (0, 0)),
                pl.BlockSpec((1, D_CONV), lambda b, s: (0, 0)),
                pl.BlockSpec((nb, CONV_WIDTH - 1, D_CONV), lambda b, s: (b, 0, 0))]
    args = [x, ada3, ada3, w_in_bf, conv_w, norm_conv_g.reshape(1, D_CONV), past]
    out_specs = [tile(D_ATTN), tile(D_ATTN), tile(D_ATTN), tile(D_CONV),
                 pl.BlockSpec((nb, CONV_WIDTH - 1, D_CONV), lambda b, s: (b, 0, 0))]
    out_shape = ([jax.ShapeDtypeStruct((bsz, seq, D_ATTN), F32)] * 3
                 + [jax.ShapeDtypeStruct((bsz, seq, D_CONV), F32),
                    jax.ShapeDtypeStruct((bsz, CONV_WIDTH - 1, D_CONV), F32)])
    if carry:
        assert nb == 1
        out_specs += [pl.BlockSpec((1, D_ATTN, tm), lambda b, s: (b, 0, s))] * 2
        out_shape += [jax.ShapeDtypeStruct((bsz, D_ATTN, seq), F32)] * 2
    return pl.pallas_call(
        body,
        grid=grid,
        in_specs=in_specs,
        out_specs=out_specs,
        out_shape=out_shape,
        scratch_shapes=[pltpu.VMEM((nb, tm + V7X_SUBLANES, D_CONV), F32)],
        compiler_params=_cparams(("parallel", "arbitrary"), 56),
        name="inproj_prompt" if carry else "inproj_sample",
    )(*args)


def _attn_prompt_body(q_ref, k_ref, v_ref, bias_ref, o_ref, ob_ref, lb_ref, *, seq):
    scale = HEAD_DIM ** -0.5 * LOG2E
    lane = lax.broadcasted_iota(I32, (Q_BLK, PAIR), 1)
    even = lane < HEAD_DIM
    ones = jnp.ones((Q_BLK, PAIR), BF16)

    def rows(ref, start, d):
        if d == 1:
            return ref[pl.ds(start, Q_BLK), :]
        return ref[pl.ds(start, Q_BLK, stride=d), :]

    def store(ref, br, start, d, val):
        if d == 1:
            ref[br, pl.ds(start, Q_BLK), :] = val
        else:
            ref[br, pl.ds(start, Q_BLK, stride=d), :] = val

    for br, (_, d) in enumerate(BRANCHES):
        nblk = seq // (d * Q_BLK)

        def block(it, carry, br=br, d=d, nblk=nblk):
            r = it // nblk
            i = it - r * nblk
            cur0 = r + d * Q_BLK * i
            qf = rows(q_ref, cur0, d) * scale
            qs = jnp.concatenate([jnp.where(even, qf, 0.0), jnp.where(even, 0.0, qf)], axis=0).astype(BF16)
            kc = rows(k_ref, cur0, d).astype(BF16)
            vc = rows(v_ref, cur0, d).astype(BF16)
            if nblk == 1:
                s = lax.dot_general(qs, kc, (((1,), (1,)), ((), ())), preferred_element_type=F32)
                s = s + bias_ref[br, 0, 0, :, Q_BLK:]
                vaug = jnp.concatenate([vc, ones], axis=1)
            else:
                prev0 = r + d * Q_BLK * jnp.maximum(i - 1, 0)
                kp = rows(k_ref, prev0, d).astype(BF16)
                vp = rows(v_ref, prev0, d).astype(BF16)
                kk = jnp.concatenate([kp, kc], axis=0)
                s = lax.dot_general(qs, kk, (((1,), (1,)), ((), ())), preferred_element_type=F32)
                s = s + bias_ref[br, jnp.where(i == 0, 1, 0), 0]
                vaug = jnp.concatenate([jnp.concatenate([vp, vc], axis=0),
                                        jnp.concatenate([ones, ones], axis=0)], axis=1)
            m = jnp.max(s, axis=-1, keepdims=True)
            p = jnp.exp2(s - m).astype(BF16)
            out = jnp.dot(p, vaug, preferred_element_type=F32)
            num, den = out[:, :PAIR], out[:, PAIR:]
            o = num / den
            lse = m + jnp.log2(den)
            store(ob_ref, br, cur0, d, jnp.where(even, o[:Q_BLK], o[Q_BLK:]))
            store(lb_ref, br, cur0, d, jnp.where(even, lse[:Q_BLK], lse[Q_BLK:]))
            return carry

        lax.fori_loop(0, d * nblk, block, 0, unroll=ATTN_UNROLL)

    mt = 2 * Q_BLK

    def merge(i, carry):
        sl = pl.ds(pl.multiple_of(i * mt, mt), mt)
        l0, l1, l2 = lb_ref[0, sl, :], lb_ref[1, sl, :], lb_ref[2, sl, :]
        mx = jnp.maximum(jnp.maximum(l0, l1), l2)
        w0, w1, w2 = jnp.exp2(l0 - mx), jnp.exp2(l1 - mx), jnp.exp2(l2 - mx)
        acc = w0 * ob_ref[0, sl, :] + w1 * ob_ref[1, sl, :] + w2 * ob_ref[2, sl, :]
        o_ref[sl, :] = acc / (w0 + w1 + w2)
        return carry

    lax.fori_loop(0, seq // mt, merge, 0)


def _attn_prompt(q, k, v, bias_p):
    bsz, seq, _ = q.shape
    assert seq % (BRANCHES[-1][1] * Q_BLK) == 0
    body = functools.partial(_attn_prompt_body, seq=seq)
    col = pl.BlockSpec((None, seq, PAIR), lambda b, j: (b, 0, j))
    return pl.pallas_call(
        body,
        grid=(bsz, N_PAIRS),
        in_specs=[col, col, col,
                  pl.BlockSpec((len(BRANCHES), 2, 1, 2 * Q_BLK, 2 * Q_BLK), lambda b, j: (0, 0, j, 0, 0))],
        out_specs=col,
        out_shape=jax.ShapeDtypeStruct((bsz, seq, D_ATTN), F32),
        scratch_shapes=[pltpu.VMEM((len(BRANCHES), seq, PAIR), F32),
                        pltpu.VMEM((len(BRANCHES), seq, PAIR), F32)],
        compiler_params=_cparams(("parallel", "parallel"), 48),
        name="attn_prompt",
    )(q, k, v, bias_p)


def _attn_sample_body(q_ref, kn_ref, vn_ref, ckt_ref, cvt_ref, bc_ref, bn_ref, o_ref):
    scale = HEAD_DIM ** -0.5
    nt = (((1,), (1,)), ((), ()))
    nbr = len(BRANCHES)
    for h in range(N_HEADS):
        rows = slice(h * HEAD_DIM, (h + 1) * HEAD_DIM)
        qh = (q_ref[h] * scale).astype(BF16)
        s_c = jnp.dot(qh, ckt_ref[rows, :].astype(BF16), preferred_element_type=F32)
        s_n = lax.dot_general(qh, kn_ref[h].astype(BF16), nt, preferred_element_type=F32)
        ps, lses, dens = [], [], []
        for br in range(nbr):
            lc = s_c + bc_ref[br, h]
            ln = s_n + bn_ref[br, h]
            m = jnp.maximum(jnp.max(lc, axis=-1, keepdims=True), jnp.max(ln, axis=-1, keepdims=True))
            pc = jnp.exp(lc - m)
            pn = jnp.exp(ln - m)
            den = jnp.sum(pc, axis=-1, keepdims=True) + jnp.sum(pn, axis=-1, keepdims=True)
            ps.append((pc, pn))
            dens.append(den)
            lses.append(m + jnp.log(den))
        mx = jnp.maximum(jnp.maximum(lses[0], lses[1]), lses[2])
        ws = [jnp.exp(l - mx) for l in lses]
        wsum = ws[0] + ws[1] + ws[2]
        coefs = [ws[br] / (wsum * dens[br]) for br in range(nbr)]
        p_c = coefs[0] * ps[0][0] + coefs[1] * ps[1][0] + coefs[2] * ps[2][0]
        p_n = coefs[0] * ps[0][1] + coefs[1] * ps[1][1] + coefs[2] * ps[2][1]
        o = lax.dot_general(p_c.astype(BF16), cvt_ref[rows, :].astype(BF16), nt, preferred_element_type=F32)
        o_ref[h] = o + jnp.dot(p_n.astype(BF16), vn_ref[h].astype(BF16), preferred_element_type=F32)


def _attn_sample(q, k_new, v_new, cache_kt, cache_vt, bias_c, bias_n):
    bsz, _, n_new, _ = q.shape
    n_past = cache_kt.shape[2]
    new = pl.BlockSpec((None, N_HEADS, n_new, HEAD_DIM), lambda b: (b, 0, 0, 0))
    cache = pl.BlockSpec((None, D_ATTN, n_past), lambda b: (b, 0, 0))
    return pl.pallas_call(
        _attn_sample_body,
        grid=(bsz,),
        in_specs=[new, new, new, cache, cache,
                  pl.BlockSpec(bias_c.shape, lambda b: (0, 0, 0, 0)), pl.BlockSpec(bias_n.shape, lambda b: (0, 0, 0, 0))],
        out_specs=new,
        out_shape=jax.ShapeDtypeStruct((bsz, N_HEADS, n_new, HEAD_DIM), F32),
        compiler_params=_cparams(("parallel",), 48),
        name="attn_sample",
    )(q, k_new, v_new, cache_kt, cache_vt, bias_c, bias_n)


def _layer_norm(y, g, b):
    mu = jnp.mean(y, axis=-1, keepdims=True)
    c = y - mu
    var = jnp.mean(c * c, axis=-1, keepdims=True)
    return c * lax.rsqrt(var + LN_EPS) * g + b


def _to_row_tiles(ref, val):
    rows, dm = val.shape
    assert dm == V7X_SUBLANES * V7X_LANES
    for c in range(V7X_SUBLANES):
        ref[pl.ds(c, rows, stride=V7X_SUBLANES), :] = val[:, c * V7X_LANES:(c + 1) * V7X_LANES]


def _from_row_tiles(ref, rows, lead=()):
    chunks = [ref[lead + (pl.ds(c, rows, stride=V7X_SUBLANES), slice(None))] for c in range(V7X_SUBLANES)]
    return jnp.concatenate(chunks, axis=1)


def _outproj_body(*refs, nb, tm, alpha, aliased, n_main):
    if aliased:
        refs = refs[2:]
    (a_ref, c_ref, x_ref, g1_ref, sh2_ref, sc2_ref, wa_ref, wc_ref, ng_ref, lg_ref, lb_ref, x1_ref, u2_ref) = refs
    rows = nb * tm

    @pl.when(pl.program_id(0) < n_main)
    def _():
        a = a_ref[...]
        ms = jnp.mean(a * a, axis=-1, keepdims=True)
        an = (a * lax.rsqrt(ms + RMS_EPS) * ng_ref[...]).reshape(rows, D_ATTN).astype(BF16)
        cn = c_ref[...].reshape(rows, D_CONV).astype(BF16)
        mix = (jnp.dot(an, wa_ref[...], preferred_element_type=F32)
               + jnp.dot(cn, wc_ref[...], preferred_element_type=F32))
        dm = mix.shape[-1]
        y = alpha * x_ref[...] + (1.0 + g1_ref[...]) * mix.reshape(nb, tm, dm)
        x1 = _layer_norm(y, lg_ref[...], lb_ref[...])
        x1_ref[...] = x1.reshape(rows, dm)
        u2_ref[...] = (x1 * (1.0 + sc2_ref[...]) + sh2_ref[...]).reshape(rows, dm)

    @pl.when(pl.program_id(0) >= n_main)
    def _():
        x1_ref[...] = jnp.zeros_like(x1_ref)
        u2_ref[...] = jnp.zeros_like(u2_ref)


def _outproj(attn, conv, x, ada3, w_o_bf, norm_attn_g, ln1_g, ln1_b, *, nb, tm, alpha, n_total, row0, prev=None):
    bsz, seq, dm = x.shape
    rows = nb * tm
    assert row0 % rows == 0
    st = seq // tm
    blk0 = row0 // rows
    n_main = (bsz // nb) * st
    aliased = prev is not None
    n_steps = n_main if aliased else pl.cdiv(n_total, rows)

    def bs_of(i):
        j = jnp.minimum(i, n_main - 1)
        return j // st, j % st

    tile = lambda width: pl.BlockSpec((nb, tm, width), lambda i: bs_of(i) + (0,))
    ada = lambda chunk: pl.BlockSpec((nb, 1, dm), lambda i: (bs_of(i)[0], 0, chunk))
    const = lambda shape: pl.BlockSpec(shape, lambda i: (0,) * len(shape))
    out = pl.BlockSpec((rows, dm), lambda i: (blk0 + i, 0))
    body = functools.partial(_outproj_body, nb=nb, tm=tm, alpha=alpha, aliased=aliased, n_main=n_main)
    in_specs = [tile(D_ATTN), tile(D_CONV), tile(dm), ada(_GATE1), ada(_SHIFT2), ada(_SCALE2),
                const((D_ATTN, dm)), const((D_CONV, dm)), const((1, D_ATTN)), const((1, dm)), const((1, dm))]
    args = [attn, conv, x, ada3, ada3, ada3, w_o_bf[:D_ATTN], w_o_bf[D_ATTN:],
            norm_attn_g.reshape(1, D_ATTN), ln1_g.reshape(1, dm), ln1_b.reshape(1, dm)]
    kwargs = {}
    if aliased:
        in_specs = [pl.BlockSpec(memory_space=pl.ANY)] * 2 + in_specs
        args = list(prev) + args
        kwargs["input_output_aliases"] = {0: 0, 1: 1}
    return pl.pallas_call(
        body,
        grid=(n_steps,),
        in_specs=in_specs,
        out_specs=[out, out],
        out_shape=[jax.ShapeDtypeStruct((n_total, dm), F32)] * 2,
        compiler_params=_cparams(("parallel",), 48),
        name="outproj_sample" if aliased else "outproj_prompt",
        **kwargs,
    )(*args)


def _router_body(u_ref, w_ref, b_ref, ri_ref, rr_ref, rg_ref, cnt_ref, run_ref, *, tm):
    i = pl.program_id(0)

    @pl.when(i == 0)
    def _():
        run_ref[...] = jnp.zeros_like(run_ref)

    u = u_ref[...]
    u_hi = u.astype(BF16)
    u_lo = (u - u_hi.astype(F32)).astype(BF16)
    both = jnp.dot(u_hi, w_ref[...], preferred_element_type=F32)
    cross = jnp.dot(u_lo, w_ref[:, :V7X_LANES], preferred_element_type=F32)
    logits = both[:, :V7X_LANES] + both[:, V7X_LANES:] + cross + b_ref[...]
    lane_i = lax.broadcasted_iota(I32, (tm, V7X_LANES), 1)
    lane = lane_i.astype(F32)
    vals = logits
    tops, idxs = [], []
    for _ in range(TOP_K):
        mk = jnp.max(vals, axis=-1, keepdims=True)
        ik = jnp.min(jnp.where(vals == mk, lane, float(V7X_LANES)), axis=-1, keepdims=True)
        tops.append(mk)
        idxs.append(ik)
        vals = jnp.where(lane == ik, -jnp.inf, vals)
    es = [jnp.exp(t - tops[0]) for t in tops]
    den = es[0] + es[1] + es[2] + es[3]
    sel = jnp.zeros((tm, V7X_LANES), F32)
    for ik in idxs:
        sel = sel + jnp.where(lane == ik, 1.0, 0.0)
    ri = lax.broadcasted_iota(I32, (tm, tm), 0)
    ci = lax.broadcasted_iota(I32, (tm, tm), 1)
    lower = jnp.where(ci < ri, 1.0, 0.0).astype(BF16)
    before = jnp.dot(lower, sel.astype(BF16), preferred_element_type=F32) + run_ref[...]
    out_i = jnp.full((tm, V7X_LANES), -1, I32)
    out_r = jnp.zeros((tm, V7X_LANES), I32)
    out_g = jnp.zeros((tm, V7X_LANES), F32)
    for kk in range(TOP_K):
        rank = jnp.sum(jnp.where(lane == idxs[kk], before, 0.0), axis=-1, keepdims=True)
        out_i = jnp.where(lane_i == kk, idxs[kk].astype(I32), out_i)
        out_r = jnp.where(lane_i == kk, rank.astype(I32), out_r)
        out_g = jnp.where(lane_i == kk, es[kk] / den, out_g)
    ri_ref[...] = out_i
    rr_ref[...] = out_r
    rg_ref[...] = out_g
    run_ref[...] = run_ref[...] + jnp.sum(sel, axis=0, keepdims=True)
    cnt_ref[...] = run_ref[...]


def _router(u2, router_w, router_b, *, tm):
    n, dm = u2.shape
    w_pad = jnp.pad(router_w, ((0, 0), (0, V7X_LANES - N_EXPERTS)))
    w_hi = w_pad.astype(BF16)
    w_lo = (w_pad - w_hi.astype(F32)).astype(BF16)
    b_pad = jnp.pad(router_b, (0, V7X_LANES - N_EXPERTS), constant_values=NEG_INF).reshape(1, V7X_LANES)
    tok = pl.BlockSpec((tm, V7X_LANES), lambda i: (i, 0))
    return pl.pallas_call(
        functools.partial(_router_body, tm=tm),
        grid=(n // tm,),
        in_specs=[pl.BlockSpec((tm, dm), lambda i: (i, 0)),
                  pl.BlockSpec((dm, 2 * V7X_LANES), lambda i: (0, 0)),
                  pl.BlockSpec((1, V7X_LANES), lambda i: (0, 0))],
        out_specs=[tok, tok, tok, pl.BlockSpec((1, V7X_LANES), lambda i: (0, 0))],
        out_shape=[jax.ShapeDtypeStruct((n, V7X_LANES), I32), jax.ShapeDtypeStruct((n, V7X_LANES), I32),
                   jax.ShapeDtypeStruct((n, V7X_LANES), F32), jax.ShapeDtypeStruct((1, V7X_LANES), F32)],
        scratch_shapes=[pltpu.VMEM((1, V7X_LANES), F32)],
        compiler_params=_cparams(("arbitrary",), 32),
        name="router",
    )(u2, jnp.concatenate([w_hi, w_lo], axis=1), b_pad)


def _dest_body(ps_ref, ri_ref, rr_ref, d_ref):
    idx = ri_ref[...]
    acc = rr_ref[...]
    for e in range(N_EXPERTS):
        acc = acc + jnp.where(idx == e, ps_ref[e], 0)
    d_ref[...] = acc


def _dest_rows(expert, rank, pad_starts):
    return pl.pallas_call(
        _dest_body,
        in_specs=[pl.BlockSpec(memory_space=pltpu.SMEM), pl.BlockSpec(memory_space=pltpu.VMEM),
                  pl.BlockSpec(memory_space=pltpu.VMEM)],
        out_specs=pl.BlockSpec(memory_space=pltpu.VMEM),
        out_shape=jax.ShapeDtypeStruct(expert.shape, I32),
        name="dest_rows",
    )(pad_starts, expert, rank)


ISSUE_TOKENS = 2


def _row_tile(ref, row):
    return ref.at[pl.ds(pl.multiple_of(row * V7X_SUBLANES, V7X_SUBLANES), V7X_SUBLANES), :]


def _issue_rows(dest_ref, n_tok, start_copy):
    def trip(it, carry):
        base = it * ISSUE_TOKENS
        rows = [dest_ref[0, 0, (base + u) * TOP_K + kk] for u in range(ISSUE_TOKENS) for kk in range(TOP_K)]
        for u in range(ISSUE_TOKENS):
            for kk in range(TOP_K):
                start_copy(base + u, kk, rows[u * TOP_K + kk])
        return carry

    lax.fori_loop(0, n_tok // ISSUE_TOKENS, trip, 0)


DISPATCH_SLOTS = 3


FILL_UNROLL = 8


def _dispatch_body(fr_ref, dest_ref, u_ref, xs_out, stage_ref, zero_ref, sems, fill_sem, *, tm, n_fill):
    i = pl.program_id(0)
    last = pl.num_programs(0) - 1
    slot = i % DISPATCH_SLOTS

    def wait_step(s):
        for _ in range(TOP_K):
            pltpu.make_async_copy(stage_ref.at[s], xs_out.at[pl.ds(0, tm * V7X_SUBLANES), :], sems.at[s]).wait()

    @pl.when(i == 0)
    def _():
        zero_ref[...] = jnp.zeros_like(zero_ref)

        def trip(it, carry):
            rows = [fr_ref[it * FILL_UNROLL + u] for u in range(FILL_UNROLL)]
            for u, row in enumerate(rows):
                pltpu.make_async_copy(zero_ref, _row_tile(xs_out, row), fill_sem).start(priority=u % DMA_QUEUES)
            return carry

        lax.fori_loop(0, n_fill // FILL_UNROLL, trip, 0)

    @pl.when(i >= DISPATCH_SLOTS - 1)
    def _():
        wait_step((i + 1) % DISPATCH_SLOTS)

    _to_row_tiles(stage_ref.at[slot], u_ref[...])

    def start_copy(r, kk, row):
        pltpu.make_async_copy(_row_tile(stage_ref.at[slot], r), _row_tile(xs_out, row),
                              sems.at[slot]).start(priority=kk % DMA_QUEUES)

    _issue_rows(dest_ref, tm, start_copy)

    @pl.when(i == last)
    def _():
        for back in range(DISPATCH_SLOTS - 1):
            @pl.when(i >= back)
            def _():
                wait_step((i - back) % DISPATCH_SLOTS)
        pltpu.make_async_copy(xs_out.at[pl.ds(0, n_fill * V7X_SUBLANES), :],
                              xs_out.at[pl.ds(0, n_fill * V7X_SUBLANES), :], fill_sem).wait()


def _dispatch(u2, dest_sm, fill_rows, n_rows, *, tm):
    n, dm = u2.shape
    n_fill = fill_rows.shape[0]
    assert n_fill == n_rows - n * TOP_K and n_fill % FILL_UNROLL == 0
    grid_spec = pltpu.PrefetchScalarGridSpec(
        num_scalar_prefetch=1,
        grid=(n // tm,),
        in_specs=[pl.BlockSpec((1, 1, tm * TOP_K), lambda i, fr: (i, 0, 0), memory_space=pltpu.SMEM),
                  pl.BlockSpec((tm, dm), lambda i, fr: (i, 0))],
        out_specs=pl.BlockSpec(memory_space=pl.ANY),
        scratch_shapes=[pltpu.VMEM((DISPATCH_SLOTS, tm * V7X_SUBLANES, V7X_LANES), F32),
                        pltpu.VMEM((V7X_SUBLANES, V7X_LANES), F32),
                        pltpu.SemaphoreType.DMA((DISPATCH_SLOTS,)), pltpu.SemaphoreType.DMA(())],
    )
    return pl.pallas_call(
        functools.partial(_dispatch_body, tm=tm, n_fill=n_fill),
        grid_spec=grid_spec,
        out_shape=jax.ShapeDtypeStruct((n_rows * V7X_SUBLANES, V7X_LANES), F32),
        compiler_params=_cparams(("arbitrary",), 32),
        name="dispatch",
    )(fill_rows, dest_sm, u2)


def _gmm_body(be_ref, bv_ref, bf_ref, bs_ref, bn_ref, xs_ref, wu_hbm, bu_ref, wd_hbm, bd_ref, ys_ref,
              wu_f32, wd_f32, wu_bf, wd_bf, sems, *, bm):
    j = pl.program_id(0)
    e = be_ref[j]
    s = bs_ref[j]
    d_ff = wd_hbm.shape[1]
    chunk = 64

    def fetch(ex, slot):
        return (pltpu.make_async_copy(wu_hbm.at[ex], wu_f32.at[slot], sems.at[0, slot]),
                pltpu.make_async_copy(wd_hbm.at[ex], wd_f32.at[slot], sems.at[1, slot]))

    @pl.when(bf_ref[j] != 0)
    def _():
        @pl.when(j == 0)
        def _():
            for c in fetch(e, s):
                c.start()

        for c in fetch(e, s):
            c.wait()

        @pl.when(bn_ref[j] >= 0)
        def _():
            for c in fetch(bn_ref[j], 1 - s):
                c.start(priority=DMA_QUEUES - 1)

        def cast(c, carry):
            sl = pl.ds(pl.multiple_of(c * chunk, chunk), chunk)
            wu_bf[sl, :] = wu_f32[s, sl, :].astype(BF16)
            wd_bf[sl, :] = wd_f32[s, sl, :].astype(BF16)
            return carry

        lax.fori_loop(0, wu_hbm.shape[1] // chunk, cast, 0)

    @pl.when(bv_ref[j] != 0)
    def _():
        for g in range(bm // EXPERT_ROWS):
            tiles = pl.ds(g * EXPERT_ROWS * V7X_SUBLANES, EXPERT_ROWS * V7X_SUBLANES)
            x = _from_row_tiles(xs_ref.at[tiles, :], EXPERT_ROWS).astype(BF16)
            glu = jnp.dot(x, wu_bf[:, :d_ff], preferred_element_type=F32) + bu_ref[0, :, :d_ff]
            lin = jnp.dot(x, wu_bf[:, d_ff:], preferred_element_type=F32) + bu_ref[0, :, d_ff:]
            glu = jnp.minimum(glu, SWIGLU_LIMIT)
            lin = jnp.clip(lin, -SWIGLU_LIMIT, SWIGLU_LIMIT)
            act = glu * (1.0 / (1.0 + jnp.exp(-SWIGLU_ALPHA * glu))) * (lin + 1.0)
            _to_row_tiles(ys_ref.at[tiles, :],
                          jnp.dot(act.astype(BF16), wd_bf[...], preferred_element_type=F32) + bd_ref[0])

    @pl.when(bv_ref[j] == 0)
    def _():
        ys_ref[...] = jnp.zeros_like(ys_ref)


def _gmm(xs, blk_e, blk_valid, blk_first, blk_slot, blk_next, w_up, b_up, w_down, b_down, *, bm):
    n_rows = xs.shape[0] // V7X_SUBLANES
    n_e, dm, d_up = w_up.shape
    d_ff = w_down.shape[1]
    assert d_ff == dm
    row_tiles = pl.BlockSpec((bm * V7X_SUBLANES, V7X_LANES), lambda j, *_: (j, 0))
    grid_spec = pltpu.PrefetchScalarGridSpec(
        num_scalar_prefetch=5,
        grid=(n_rows // bm,),
        in_specs=[row_tiles,
                  pl.BlockSpec(memory_space=pl.ANY),
                  pl.BlockSpec((1, 1, d_up), lambda j, be, *_: (be[j], 0, 0)),
                  pl.BlockSpec(memory_space=pl.ANY),
                  pl.BlockSpec((1, 1, dm), lambda j, be, *_: (be[j], 0, 0))],
        out_specs=row_tiles,
        scratch_shapes=[pltpu.VMEM((2, dm, d_up), F32), pltpu.VMEM((2, d_ff, dm), F32),
                        pltpu.VMEM((dm, d_up), BF16), pltpu.VMEM((d_ff, dm), BF16),
                        pltpu.SemaphoreType.DMA((2, 2))],
    )
    return pl.pallas_call(
        functools.partial(_gmm_body, bm=bm),
        grid_spec=grid_spec,
        out_shape=jax.ShapeDtypeStruct(xs.shape, F32),
        compiler_params=_cparams(("arbitrary",), 56),
        name="expert_mlp",
    )(blk_e, blk_valid, blk_first, blk_slot, blk_next, xs, w_up, b_up.reshape(n_e, 1, d_up), w_down,
      b_down.reshape(n_e, 1, dm))


def _combine_body(rt_ref, rtn_ref, ys_hbm, rg_ref, x1_ref, g2_ref, lg_ref, lb_ref, y_ref, rows_ref, sems,
                  *, nb, tm, alpha, n_steps):
    n_tok = nb * tm
    i = pl.program_id(0)
    slot = i % 2

    def issue(dest_ref, s):
        def start_copy(r, kk, row):
            pltpu.make_async_copy(_row_tile(ys_hbm, row), _row_tile(rows_ref.at[s, kk], r),
                                  sems.at[s]).start(priority=kk % DMA_QUEUES)

        _issue_rows(dest_ref, n_tok, start_copy)

    @pl.when(i == 0)
    def _():
        issue(rt_ref, 0)

    @pl.when(i + 1 < n_steps)
    def _():
        issue(rtn_ref, 1 - slot)

    for kk in range(TOP_K):
        pltpu.make_async_copy(ys_hbm.at[pl.ds(0, n_tok * V7X_SUBLANES), :], rows_ref.at[slot, kk],
                              sems.at[slot]).wait()

    gates = rg_ref[...]
    ffn = gates[:, 0:1] * _from_row_tiles(rows_ref, n_tok, (slot, 0))
    for kk in range(1, TOP_K):
        ffn = ffn + gates[:, kk:kk + 1] * _from_row_tiles(rows_ref, n_tok, (slot, kk))
    dm = ffn.shape[-1]
    y = alpha * x1_ref[...].reshape(nb, tm, dm) + (1.0 + g2_ref[...]) * ffn.reshape(nb, tm, dm)
    y_ref[...] = _layer_norm(y, lg_ref[...], lb_ref[...])


def _combine(ys, dest_sm, route_g, x1_all, ada3, ln2_g, ln2_b, *, bsz, seq, nb, tm, alpha, row0):
    dm = x1_all.shape[1]
    n_tok = nb * tm
    assert row0 % n_tok == 0 and dest_sm.shape[2] == n_tok * TOP_K and n_tok % ISSUE_TOKENS == 0
    blk0 = row0 // n_tok
    st = seq // tm
    n_steps = (bsz // nb) * st
    dest = lambda off: pl.BlockSpec((1, 1, n_tok * TOP_K),
                                    lambda i: (blk0 + jnp.minimum(i + off, n_steps - 1), 0, 0),
                                    memory_space=pltpu.SMEM)
    return pl.pallas_call(
        functools.partial(_combine_body, nb=nb, tm=tm, alpha=alpha, n_steps=n_steps),
        grid=(n_steps,),
        in_specs=[dest(0), dest(1),
                  pl.BlockSpec(memory_space=pl.ANY),
                  pl.BlockSpec((n_tok, V7X_LANES), lambda i: (blk0 + i, 0)),
                  pl.BlockSpec((n_tok, dm), lambda i: (blk0 + i, 0)),
                  pl.BlockSpec((nb, 1, dm), lambda i: (i // st, 0, _GATE2)),
                  pl.BlockSpec((1, dm), lambda i: (0, 0)),
                  pl.BlockSpec((1, dm), lambda i: (0, 0))],
        out_specs=pl.BlockSpec((nb, tm, dm), lambda i: (i // st, i % st, 0)),
        scratch_shapes=[pltpu.VMEM((2, TOP_K, n_tok * V7X_SUBLANES, V7X_LANES), F32),
                        pltpu.SemaphoreType.DMA((2,))],
        out_shape=jax.ShapeDtypeStruct((bsz, seq, dm), F32),
        compiler_params=_cparams(("arbitrary",), 48),
        name="combine_prompt" if row0 == 0 else "combine_sample",
    )(dest_sm, dest_sm, ys, route_g, x1_all, ada3, ln2_g.reshape(1, dm), ln2_b.reshape(1, dm))


def _layer(xp, xs, cache_k, cache_v, conv_past, cp, cs, w_ada, b_ada, w_in, conv_w, norm_attn_g, norm_conv_g,
           w_o, ln1_g, ln1_b, bias_tabs, router_w, router_b, w_up, b_up, w_down, b_down, ln2_g, ln2_b, alpha):
    bp, sp, dm = xp.shape
    bs, ts, _ = xs.shape
    n_p, n_s = bp * sp, bs * ts
    n_tok = n_p + n_s
    assert n_p % TOK_TILE == 0 and n_s == TOK_TILE and sp % ROW_TILE == 0 and sp % OUT_TILE == 0
    bias_p, bias_c, bias_n = bias_tabs

    ada = _ada(jnp.concatenate([cp, cs], axis=0), w_ada, b_ada)
    ada_p = ada[:bp].reshape(bp, 1, -1)
    ada_s = ada[bp:].reshape(bs, 1, -1)
    w_in_bf = w_in.astype(BF16)
    w_o_bf = w_o.astype(BF16)

    qp, kp, vp, convp, cstp, kpt, vpt = _inproj(xp, ada_p, w_in_bf, conv_w, norm_conv_g, None, nb=1, tm=ROW_TILE)
    attn_p = _attn_prompt(qp, kp, vp, bias_p)
    x1_all, u2_all = _outproj(attn_p, convp, xp, ada_p, w_o_bf, norm_attn_g, ln1_g, ln1_b,
                              nb=1, tm=OUT_TILE, alpha=alpha, n_total=n_tok, row0=0)
    qs, ks, vs, convs, csts = _inproj(xs, ada_s, w_in_bf, conv_w, norm_conv_g, conv_past, nb=bs, tm=ts)
    heads = lambda a: jnp.transpose(a.reshape(bs, ts, N_HEADS, HEAD_DIM), (0, 2, 1, 3))
    n_past = cache_k.shape[1]
    feat_major = lambda c: jnp.transpose(c.reshape(bs, n_past, D_ATTN), (0, 2, 1))
    attn_s = _attn_sample(heads(qs), heads(ks), heads(vs), feat_major(cache_k), feat_major(cache_v), bias_c, bias_n)
    attn_s = jnp.transpose(attn_s, (0, 2, 1, 3)).reshape(bs, ts, D_ATTN)
    x1_all, u2_all = _outproj(attn_s, convs, xs, ada_s, w_o_bf, norm_attn_g, ln1_g, ln1_b,
                              nb=bs, tm=ts, alpha=alpha, n_total=n_tok, row0=n_p, prev=(x1_all, u2_all))

    route_i, route_r, route_g, counts = _router(u2_all, router_w, router_b, tm=TOK_TILE)
    counts = counts[0, :N_EXPERTS].astype(I32)
    padded = ((counts + EXPERT_BLK - 1) // EXPERT_BLK) * EXPERT_BLK
    pad_ends = jnp.cumsum(padded)
    pad_starts = (pad_ends - padded).astype(I32)
    n_blocks = n_tok * TOP_K // EXPERT_BLK + N_EXPERTS
    n_rows = n_blocks * EXPERT_BLK
    blk_row = jnp.arange(n_blocks, dtype=I32) * EXPERT_BLK
    blk_valid = (blk_row < pad_ends[-1]).astype(I32)
    blk_e = jnp.sum((blk_row[:, None] >= pad_ends[None, :]).astype(I32), axis=1)
    last_e = jnp.sum((pad_ends[-1] - 1 >= pad_ends).astype(I32))
    blk_e = jnp.where(blk_valid != 0, blk_e, last_e).astype(I32)
    ar = jnp.arange(N_EXPERTS, dtype=I32)
    has_rows = padded > 0
    slot_of_e = (jnp.cumsum(has_rows.astype(I32)) - 1) % 2
    next_of_e = jnp.min(jnp.where((ar[None, :] > ar[:, None]) & has_rows[None, :], ar[None, :], N_EXPERTS), axis=1)
    next_of_e = jnp.where(next_of_e < N_EXPERTS, next_of_e, -1)
    is_e = blk_e[:, None] == ar[None, :]
    pick = lambda per_expert: jnp.sum(jnp.where(is_e, per_expert[None, :], 0), axis=1).astype(I32)
    blk_first = (blk_valid * (blk_row == pick(pad_starts)).astype(I32)).astype(I32)
    blk_slot = pick(slot_of_e)
    blk_next = pick(next_of_e)
    dense = lambda a: a[:, :TOP_K].reshape(n_tok * TOP_K // V7X_LANES, V7X_LANES)
    dest = _dest_rows(dense(route_i), dense(route_r), pad_starts)
    dest_sm = dest.reshape(n_tok // TOK_TILE, 1, TOK_TILE * TOP_K)

    run_start = jnp.concatenate([pad_starts + counts, pad_ends[-1:]])
    run_len = jnp.concatenate([padded - counts, n_rows - pad_ends[-1:]])
    run_k0 = jnp.cumsum(run_len) - run_len
    k = jnp.arange(n_rows - n_tok * TOP_K, dtype=I32)[:, None]
    in_run = (k >= run_k0[None, :]) & (k < (run_k0 + run_len)[None, :])
    fill_rows = jnp.sum(jnp.where(in_run, run_start[None, :] + k - run_k0[None, :], 0), axis=1).astype(I32)
    x_sorted = _dispatch(u2_all, dest_sm, fill_rows, n_rows, tm=TOK_TILE)
    y_sorted = _gmm(x_sorted, blk_e, blk_valid, blk_first, blk_slot, blk_next, w_up, b_up, w_down, b_down,
                    bm=EXPERT_BLK)
    yp = _combine(y_sorted, dest_sm, route_g, x1_all, ada_p, ln2_g, ln2_b,
                  bsz=bp, seq=sp, nb=1, tm=TOK_TILE, alpha=alpha, row0=0)
    ys_out = _combine(y_sorted, dest_sm, route_g, x1_all, ada_s, ln2_g, ln2_b,
                      bsz=bs, seq=ts, nb=bs, tm=ts, alpha=alpha, row0=n_p)
    return yp, ys_out, kpt, vpt, cstp, ks, vs, csts


def kernel(x_prompt, x_sample, cache_k, cache_v, state_conv, c_prompt, c_sample, w_ada, b_ada, w_in, conv_w,
           norm_attn_g, norm_conv_g, w_o, ln1_g, ln1_b, rel_bias, router_w, router_b, w_up, b_up, w_down, b_down,
           ln2_g, ln2_b):
    depth = w_ada.shape[0]
    alpha = (2 * depth) ** 0.25
    xp, xs = x_prompt, x_sample
    bp, sp, _ = xp.shape
    bs, ts, _ = xs.shape
    n_keep = min(BRANCHES[-1][0], sp)
    bias_tabs = _bias_tables(rel_bias, ts, cache_k.shape[2])
    outs = [[] for _ in range(6)]
    for l in range(depth):
        xp, xs, kp, vp, cstp, ks, vs, csts = _layer(
            xp, xs, cache_k[l], cache_v[l], state_conv[l], c_prompt, c_sample, w_ada[l], b_ada[l], w_in[l],
            conv_w[l], norm_attn_g[l], norm_conv_g[l], w_o[l], ln1_g[l], ln1_b[l], bias_tabs, router_w[l],
            router_b[l], w_up[l], b_up[l], w_down[l], b_down[l], ln2_g[l], ln2_b[l], alpha)
        kp = jnp.transpose(kp, (0, 2, 1)).reshape(bp, sp, N_HEADS, HEAD_DIM)[:, -n_keep:]
        vp = jnp.transpose(vp, (0, 2, 1)).reshape(bp, sp, N_HEADS, HEAD_DIM)[:, -n_keep:]
        for lst, val in zip(outs, (kp, vp, cstp, ks.reshape(bs, ts, N_HEADS, HEAD_DIM),
                                   vs.reshape(bs, ts, N_HEADS, HEAD_DIM), csts)):
            lst.append(val)
    return (xp, xs) + tuple(jnp.stack(o) for o in outs)
```

```python
import functools
import math

import numpy as np
import jax
import jax.numpy as jnp
from jax import lax
from jax.experimental import pallas as pl
from jax.experimental.pallas import tpu as pltpu

F32 = jnp.float32
BF16 = jnp.bfloat16
I32 = jnp.int32

HEAD_DIM = 64
N_HEADS = 12
D_ATTN = N_HEADS * HEAD_DIM
D_CONV = 256
CONV_WIDTH = 3
BRANCHES = ((128, 1), (512, 4), (2048, 16))
NUM_BUCKETS = 32
MAX_DISTANCE = 2048
N_EXPERTS = 32
TOP_K = 4
SWIGLU_LIMIT = 7.0
SWIGLU_ALPHA = 1.702
LN_EPS = 1e-5
RMS_EPS = 1e-6
NEG_INF = -1e30
LOG2E = math.log2(math.e)

V7X_LANES = 128
V7X_SUBLANES = 8
V7X_VMEM_BYTES = 64 * 1024 * 1024
DMA_QUEUES = 2

Q_BLK = 128
PAIR = 2 * HEAD_DIM
N_PAIRS = N_HEADS // 2
TOK_TILE = 256
ROW_TILE = 512
OUT_TILE = 512
EXPERT_BLK = 256
EXPERT_ROWS = 256
ATTN_UNROLL = 16


def _cparams(sem, vmem_mb):
    return pltpu.CompilerParams(dimension_semantics=sem, vmem_limit_bytes=vmem_mb * 1024 * 1024)


def _t5_bucket_np(dist):
    dist = np.asarray(dist, np.int64)
    max_exact = NUM_BUCKETS // 2
    df = np.maximum(dist, max_exact).astype(np.float32)
    large = max_exact + (np.log(df / np.float32(max_exact)) / np.float32(math.log(MAX_DISTANCE / max_exact))
                         * np.float32(NUM_BUCKETS - max_exact)).astype(np.int32)
    return np.where(dist < max_exact, dist, np.minimum(large, NUM_BUCKETS - 1)).astype(np.int32)


def _prompt_bucket_index():
    a = np.arange(Q_BLK)[:, None]
    c = np.arange(2 * Q_BLK)[None, :]
    step = Q_BLK + a - c
    valid = (step >= 0) & (step <= Q_BLK)
    out = []
    for _, d in BRANCHES:
        out.append(np.where(valid, _t5_bucket_np(np.clip(step, 0, Q_BLK) * d), -1))
    return np.stack(out).astype(np.int32)


def _sample_bucket_index(n_new, n_past):
    t = np.arange(n_new)[:, None]
    out_c, out_n = [], []
    for w, d in BRANCHES:
        dist_c = n_past + t - np.arange(n_past)[None, :]
        dist_n = t - np.arange(n_new)[None, :]
        for dist, out in ((dist_c, out_c), (dist_n, out_n)):
            valid = (dist >= 0) & (dist <= w) & (dist % d == 0)
            out.append(np.where(valid, _t5_bucket_np(np.clip(dist, 0, w)), -1))
    return np.stack(out_c).astype(np.int32), np.stack(out_n).astype(np.int32)


def _bias_body(rb_ref, idx_ref, out_ref, *, mult, hide_cols):
    idx = idx_ref[0]
    col = lax.broadcasted_iota(I32, idx.shape, 1)
    for h in range(N_HEADS):
        acc = jnp.where(idx < 0, NEG_INF, 0.0).astype(F32)
        for b in range(NUM_BUCKETS):
            acc = acc + jnp.where(idx == b, rb_ref[b * N_HEADS + h] * mult, 0.0)
        out_ref[0, 0, h] = acc
        if hide_cols:
            out_ref[0, 1, h] = jnp.where(col < hide_cols, NEG_INF, acc)


def _bias_expand(rel_bias, idx_np, name, mult=1.0, hide_cols=0):
    nbr, r, c = idx_np.shape
    nvar = 2 if hide_cols else 1
    return pl.pallas_call(
        functools.partial(_bias_body, mult=mult, hide_cols=hide_cols),
        grid=(nbr,),
        in_specs=[pl.BlockSpec(memory_space=pltpu.SMEM),
                  pl.BlockSpec((1, r, c), lambda i: (i, 0, 0))],
        out_specs=pl.BlockSpec((1, nvar, N_HEADS, r, c), lambda i: (i, 0, 0, 0, 0)),
        out_shape=jax.ShapeDtypeStruct((nbr, nvar, N_HEADS, r, c), F32),
        name=name,
    )(rel_bias.reshape(-1), jnp.asarray(idx_np))


def _bias_tables(rel_bias, n_new, n_past):
    bias_p = _bias_expand(rel_bias, _prompt_bucket_index(), "bias_prompt", mult=LOG2E, hide_cols=Q_BLK)
    bias_p = bias_p.reshape(len(BRANCHES), 2, N_PAIRS, 2 * Q_BLK, 2 * Q_BLK)
    ic, inw = _sample_bucket_index(n_new, n_past)
    plain = lambda idx, name: _bias_expand(rel_bias, idx, name).reshape((idx.shape[0], N_HEADS) + idx.shape[1:])
    return bias_p, plain(ic, "bias_cache"), plain(inw, "bias_new")


def _ada_body(c_ref, w_ref, b_ref, o_ref):
    c = c_ref[...]
    s = c * (1.0 / (1.0 + jnp.exp(-c)))
    o_ref[...] = jnp.dot(s.astype(BF16), w_ref[...].astype(BF16), preferred_element_type=F32) + b_ref[...]


def _ada(c_all, w_ada, b_ada):
    n, dm = c_all.shape
    n_out = w_ada.shape[1]
    tn = dm
    return pl.pallas_call(
        _ada_body,
        grid=(n_out // tn,),
        in_specs=[pl.BlockSpec((n, dm), lambda j: (0, 0)),
                  pl.BlockSpec((dm, tn), lambda j: (0, j)),
                  pl.BlockSpec((1, tn), lambda j: (0, j))],
        out_specs=pl.BlockSpec((n, tn), lambda j: (0, j)),
        out_shape=jax.ShapeDtypeStruct((n, n_out), F32),
        compiler_params=_cparams(("parallel",), 32),
        name="ada",
    )(c_all, w_ada, b_ada.reshape(1, n_out))


_SHIFT1, _SCALE1, _GATE1, _SHIFT2, _SCALE2, _GATE2 = range(6)


def _inproj_body(*refs, nb, tm, carry):
    if carry:
        (x_ref, sh_ref, sc_ref, w_ref, cw_ref, ng_ref, past_ref,
         q_ref, k_ref, v_ref, conv_ref, cst_ref, kt_ref, vt_ref, zz_ref) = refs
    else:
        (x_ref, sh_ref, sc_ref, w_ref, cw_ref, ng_ref, past_ref,
         q_ref, k_ref, v_ref, conv_ref, cst_ref, zz_ref) = refs
    dm = x_ref.shape[-1]
    rows = nb * tm
    u = x_ref[...] * (1.0 + sc_ref[...]) + sh_ref[...]
    u = u.reshape(rows, dm).astype(BF16)

    def proj(lo, width):
        return jnp.dot(u, w_ref[:, lo:lo + width], preferred_element_type=F32)

    q_ref[...] = proj(0, D_ATTN).reshape(nb, tm, D_ATTN)
    k = proj(D_ATTN, D_ATTN)
    v = proj(2 * D_ATTN, D_ATTN)
    k_ref[...] = k.reshape(nb, tm, D_ATTN)
    v_ref[...] = v.reshape(nb, tm, D_ATTN)
    if carry:
        kt_ref[0] = k.T
        vt_ref[0] = v.T
    gb = proj(3 * D_ATTN, D_CONV)
    gc = proj(3 * D_ATTN + D_CONV, D_CONV)
    hh = proj(3 * D_ATTN + 2 * D_CONV, D_CONV)
    z = (gc * hh).reshape(nb, tm, D_CONV)

    if carry:
        s = pl.program_id(1)

        @pl.when(s == 0)
        def _():
            zz_ref[:, 0:V7X_SUBLANES, :] = jnp.zeros((nb, V7X_SUBLANES, D_CONV), F32)

        @pl.when(s > 0)
        def _():
            zz_ref[:, 0:V7X_SUBLANES, :] = zz_ref[:, tm:tm + V7X_SUBLANES, :]
    else:
        zz_ref[:, V7X_SUBLANES - 2:V7X_SUBLANES, :] = past_ref[...]
    zz_ref[:, V7X_SUBLANES:, :] = z

    cw = cw_ref[...]
    yc = (cw[0:1, :] * zz_ref[:, V7X_SUBLANES - 2:V7X_SUBLANES - 2 + tm, :]
          + cw[1:2, :] * zz_ref[:, V7X_SUBLANES - 1:V7X_SUBLANES - 1 + tm, :]
          + cw[2:3, :] * z)
    g = gb.reshape(nb, tm, D_CONV) * yc
    ms = jnp.mean(g * g, axis=-1, keepdims=True)
    conv_ref[...] = g * lax.rsqrt(ms + RMS_EPS) * ng_ref[...]
    cst_ref[...] = zz_ref[:, tm + V7X_SUBLANES - 2:tm + V7X_SUBLANES, :]


def _inproj(x, ada3, w_in_bf, conv_w, norm_conv_g, past, *, nb, tm):
    bsz, seq, dm = x.shape
    carry = past is None
    if carry:
        past = jnp.zeros((bsz, CONV_WIDTH - 1, D_CONV), F32)
    grid = (bsz // nb, seq // tm)
    d_in = w_in_bf.shape[1]
    body = functools.partial(_inproj_body, nb=nb, tm=tm, carry=carry)
    tile = lambda width: pl.BlockSpec((nb, tm, width), lambda b, s: (b, s, 0))
    in_specs = [tile(dm),
                pl.BlockSpec((nb, 1, dm), lambda b, s: (b, 0, _SHIFT1)),
                pl.BlockSpec((nb, 1, dm), lambda b, s: (b, 0, _SCALE1)),
                pl.BlockSpec((dm, d_in), lambda b, s: (0, 0)),
                pl.BlockSpec((CONV_WIDTH, D_CONV), lambda b, s: (0, 0)),
                pl.BlockSpec((1, D_CONV), lambda b, s: (0, 0)),
                pl.BlockSpec((nb, CONV_WIDTH - 1, D_CONV), lambda b, s: (b, 0, 0))]
    args = [x, ada3, ada3, w_in_bf, conv_w, norm_conv_g.reshape(1, D_CONV), past]
    out_specs = [tile(D_ATTN), tile(D_ATTN), tile(D_ATTN), tile(D_CONV),
                 pl.BlockSpec((nb, CONV_WIDTH - 1, D_CONV), lambda b, s: (b, 0, 0))]
    out_shape = ([jax.ShapeDtypeStruct((bsz, seq, D_ATTN), F32)] * 3
                 + [jax.ShapeDtypeStruct((bsz, seq, D_CONV), F32),
                    jax.ShapeDtypeStruct((bsz, CONV_WIDTH - 1, D_CONV), F32)])
    if carry:
        assert nb == 1
        out_specs += [pl.BlockSpec((1, D_ATTN, tm), lambda b, s: (b, 0, s))] * 2
        out_shape += [jax.ShapeDtypeStruct((bsz, D_ATTN, seq), F32)] * 2
    return pl.pallas_call(
        body,
        grid=grid,
        in_specs=in_specs,
        out_specs=out_specs,
        out_shape=out_shape,
        scratch_shapes=[pltpu.VMEM((nb, tm + V7X_SUBLANES, D_CONV), F32)],
        compiler_params=_cparams(("parallel", "arbitrary"), 56),
        name="inproj_prompt" if carry else "inproj_sample",
    )(*args)


def _attn_prompt_body(q_ref, k_ref, v_ref, bias_ref, o_ref, ob_ref, lb_ref, *, seq):
    scale = HEAD_DIM ** -0.5 * LOG2E
    lane = lax.broadcasted_iota(I32, (Q_BLK, PAIR), 1)
    even = lane < HEAD_DIM
    ones = jnp.ones((Q_BLK, PAIR), BF16)

    def rows(ref, start, d):
        if d == 1:
            return ref[pl.ds(start, Q_BLK), :]
        return ref[pl.ds(start, Q_BLK, stride=d), :]

    def store(ref, br, start, d, val):
        if d == 1:
            ref[br, pl.ds(start, Q_BLK), :] = val
        else:
            ref[br, pl.ds(start, Q_BLK, stride=d), :] = val

    for br, (_, d) in enumerate(BRANCHES):
        nblk = seq // (d * Q_BLK)

        def block(it, carry, br=br, d=d, nblk=nblk):
            r = it // nblk
            i = it - r * nblk
            cur0 = r + d * Q_BLK * i
            qf = rows(q_ref, cur0, d) * scale
            qs = jnp.concatenate([jnp.where(even, qf, 0.0), jnp.where(even, 0.0, qf)], axis=0).astype(BF16)
            kc = rows(k_ref, cur0, d).astype(BF16)
            vc = rows(v_ref, cur0, d).astype(BF16)
            if nblk == 1:
                s = lax.dot_general(qs, kc, (((1,), (1,)), ((), ())), preferred_element_type=F32)
                s = s + bias_ref[br, 0, 0, :, Q_BLK:]
                vaug = jnp.concatenate([vc, ones], axis=1)
            else:
                prev0 = r + d * Q_BLK * jnp.maximum(i - 1, 0)
                kp = rows(k_ref, prev0, d).astype(BF16)
                vp = rows(v_ref, prev0, d).astype(BF16)
                kk = jnp.concatenate([kp, kc], axis=0)
                s = lax.dot_general(qs, kk, (((1,), (1,)), ((), ())), preferred_element_type=F32)
                s = s + bias_ref[br, jnp.where(i == 0, 1, 0), 0]
                vaug = jnp.concatenate([jnp.concatenate([vp, vc], axis=0),
                                        jnp.concatenate([ones, ones], axis=0)], axis=1)
            m = jnp.max(s, axis=-1, keepdims=True)
            p = jnp.exp2(s - m).astype(BF16)
            out = jnp.dot(p, vaug, preferred_element_type=F32)
            num, den = out[:, :PAIR], out[:, PAIR:]
            o = num / den
            lse = m + jnp.log2(den)
            store(ob_ref, br, cur0, d, jnp.where(even, o[:Q_BLK], o[Q_BLK:]))
            store(lb_ref, br, cur0, d, jnp.where(even, lse[:Q_BLK], lse[Q_BLK:]))
            return carry

        lax.fori_loop(0, d * nblk, block, 0, unroll=ATTN_UNROLL)

    mt = 2 * Q_BLK

    def merge(i, carry):
        sl = pl.ds(pl.multiple_of(i * mt, mt), mt)
        l0, l1, l2 = lb_ref[0, sl, :], lb_ref[1, sl, :], lb_ref[2, sl, :]
        mx = jnp.maximum(jnp.maximum(l0, l1), l2)
        w0, w1, w2 = jnp.exp2(l0 - mx), jnp.exp2(l1 - mx), jnp.exp2(l2 - mx)
        acc = w0 * ob_ref[0, sl, :] + w1 * ob_ref[1, sl, :] + w2 * ob_ref[2, sl, :]
        o_ref[sl, :] = acc / (w0 + w1 + w2)
        return carry

    lax.fori_loop(0, seq // mt, merge, 0)


def _attn_prompt(q, k, v, bias_p):
    bsz, seq, _ = q.shape
    assert seq % (BRANCHES[-1][1] * Q_BLK) == 0
    body = functools.partial(_attn_prompt_body, seq=seq)
    col = pl.BlockSpec((None, seq, PAIR), lambda b, j: (b, 0, j))
    return pl.pallas_call(
        body,
        grid=(bsz, N_PAIRS),
        in_specs=[col, col, col,
                  pl.BlockSpec((len(BRANCHES), 2, 1, 2 * Q_BLK, 2 * Q_BLK), lambda b, j: (0, 0, j, 0, 0))],
        out_specs=col,
        out_shape=jax.ShapeDtypeStruct((bsz, seq, D_ATTN), F32),
        scratch_shapes=[pltpu.VMEM((len(BRANCHES), seq, PAIR), F32),
                        pltpu.VMEM((len(BRANCHES), seq, PAIR), F32)],
        compiler_params=_cparams(("parallel", "parallel"), 48),
        name="attn_prompt",
    )(q, k, v, bias_p)


def _attn_sample_body(q_ref, kn_ref, vn_ref, ckt_ref, cvt_ref, bc_ref, bn_ref, o_ref):
    scale = HEAD_DIM ** -0.5
    nt = (((1,), (1,)), ((), ()))
    nbr = len(BRANCHES)
    for h in range(N_HEADS):
        rows = slice(h * HEAD_DIM, (h + 1) * HEAD_DIM)
        qh = (q_ref[h] * scale).astype(BF16)
        s_c = jnp.dot(qh, ckt_ref[rows, :].astype(BF16), preferred_element_type=F32)
        s_n = lax.dot_general(qh, kn_ref[h].astype(BF16), nt, preferred_element_type=F32)
        ps, lses, dens = [], [], []
        for br in range(nbr):
            lc = s_c + bc_ref[br, h]
            ln = s_n + bn_ref[br, h]
            m = jnp.maximum(jnp.max(lc, axis=-1, keepdims=True), jnp.max(ln, axis=-1, keepdims=True))
            pc = jnp.exp(lc - m)
            pn = jnp.exp(ln - m)
            den = jnp.sum(pc, axis=-1, keepdims=True) + jnp.sum(pn, axis=-1, keepdims=True)
            ps.append((pc, pn))
            dens.append(den)
            lses.append(m + jnp.log(den))
        mx = jnp.maximum(jnp.maximum(lses[0], lses[1]), lses[2])
        ws = [jnp.exp(l - mx) for l in lses]
        wsum = ws[0] + ws[1] + ws[2]
        coefs = [ws[br] / (wsum * dens[br]) for br in range(nbr)]
        p_c = coefs[0] * ps[0][0] + coefs[1] * ps[1][0] + coefs[2] * ps[2][0]
        p_n = coefs[0] * ps[0][1] + coefs[1] * ps[1][1] + coefs[2] * ps[2][1]
        o = lax.dot_general(p_c.astype(BF16), cvt_ref[rows, :].astype(BF16), nt, preferred_element_type=F32)
        o_ref[h] = o + jnp.dot(p_n.astype(BF16), vn_ref[h].astype(BF16), preferred_element_type=F32)


def _attn_sample(q, k_new, v_new, cache_kt, cache_vt, bias_c, bias_n):
    bsz, _, n_new, _ = q.shape
    n_past = cache_kt.shape[2]
    new = pl.BlockSpec((None, N_HEADS, n_new, HEAD_DIM), lambda b: (b, 0, 0, 0))
    cache = pl.BlockSpec((None, D_ATTN, n_past), lambda b: (b, 0, 0))
    return pl.pallas_call(
        _attn_sample_body,
        grid=(bsz,),
        in_specs=[new, new, new, cache, cache,
                  pl.BlockSpec(bias_c.shape, lambda b: (0, 0, 0, 0)), pl.BlockSpec(bias_n.shape, lambda b: (0, 0, 0, 0))],
        out_specs=new,
        out_shape=jax.ShapeDtypeStruct((bsz, N_HEADS, n_new, HEAD_DIM), F32),
        compiler_params=_cparams(("parallel",), 48),
        name="attn_sample",
    )(q, k_new, v_new, cache_kt, cache_vt, bias_c, bias_n)


def _layer_norm(y, g, b):
    mu = jnp.mean(y, axis=-1, keepdims=True)
    c = y - mu
    var = jnp.mean(c * c, axis=-1, keepdims=True)
    return c * lax.rsqrt(var + LN_EPS) * g + b


def _to_row_tiles(ref, val):
    rows, dm = val.shape
    assert dm == V7X_SUBLANES * V7X_LANES
    for c in range(V7X_SUBLANES):
        ref[pl.ds(c, rows, stride=V7X_SUBLANES), :] = val[:, c * V7X_LANES:(c + 1) * V7X_LANES]


def _from_row_tiles(ref, rows, lead=()):
    chunks = [ref[lead + (pl.ds(c, rows, stride=V7X_SUBLANES), slice(None))] for c in range(V7X_SUBLANES)]
    return jnp.concatenate(chunks, axis=1)


def _outproj_body(*refs, nb, tm, alpha, aliased, n_main):
    if aliased:
        refs = refs[2:]
    (a_ref, c_ref, x_ref, g1_ref, sh2_ref, sc2_ref, wa_ref, wc_ref, ng_ref, lg_ref, lb_ref, x1_ref, u2_ref) = refs
    rows = nb * tm

    @pl.when(pl.program_id(0) < n_main)
    def _():
        a = a_ref[...]
        ms = jnp.mean(a * a, axis=-1, keepdims=True)
        an = (a * lax.rsqrt(ms + RMS_EPS) * ng_ref[...]).reshape(rows, D_ATTN).astype(BF16)
        cn = c_ref[...].reshape(rows, D_CONV).astype(BF16)
        mix = (jnp.dot(an, wa_ref[...], preferred_element_type=F32)
               + jnp.dot(cn, wc_ref[...], preferred_element_type=F32))
        dm = mix.shape[-1]
        y = alpha * x_ref[...] + (1.0 + g1_ref[...]) * mix.reshape(nb, tm, dm)
        x1 = _layer_norm(y, lg_ref[...], lb_ref[...])
        x1_ref[...] = x1.reshape(rows, dm)
        u2_ref[...] = (x1 * (1.0 + sc2_ref[...]) + sh2_ref[...]).reshape(rows, dm)

    @pl.when(pl.program_id(0) >= n_main)
    def _():
        x1_ref[...] = jnp.zeros_like(x1_ref)
        u2_ref[...] = jnp.zeros_like(u2_ref)


def _outproj(attn, conv, x, ada3, w_o_bf, norm_attn_g, ln1_g, ln1_b, *, nb, tm, alpha, n_total, row0, prev=None):
    bsz, seq, dm = x.shape
    rows = nb * tm
    assert row0 % rows == 0
    st = seq // tm
    blk0 = row0 // rows
    n_main = (bsz // nb) * st
    aliased = prev is not None
    n_steps = n_main if aliased else pl.cdiv(n_total, rows)

    def bs_of(i):
        j = jnp.minimum(i, n_main - 1)
        return j // st, j % st

    tile = lambda width: pl.BlockSpec((nb, tm, width), lambda i: bs_of(i) + (0,))
    ada = lambda chunk: pl.BlockSpec((nb, 1, dm), lambda i: (bs_of(i)[0], 0, chunk))
    const = lambda shape: pl.BlockSpec(shape, lambda i: (0,) * len(shape))
    out = pl.BlockSpec((rows, dm), lambda i: (blk0 + i, 0))
    body = functools.partial(_outproj_body, nb=nb, tm=tm, alpha=alpha, aliased=aliased, n_main=n_main)
    in_specs = [tile(D_ATTN), tile(D_CONV), tile(dm), ada(_GATE1), ada(_SHIFT2), ada(_SCALE2),
                const((D_ATTN, dm)), const((D_CONV, dm)), const((1, D_ATTN)), const((1, dm)), const((1, dm))]
    args = [attn, conv, x, ada3, ada3, ada3, w_o_bf[:D_ATTN], w_o_bf[D_ATTN:],
            norm_attn_g.reshape(1, D_ATTN), ln1_g.reshape(1, dm), ln1_b.reshape(1, dm)]
    kwargs = {}
    if aliased:
        in_specs = [pl.BlockSpec(memory_space=pl.ANY)] * 2 + in_specs
        args = list(prev) + args
        kwargs["input_output_aliases"] = {0: 0, 1: 1}
    return pl.pallas_call(
        body,
        grid=(n_steps,),
        in_specs=in_specs,
        out_specs=[out, out],
        out_shape=[jax.ShapeDtypeStruct((n_total, dm), F32)] * 2,
        compiler_params=_cparams(("parallel",), 48),
        name="outproj_sample" if aliased else "outproj_prompt",
        **kwargs,
    )(*args)


def _router_body(u_ref, w_ref, b_ref, ri_ref, rr_ref, rg_ref, cnt_ref, run_ref, *, tm):
    i = pl.program_id(0)

    @pl.when(i == 0)
    def _():
        run_ref[...] = jnp.zeros_like(run_ref)

    u = u_ref[...]
    u_hi = u.astype(BF16)
    u_lo = (u - u_hi.astype(F32)).astype(BF16)
    both = jnp.dot(u_hi, w_ref[...], preferred_element_type=F32)
    cross = jnp.dot(u_lo, w_ref[:, :V7X_LANES], preferred_element_type=F32)
    logits = both[:, :V7X_LANES] + both[:, V7X_LANES:] + cross + b_ref[...]
    lane_i = lax.broadcasted_iota(I32, (tm, V7X_LANES), 1)
    lane = lane_i.astype(F32)
    vals = logits
    tops, idxs = [], []
    for _ in range(TOP_K):
        mk = jnp.max(vals, axis=-1, keepdims=True)
        ik = jnp.min(jnp.where(vals == mk, lane, float(V7X_LANES)), axis=-1, keepdims=True)
        tops.append(mk)
        idxs.append(ik)
        vals = jnp.where(lane == ik, -jnp.inf, vals)
    es = [jnp.exp(t - tops[0]) for t in tops]
    den = es[0] + es[1] + es[2] + es[3]
    sel = jnp.zeros((tm, V7X_LANES), F32)
    for ik in idxs:
        sel = sel + jnp.where(lane == ik, 1.0, 0.0)
    ri = lax.broadcasted_iota(I32, (tm, tm), 0)
    ci = lax.broadcasted_iota(I32, (tm, tm), 1)
    lower = jnp.where(ci < ri, 1.0, 0.0).astype(BF16)
    before = jnp.dot(lower, sel.astype(BF16), preferred_element_type=F32) + run_ref[...]
    out_i = jnp.full((tm, V7X_LANES), -1, I32)
    out_r = jnp.zeros((tm, V7X_LANES), I32)
    out_g = jnp.zeros((tm, V7X_LANES), F32)
    for kk in range(TOP_K):
        rank = jnp.sum(jnp.where(lane == idxs[kk], before, 0.0), axis=-1, keepdims=True)
        out_i = jnp.where(lane_i == kk, idxs[kk].astype(I32), out_i)
        out_r = jnp.where(lane_i == kk, rank.astype(I32), out_r)
        out_g = jnp.where(lane_i == kk, es[kk] / den, out_g)
    ri_ref[...] = out_i
    rr_ref[...] = out_r
    rg_ref[...] = out_g
    run_ref[...] = run_ref[...] + jnp.sum(sel, axis=0, keepdims=True)
    cnt_ref[...] = run_ref[...]


def _router(u2, router_w, router_b, *, tm):
    n, dm = u2.shape
    w_pad = jnp.pad(router_w, ((0, 0), (0, V7X_LANES - N_EXPERTS)))
    w_hi = w_pad.astype(BF16)
    w_lo = (w_pad - w_hi.astype(F32)).astype(BF16)
    b_pad = jnp.pad(router_b, (0, V7X_LANES - N_EXPERTS), constant_values=NEG_INF).reshape(1, V7X_LANES)
    tok = pl.BlockSpec((tm, V7X_LANES), lambda i: (i, 0))
    return pl.pallas_call(
        functools.partial(_router_body, tm=tm),
        grid=(n // tm,),
        in_specs=[pl.BlockSpec((tm, dm), lambda i: (i, 0)),
                  pl.BlockSpec((dm, 2 * V7X_LANES), lambda i: (0, 0)),
                  pl.BlockSpec((1, V7X_LANES), lambda i: (0, 0))],
        out_specs=[tok, tok, tok, pl.BlockSpec((1, V7X_LANES), lambda i: (0, 0))],
        out_shape=[jax.ShapeDtypeStruct((n, V7X_LANES), I32), jax.ShapeDtypeStruct((n, V7X_LANES), I32),
                   jax.ShapeDtypeStruct((n, V7X_LANES), F32), jax.ShapeDtypeStruct((1, V7X_LANES), F32)],
        scratch_shapes=[pltpu.VMEM((1, V7X_LANES), F32)],
        compiler_params=_cparams(("arbitrary",), 32),
        name="router",
    )(u2, jnp.concatenate([w_hi, w_lo], axis=1), b_pad)


def _dest_body(ps_ref, ri_ref, rr_ref, d_ref):
    idx = ri_ref[...]
    acc = rr_ref[...]
    for e in range(N_EXPERTS):
        acc = acc + jnp.where(idx == e, ps_ref[e], 0)
    d_ref[...] = acc


def _dest_rows(expert, rank, pad_starts):
    return pl.pallas_call(
        _dest_body,
        in_specs=[pl.BlockSpec(memory_space=pltpu.SMEM), pl.BlockSpec(memory_space=pltpu.VMEM),
                  pl.BlockSpec(memory_space=pltpu.VMEM)],
        out_specs=pl.BlockSpec(memory_space=pltpu.VMEM),
        out_shape=jax.ShapeDtypeStruct(expert.shape, I32),
        name="dest_rows",
    )(pad_starts, expert, rank)


ISSUE_TOKENS = 2


def _row_tile(ref, row):
    return ref.at[pl.ds(pl.multiple_of(row * V7X_SUBLANES, V7X_SUBLANES), V7X_SUBLANES), :]


def _issue_rows(dest_ref, n_tok, start_copy):
    def trip(it, carry):
        base = it * ISSUE_TOKENS
        rows = [dest_ref[0, 0, (base + u) * TOP_K + kk] for u in range(ISSUE_TOKENS) for kk in range(TOP_K)]
        for u in range(ISSUE_TOKENS):
            for kk in range(TOP_K):
                start_copy(base + u, kk, rows[u * TOP_K + kk])
        return carry

    lax.fori_loop(0, n_tok // ISSUE_TOKENS, trip, 0)


DISPATCH_SLOTS = 3


FILL_UNROLL = 8


def _dispatch_body(fr_ref, dest_ref, u_ref, xs_out, stage_ref, zero_ref, sems, fill_sem, *, tm, n_fill):
    i = pl.program_id(0)
    last = pl.num_programs(0) - 1
    slot = i % DISPATCH_SLOTS

    def wait_step(s):
        for _ in range(TOP_K):
            pltpu.make_async_copy(stage_ref.at[s], xs_out.at[pl.ds(0, tm * V7X_SUBLANES), :], sems.at[s]).wait()

    @pl.when(i == 0)
    def _():
        zero_ref[...] = jnp.zeros_like(zero_ref)

        def trip(it, carry):
            rows = [fr_ref[it * FILL_UNROLL + u] for u in range(FILL_UNROLL)]
            for u, row in enumerate(rows):
                pltpu.make_async_copy(zero_ref, _row_tile(xs_out, row), fill_sem).start(priority=u % DMA_QUEUES)
            return carry

        lax.fori_loop(0, n_fill // FILL_UNROLL, trip, 0)

    @pl.when(i >= DISPATCH_SLOTS - 1)
    def _():
        wait_step((i + 1) % DISPATCH_SLOTS)

    _to_row_tiles(stage_ref.at[slot], u_ref[...])

    def start_copy(r, kk, row):
        pltpu.make_async_copy(_row_tile(stage_ref.at[slot], r), _row_tile(xs_out, row),
                              sems.at[slot]).start(priority=kk % DMA_QUEUES)

    _issue_rows(dest_ref, tm, start_copy)

    @pl.when(i == last)
    def _():
        for back in range(DISPATCH_SLOTS - 1):
            @pl.when(i >= back)
            def _():
                wait_step((i - back) % DISPATCH_SLOTS)
        pltpu.make_async_copy(xs_out.at[pl.ds(0, n_fill * V7X_SUBLANES), :],
                              xs_out.at[pl.ds(0, n_fill * V7X_SUBLANES), :], fill_sem).wait()


def _dispatch(u2, dest_sm, fill_rows, n_rows, *, tm):
    n, dm = u2.shape
    n_fill = fill_rows.shape[0]
    assert n_fill == n_rows - n * TOP_K and n_fill % FILL_UNROLL == 0
    grid_spec = pltpu.PrefetchScalarGridSpec(
        num_scalar_prefetch=1,
        grid=(n // tm,),
        in_specs=[pl.BlockSpec((1, 1, tm * TOP_K), lambda i, fr: (i, 0, 0), memory_space=pltpu.SMEM),
                  pl.BlockSpec((tm, dm), lambda i, fr: (i, 0))],
        out_specs=pl.BlockSpec(memory_space=pl.ANY),
        scratch_shapes=[pltpu.VMEM((DISPATCH_SLOTS, tm * V7X_SUBLANES, V7X_LANES), F32),
                        pltpu.VMEM((V7X_SUBLANES, V7X_LANES), F32),
                        pltpu.SemaphoreType.DMA((DISPATCH_SLOTS,)), pltpu.SemaphoreType.DMA(())],
    )
    return pl.pallas_call(
        functools.partial(_dispatch_body, tm=tm, n_fill=n_fill),
        grid_spec=grid_spec,
        out_shape=jax.ShapeDtypeStruct((n_rows * V7X_SUBLANES, V7X_LANES), F32),
        compiler_params=_cparams(("arbitrary",), 32),
        name="dispatch",
    )(fill_rows, dest_sm, u2)


def _gmm_body(be_ref, bv_ref, bf_ref, bs_ref, bn_ref, xs_ref, wu_hbm, bu_ref, wd_hbm, bd_ref, ys_ref,
              wu_f32, wd_f32, wu_bf, wd_bf, sems, *, bm):
    j = pl.program_id(0)
    e = be_ref[j]
    s = bs_ref[j]
    d_ff = wd_hbm.shape[1]
    chunk = 64

    def fetch(ex, slot):
        return (pltpu.make_async_copy(wu_hbm.at[ex], wu_f32.at[slot], sems.at[0, slot]),
                pltpu.make_async_copy(wd_hbm.at[ex], wd_f32.at[slot], sems.at[1, slot]))

    @pl.when(bf_ref[j] != 0)
    def _():
        @pl.when(j == 0)
        def _():
            for c in fetch(e, s):
                c.start()

        for c in fetch(e, s):
            c.wait()

        @pl.when(bn_ref[j] >= 0)
        def _():
            for c in fetch(bn_ref[j], 1 - s):
                c.start(priority=DMA_QUEUES - 1)

        def cast(c, carry):
            sl = pl.ds(pl.multiple_of(c * chunk, chunk), chunk)
            wu_bf[sl, :] = wu_f32[s, sl, :].astype(BF16)
            wd_bf[sl, :] = wd_f32[s, sl, :].astype(BF16)
            return carry

        lax.fori_loop(0, wu_hbm.shape[1] // chunk, cast, 0)

    @pl.when(bv_ref[j] != 0)
    def _():
        for g in range(bm // EXPERT_ROWS):
            tiles = pl.ds(g * EXPERT_ROWS * V7X_SUBLANES, EXPERT_ROWS * V7X_SUBLANES)
            x = _from_row_tiles(xs_ref.at[tiles, :], EXPERT_ROWS).astype(BF16)
            glu = jnp.dot(x, wu_bf[:, :d_ff], preferred_element_type=F32) + bu_ref[0, :, :d_ff]
            lin = jnp.dot(x, wu_bf[:, d_ff:], preferred_element_type=F32) + bu_ref[0, :, d_ff:]
            glu = jnp.minimum(glu, SWIGLU_LIMIT)
            lin = jnp.clip(lin, -SWIGLU_LIMIT, SWIGLU_LIMIT)
            act = glu * (1.0 / (1.0 + jnp.exp(-SWIGLU_ALPHA * glu))) * (lin + 1.0)
            _to_row_tiles(ys_ref.at[tiles, :],
                          jnp.dot(act.astype(BF16), wd_bf[...], preferred_element_type=F32) + bd_ref[0])

    @pl.when(bv_ref[j] == 0)
    def _():
        ys_ref[...] = jnp.zeros_like(ys_ref)


def _gmm(xs, blk_e, blk_valid, blk_first, blk_slot, blk_next, w_up, b_up, w_down, b_down, *, bm):
    n_rows = xs.shape[0] // V7X_SUBLANES
    n_e, dm, d_up = w_up.shape
    d_ff = w_down.shape[1]
    assert d_ff == dm
    row_tiles = pl.BlockSpec((bm * V7X_SUBLANES, V7X_LANES), lambda j, *_: (j, 0))
    grid_spec = pltpu.PrefetchScalarGridSpec(
        num_scalar_prefetch=5,
        grid=(n_rows // bm,),
        in_specs=[row_tiles,
                  pl.BlockSpec(memory_space=pl.ANY),
                  pl.BlockSpec((1, 1, d_up), lambda j, be, *_: (be[j], 0, 0)),
                  pl.BlockSpec(memory_space=pl.ANY),
                  pl.BlockSpec((1, 1, dm), lambda j, be, *_: (be[j], 0, 0))],
        out_specs=row_tiles,
        scratch_shapes=[pltpu.VMEM((2, dm, d_up), F32), pltpu.VMEM((2, d_ff, dm), F32),
                        pltpu.VMEM((dm, d_up), BF16), pltpu.VMEM((d_ff, dm), BF16),
                        pltpu.SemaphoreType.DMA((2, 2))],
    )
    return pl.pallas_call(
        functools.partial(_gmm_body, bm=bm),
        grid_spec=grid_spec,
        out_shape=jax.ShapeDtypeStruct(xs.shape, F32),
        compiler_params=_cparams(("arbitrary",), 56),
        name="expert_mlp",
    )(blk_e, blk_valid, blk_first, blk_slot, blk_next, xs, w_up, b_up.reshape(n_e, 1, d_up), w_down,
      b_down.reshape(n_e, 1, dm))


def _combine_body(rt_ref, rtn_ref, ys_hbm, rg_ref, x1_ref, g2_ref, lg_ref, lb_ref, y_ref, rows_ref, sems,
                  *, nb, tm, alpha, n_steps):
    n_tok = nb * tm
    i = pl.program_id(0)
    slot = i % 2

    def issue(dest_ref, s):
        def start_copy(r, kk, row):
            pltpu.make_async_copy(_row_tile(ys_hbm, row), _row_tile(rows_ref.at[s, kk], r),
                                  sems.at[s]).start(priority=DMA_QUEUES - 1)

        _issue_rows(dest_ref, n_tok, start_copy)

    @pl.when(i == 0)
    def _():
        issue(rt_ref, 0)

    @pl.when(i + 1 < n_steps)
    def _():
        issue(rtn_ref, 1 - slot)

    for kk in range(TOP_K):
        pltpu.make_async_copy(ys_hbm.at[pl.ds(0, n_tok * V7X_SUBLANES), :], rows_ref.at[slot, kk],
                              sems.at[slot]).wait()

    gates = rg_ref[...]
    ffn = gates[:, 0:1] * _from_row_tiles(rows_ref, n_tok, (slot, 0))
    for kk in range(1, TOP_K):
        ffn = ffn + gates[:, kk:kk + 1] * _from_row_tiles(rows_ref, n_tok, (slot, kk))
    dm = ffn.shape[-1]
    y = alpha * x1_ref[...].reshape(nb, tm, dm) + (1.0 + g2_ref[...]) * ffn.reshape(nb, tm, dm)
    y_ref[...] = _layer_norm(y, lg_ref[...], lb_ref[...])


def _combine(ys, dest_sm, route_g, x1_all, ada3, ln2_g, ln2_b, *, bsz, seq, nb, tm, alpha, row0):
    dm = x1_all.shape[1]
    n_tok = nb * tm
    assert row0 % n_tok == 0 and dest_sm.shape[2] == n_tok * TOP_K and n_tok % ISSUE_TOKENS == 0
    blk0 = row0 // n_tok
    st = seq // tm
    n_steps = (bsz // nb) * st
    dest = lambda off: pl.BlockSpec((1, 1, n_tok * TOP_K),
                                    lambda i: (blk0 + jnp.minimum(i + off, n_steps - 1), 0, 0),
                                    memory_space=pltpu.SMEM)
    return pl.pallas_call(
        functools.partial(_combine_body, nb=nb, tm=tm, alpha=alpha, n_steps=n_steps),
        grid=(n_steps,),
        in_specs=[dest(0), dest(1),
                  pl.BlockSpec(memory_space=pl.ANY),
                  pl.BlockSpec((n_tok, V7X_LANES), lambda i: (blk0 + i, 0)),
                  pl.BlockSpec((n_tok, dm), lambda i: (blk0 + i, 0)),
                  pl.BlockSpec((nb, 1, dm), lambda i: (i // st, 0, _GATE2)),
                  pl.BlockSpec((1, dm), lambda i: (0, 0)),
                  pl.BlockSpec((1, dm), lambda i: (0, 0))],
        out_specs=pl.BlockSpec((nb, tm, dm), lambda i: (i // st, i % st, 0)),
        scratch_shapes=[pltpu.VMEM((2, TOP_K, n_tok * V7X_SUBLANES, V7X_LANES), F32),
                        pltpu.SemaphoreType.DMA((2,))],
        out_shape=jax.ShapeDtypeStruct((bsz, seq, dm), F32),
        compiler_params=_cparams(("arbitrary",), 48),
        name="combine_prompt" if row0 == 0 else "combine_sample",
    )(dest_sm, dest_sm, ys, route_g, x1_all, ada3, ln2_g.reshape(1, dm), ln2_b.reshape(1, dm))


def _layer(xp, xs, cache_k, cache_v, conv_past, cp, cs, w_ada, b_ada, w_in, conv_w, norm_attn_g, norm_conv_g,
           w_o, ln1_g, ln1_b, bias_tabs, router_w, router_b, w_up, b_up, w_down, b_down, ln2_g, ln2_b, alpha):
    bp, sp, dm = xp.shape
    bs, ts, _ = xs.shape
    n_p, n_s = bp * sp, bs * ts
    n_tok = n_p + n_s
    assert n_p % TOK_TILE == 0 and n_s == TOK_TILE and sp % ROW_TILE == 0 and sp % OUT_TILE == 0
    bias_p, bias_c, bias_n = bias_tabs

    ada = _ada(jnp.concatenate([cp, cs], axis=0), w_ada, b_ada)
    ada_p = ada[:bp].reshape(bp, 1, -1)
    ada_s = ada[bp:].reshape(bs, 1, -1)
    w_in_bf = w_in.astype(BF16)
    w_o_bf = w_o.astype(BF16)

    qp, kp, vp, convp, cstp, kpt, vpt = _inproj(xp, ada_p, w_in_bf, conv_w, norm_conv_g, None, nb=1, tm=ROW_TILE)
    attn_p = _attn_prompt(qp, kp, vp, bias_p)
    x1_all, u2_all = _outproj(attn_p, convp, xp, ada_p, w_o_bf, norm_attn_g, ln1_g, ln1_b,
                              nb=1, tm=OUT_TILE, alpha=alpha, n_total=n_tok, row0=0)
    qs, ks, vs, convs, csts = _inproj(xs, ada_s, w_in_bf, conv_w, norm_conv_g, conv_past, nb=bs, tm=ts)
    heads = lambda a: jnp.transpose(a.reshape(bs, ts, N_HEADS, HEAD_DIM), (0, 2, 1, 3))
    n_past = cache_k.shape[1]
    feat_major = lambda c: jnp.transpose(c.reshape(bs, n_past, D_ATTN), (0, 2, 1))
    attn_s = _attn_sample(heads(qs), heads(ks), heads(vs), feat_major(cache_k), feat_major(cache_v), bias_c, bias_n)
    attn_s = jnp.transpose(attn_s, (0, 2, 1, 3)).reshape(bs, ts, D_ATTN)
    x1_all, u2_all = _outproj(attn_s, convs, xs, ada_s, w_o_bf, norm_attn_g, ln1_g, ln1_b,
                              nb=bs, tm=ts, alpha=alpha, n_total=n_tok, row0=n_p, prev=(x1_all, u2_all))

    route_i, route_r, route_g, counts = _router(u2_all, router_w, router_b, tm=TOK_TILE)
    counts = counts[0, :N_EXPERTS].astype(I32)
    padded = ((counts + EXPERT_BLK - 1) // EXPERT_BLK) * EXPERT_BLK
    pad_ends = jnp.cumsum(padded)
    pad_starts = (pad_ends - padded).astype(I32)
    n_blocks = n_tok * TOP_K // EXPERT_BLK + N_EXPERTS
    n_rows = n_blocks * EXPERT_BLK
    blk_row = jnp.arange(n_blocks, dtype=I32) * EXPERT_BLK
    blk_valid = (blk_row < pad_ends[-1]).astype(I32)
    blk_e = jnp.sum((blk_row[:, None] >= pad_ends[None, :]).astype(I32), axis=1)
    last_e = jnp.sum((pad_ends[-1] - 1 >= pad_ends).astype(I32))
    blk_e = jnp.where(blk_valid != 0, blk_e, last_e).astype(I32)
    ar = jnp.arange(N_EXPERTS, dtype=I32)
    has_rows = padded > 0
    slot_of_e = (jnp.cumsum(has_rows.astype(I32)) - 1) % 2
    next_of_e = jnp.min(jnp.where((ar[None, :] > ar[:, None]) & has_rows[None, :], ar[None, :], N_EXPERTS), axis=1)
    next_of_e = jnp.where(next_of_e < N_EXPERTS, next_of_e, -1)
    is_e = blk_e[:, None] == ar[None, :]
    pick = lambda per_expert: jnp.sum(jnp.where(is_e, per_expert[None, :], 0), axis=1).astype(I32)
    blk_first = (blk_valid * (blk_row == pick(pad_starts)).astype(I32)).astype(I32)
    blk_slot = pick(slot_of_e)
    blk_next = pick(next_of_e)
    dense = lambda a: a[:, :TOP_K].reshape(n_tok * TOP_K // V7X_LANES, V7X_LANES)
    dest = _dest_rows(dense(route_i), dense(route_r), pad_starts)
    dest_sm = dest.reshape(n_tok // TOK_TILE, 1, TOK_TILE * TOP_K)

    run_start = jnp.concatenate([pad_starts + counts, pad_ends[-1:]])
    run_len = jnp.concatenate([padded - counts, n_rows - pad_ends[-1:]])
    run_k0 = jnp.cumsum(run_len) - run_len
    k = jnp.arange(n_rows - n_tok * TOP_K, dtype=I32)[:, None]
    in_run = (k >= run_k0[None, :]) & (k < (run_k0 + run_len)[None, :])
    fill_rows = jnp.sum(jnp.where(in_run, run_start[None, :] + k - run_k0[None, :], 0), axis=1).astype(I32)
    x_sorted = _dispatch(u2_all, dest_sm, fill_rows, n_rows, tm=TOK_TILE)
    y_sorted = _gmm(x_sorted, blk_e, blk_valid, blk_first, blk_slot, blk_next, w_up, b_up, w_down, b_down,
                    bm=EXPERT_BLK)
    yp = _combine(y_sorted, dest_sm, route_g, x1_all, ada_p, ln2_g, ln2_b,
                  bsz=bp, seq=sp, nb=1, tm=TOK_TILE, alpha=alpha, row0=0)
    ys_out = _combine(y_sorted, dest_sm, route_g, x1_all, ada_s, ln2_g, ln2_b,
                      bsz=bs, seq=ts, nb=bs, tm=ts, alpha=alpha, row0=n_p)
    return yp, ys_out, kpt, vpt, cstp, ks, vs, csts


def kernel(x_prompt, x_sample, cache_k, cache_v, state_conv, c_prompt, c_sample, w_ada, b_ada, w_in, conv_w,
           norm_attn_g, norm_conv_g, w_o, ln1_g, ln1_b, rel_bias, router_w, router_b, w_up, b_up, w_down, b_down,
           ln2_g, ln2_b):
    depth = w_ada.shape[0]
    alpha = (2 * depth) ** 0.25
    xp, xs = x_prompt, x_sample
    bp, sp, _ = xp.shape
    bs, ts, _ = xs.shape
    n_keep = min(BRANCHES[-1][0], sp)
    bias_tabs = _bias_tables(rel_bias, ts, cache_k.shape[2])
    outs = [[] for _ in range(6)]
    for l in range(depth):
        xp, xs, kp, vp, cstp, ks, vs, csts = _layer(
            xp, xs, cache_k[l], cache_v[l], state_conv[l], c_prompt, c_sample, w_ada[l], b_ada[l], w_in[l],
            conv_w[l], norm_attn_g[l], norm_conv_g[l], w_o[l], ln1_g[l], ln1_b[l], bias_tabs, router_w[l],
            router_b[l], w_up[l], b_up[l], w_down[l], b_down[l], ln2_g[l], ln2_b[l], alpha)
        kp = jnp.transpose(kp, (0, 2, 1)).reshape(bp, sp, N_HEADS, HEAD_DIM)[:, -n_keep:]
        vp = jnp.transpose(vp, (0, 2, 1)).reshape(bp, sp, N_HEADS, HEAD_DIM)[:, -n_keep:]
        for lst, val in zip(outs, (kp, vp, cstp, ks.reshape(bs, ts, N_HEADS, HEAD_DIM),
                                   vs.reshape(bs, ts, N_HEADS, HEAD_DIM), csts)):
            lst.append(val)
    return (xp, xs) + tuple(jnp.stack(o) for o in outs)
```
